```python
import math
import jax
import jax.numpy as jnp
from jax import lax
import numpy as np

D_MODEL = 1024
BATCH = 2
SEQ = 8192
DEPTH = 4
DEC_BATCH = 128
DEC_SEQ = 8
PAST_LEN = 2048
PAGE_SIZE = 128

N_HEADS = 16
HEAD_DIM = D_MODEL // N_HEADS
ATTN_SCALE = HEAD_DIM ** -0.5
D_FF = 4 * D_MODEL
N_A_LAYERS = DEPTH // 2
N_B_LAYERS = DEPTH - N_A_LAYERS
EPS = 1e-6
N_BUCKETS = 32
MAX_DISTANCE = 2048
NSA_KV_HEADS = 4
CMP_LEN = 32
CMP_STRIDE = 16
CMP_HIDDEN = 2 * HEAD_DIM
SEL_BLOCK = 64
SEL_TOP_N = 16
SEL_Q_BLOCK = 64
NSA_WINDOW = 512
N_BRANCHES = 3
NSA_SPLITS = [N_HEADS * HEAD_DIM + i * NSA_KV_HEADS * HEAD_DIM for i in range(7)]
NSA_IN = N_HEADS * HEAD_DIM + 6 * NSA_KV_HEADS * HEAD_DIM + N_BRANCHES * N_HEADS
DIL_KV_HEADS = 4
DIL_GROUPS = ((128, 1), (512, 4), (2048, 16))
N_DIL_GROUPS = len(DIL_GROUPS)
DIL_WINDOW_MAX = max(w for w, _ in DIL_GROUPS)
BAND_BLOCK = 128

kernel_name = 'yoco_nsa_dilated_decoder_step'


def rms_norm(x, g):
    xf = x.astype(jnp.float32)
    y = xf * lax.rsqrt(jnp.mean(xf * xf, axis=-1, keepdims=True) + EPS)
    return (y * g.astype(jnp.float32)).astype(x.dtype)


def rel_bucket(dist):
    max_exact = N_BUCKETS // 2
    d = jnp.maximum(dist, 0)
    ratio = jnp.log(jnp.maximum(d, 1).astype(jnp.float32) / max_exact) / math.log(MAX_DISTANCE / max_exact)
    large = max_exact + (ratio * (N_BUCKETS - max_exact)).astype(jnp.int32)
    return jnp.where(d < max_exact, d, jnp.minimum(large, N_BUCKETS - 1))


def rel_bias_lookup(rel_bias, dist):
    return rel_bias[rel_bucket(dist)].astype(jnp.float32)


def masked_softmax(logits, mask):
    l = jnp.where(mask, logits, -jnp.inf)
    m = jnp.max(l, axis=-1)
    m = jnp.where(jnp.isfinite(m), m, 0.0)
    p = jnp.exp(l - m[..., None])
    s = jnp.sum(p, axis=-1)
    return p / jnp.maximum(s, 1e-30)[..., None], m, s


def sq_relu_mlp(x, g, w1, w2):
    h = jax.nn.relu(rms_norm(x, g) @ w1)
    return x + (h * h) @ w2


def banded_attention(q, k, v, rel_bias, window, dil):
    n, L, h, hd = q.shape
    kvh = k.shape[2]
    rep = h // kvh
    nb = L // BAND_BLOCK
    n_prev = -(-window // BAND_BLOCK)
    pad = n_prev * BAND_BLOCK
    kb_len = pad + BAND_BLOCK
    qi = jnp.arange(BAND_BLOCK)[:, None] + pad
    ki = jnp.arange(kb_len)[None, :]
    dist = qi - ki
    key_real = (jnp.arange(nb)[:, None, None] * BAND_BLOCK + ki[None]) >= pad
    valid = ((dist >= 0) & (dist <= window))[None] & key_real
    bias = jnp.transpose(rel_bias_lookup(rel_bias, dist * dil), (2, 0, 1)).reshape(kvh, rep, BAND_BLOCK, kb_len)

    def one(args):
        qs, ks, vs = args
        kp = jnp.pad(ks, ((pad, 0), (0, 0), (0, 0))).reshape(nb + n_prev, BAND_BLOCK, kvh, hd)
        vp = jnp.pad(vs, ((pad, 0), (0, 0), (0, 0))).reshape(nb + n_prev, BAND_BLOCK, kvh, hd)
        kband = jnp.concatenate([kp[o:o + nb] for o in range(n_prev + 1)], axis=1)
        vband = jnp.concatenate([vp[o:o + nb] for o in range(n_prev + 1)], axis=1)
        qb = qs.reshape(nb, BAND_BLOCK, kvh, rep, hd)
        logits = jnp.einsum('bqgrd,bkgd->bgrqk', qb, kband, preferred_element_type=jnp.float32) + bias
        p, m, s = masked_softmax(logits, valid[:, None, None])
        o = jnp.einsum('bgrqk,bkgd->bqgrd', p.astype(vs.dtype), vband)
        rows = lambda a: jnp.transpose(a, (0, 3, 1, 2)).reshape(L, h)
        return o.reshape(L, h, hd), rows(m), rows(s)

    return lax.map(one, (q, k, v))


def window_sample(q, k_all, v_all, buf_len, rel_bias, window):
    db, t, h, hd = q.shape
    kvh = k_all.shape[2]
    rep = h // kvh
    L = k_all.shape[1]
    dist = buf_len + jnp.arange(t)[:, None] - jnp.arange(L)[None, :]
    valid = (dist >= 0) & (dist <= window)
    bias = jnp.transpose(rel_bias_lookup(rel_bias, dist), (0, 2, 1)).reshape(t, kvh, rep, L)
    qg = q.reshape(db, t, kvh, rep, hd)
    logits = jnp.einsum('btgrd,blgd->btgrl', qg, k_all, preferred_element_type=jnp.float32) + bias
    p, _, _ = masked_softmax(logits, valid[None, :, None, None, :])
    o = jnp.einsum('btgrl,blgd->btgrd', p.astype(v_all.dtype), v_all)
    return o.reshape(db, t, h, hd)


def nsa_project(x, attn_norm, w_in, q_norm, k_norm):
    b, t, _ = x.shape
    z = rms_norm(x, attn_norm) @ w_in
    q, kc, vc, ks, vs, kw, vw, g = jnp.split(z, NSA_SPLITS, axis=-1)
    kv = lambda a: a.reshape(b, t, NSA_KV_HEADS, HEAD_DIM)
    q = rms_norm(q.reshape(b, t, N_HEADS, HEAD_DIM), q_norm) * ATTN_SCALE
    gates = jax.nn.sigmoid(g.astype(jnp.float32)).reshape(b, t, N_BRANCHES, N_HEADS)
    return (q, kv(kc), kv(vc), rms_norm(kv(ks), k_norm[1]), kv(vs),
            rms_norm(kv(kw), k_norm[2]), kv(vw), gates)


def compress(k, v, cmp_pe, cmp_w1, cmp_w2, k_norm_c):
    n_chunks = k.shape[1] // CMP_STRIDE
    n_sub = CMP_LEN // CMP_STRIDE
    n_cmp = n_chunks - n_sub + 1

    def phi(x, i):
        ch = x[:, :n_chunks * CMP_STRIDE].reshape(x.shape[0], n_chunks, CMP_STRIDE, *x.shape[2:])
        w1s = cmp_w1[i].reshape(n_sub, CMP_STRIDE, HEAD_DIM, CMP_HIDDEN)
        hid = jnp.einsum('ld,ldh->h', cmp_pe[i], cmp_w1[i])
        for o in range(n_sub):
            hid = hid + jnp.einsum('nclgd,ldh->ncgh', ch[:, o:o + n_cmp], w1s[o])
        return jnp.einsum('ncgh,hd->ncgd', jax.nn.silu(hid), cmp_w2[i])

    ck = rms_norm(phi(k, 0), k_norm_c)
    cv = phi(v, 1)
    c_end = jnp.arange(n_cmp) * CMP_STRIDE + CMP_LEN - 1
    return ck, cv, c_end


def to_sel_blocks(k):
    n, L, kvh, hd = k.shape
    n_sel = -(-L // SEL_BLOCK)
    k = jnp.pad(k, ((0, 0), (0, n_sel * SEL_BLOCK - L), (0, 0), (0, 0)))
    return k.reshape(n, n_sel, SEL_BLOCK, kvh, hd).transpose(0, 3, 1, 2, 4)


def selection_map(c_end, n_sel):
    c_start = c_end - CMP_LEN + 1
    j0 = jnp.arange(n_sel) * SEL_BLOCK
    return ((c_start[:, None] <= j0[None] + SEL_BLOCK - 1) & (c_end[:, None] >= j0[None])).astype(jnp.float32)


def nsa_cmp_sel_block(q, qpos, ck, cv, c_end, sel_map, kb, vb, rel_bias):
    n, nq, h, hd = q.shape
    kvh = ck.shape[2]
    rep = h // kvh
    n_sel = kb.shape[2]
    n_top = min(SEL_TOP_N, n_sel)
    qg = q.reshape(n, nq, kvh, rep, hd)
    dist_c = qpos[:, :, None] - c_end[None, None, :]
    bias_c = jnp.swapaxes(rel_bias_lookup(rel_bias, dist_c), 2, 3).reshape(qpos.shape[0], nq, kvh, rep, -1)
    logits_c = jnp.einsum('nqgrd,ncgd->nqgrc', qg, ck, preferred_element_type=jnp.float32) + bias_c
    p_c, _, _ = masked_softmax(logits_c, (dist_c >= 0)[:, :, None, None, :])
    o_c = jnp.einsum('nqgrc,ncgd->nqgrd', p_c.astype(cv.dtype), cv)
    imp = jnp.einsum('nqgrc,cj->nqgj', p_c, sel_map)
    j = jnp.arange(n_sel)
    cur = (qpos // SEL_BLOCK)[..., None]
    forced = (j == 0) | (j == cur) | (j == cur - 1)
    future = j * SEL_BLOCK > qpos[..., None]
    imp = jnp.where(forced[:, :, None], jnp.inf, jnp.where(future[:, :, None], -jnp.inf, imp))
    _, idx = lax.top_k(imp, n_top)
    ni = jnp.arange(n)[:, None, None, None]
    gi = jnp.arange(kvh)[None, None, :, None]
    k_s = kb[ni, gi, idx]
    v_s = vb[ni, gi, idx]
    kpos = idx[..., None] * SEL_BLOCK + jnp.arange(SEL_BLOCK)
    dist_s = qpos[:, :, None, None, None] - kpos
    bias_s = rel_bias.reshape(N_BUCKETS, kvh, rep)[rel_bucket(dist_s), jnp.arange(kvh)[:, None, None]]
    bias_s = jnp.moveaxis(bias_s, -1, 3).astype(jnp.float32)
    logits_s = jnp.einsum('nqgrd,nqgtkd->nqgrtk', qg, k_s, preferred_element_type=jnp.float32) + bias_s
    logits_s = logits_s.reshape(n, nq, kvh, rep, n_top * SEL_BLOCK)
    mask_s = (dist_s >= 0).reshape(n, nq, kvh, 1, n_top * SEL_BLOCK)
    p_s, _, _ = masked_softmax(logits_s, mask_s)
    o_s = jnp.einsum('nqgrk,nqgkd->nqgrd', p_s.astype(v_s.dtype), v_s.reshape(n, nq, kvh, n_top * SEL_BLOCK, hd))
    return o_c.reshape(n, nq, h, hd), o_s.reshape(n, nq, h, hd)


def map_query_blocks(fn, block, q, qpos):
    n, t = q.shape[:2]
    nb = t // block
    qs = jnp.moveaxis(q.reshape(n, nb, block, *q.shape[2:]), 1, 0)
    ps = jnp.moveaxis(qpos.reshape(qpos.shape[0], nb, block), 1, 0)
    o_c, o_s = lax.map(lambda a: fn(a[0], a[1]), (qs, ps))
    back = lambda o: jnp.moveaxis(o, 0, 1).reshape(n, t, *o.shape[3:])
    return back(o_c), back(o_s)


def nsa_output(x, gates, o_c, o_s, o_w, w_out):
    o = gates[:, :, 0, :, None] * o_c + gates[:, :, 1, :, None] * o_s + gates[:, :, 2, :, None] * o_w
    return x + o.astype(x.dtype).reshape(*x.shape[:2], -1) @ w_out


def nsa_prompt(x, rel_bias, attn_norm, w_in, q_norm, k_norm, cmp_pe, cmp_w1, cmp_w2, w_out):
    b, s, _ = x.shape
    q, kc, vc, ks, vs, kw, vw, gates = nsa_project(x, attn_norm, w_in, q_norm, k_norm)
    ck, cv, c_end = compress(kc, vc, cmp_pe, cmp_w1, cmp_w2, k_norm[0])
    kb, vb = to_sel_blocks(ks), to_sel_blocks(vs)
    sel_map = selection_map(c_end, kb.shape[2])
    fn = lambda qq, pp: nsa_cmp_sel_block(qq, pp, ck, cv, c_end, sel_map, kb, vb, rel_bias)
    o_c, o_s = map_query_blocks(fn, min(SEL_Q_BLOCK, s), q, jnp.arange(s)[None])
    o_w = banded_attention(q, kw, vw, rel_bias, NSA_WINDOW, 1)[0]
    y = nsa_output(x, gates, o_c, o_s, o_w, w_out)
    rows = jnp.stack([kc, vc, ks, vs], axis=2)
    win = jnp.stack([kw, vw], axis=2)[:, -min(NSA_WINDOW, s):]
    return y, rows, win


def nsa_sample(x, cache_l, page_table, win_buf, rel_bias, attn_norm, w_in, q_norm, k_norm, cmp_pe, cmp_w1, cmp_w2, w_out):
    db, t, _ = x.shape
    q, kc, vc, ks, vs, kw, vw, gates = nsa_project(x, attn_norm, w_in, q_norm, k_norm)
    past = cache_l[page_table]
    past_len = past.shape[1] * past.shape[2]
    past = past.reshape(db, past_len, *past.shape[3:])
    cat = lambda i, new: jnp.concatenate([past[:, :, i], new], axis=1)
    ck, cv, c_end = compress(cat(0, kc), cat(1, vc), cmp_pe, cmp_w1, cmp_w2, k_norm[0])
    kb, vb = to_sel_blocks(cat(2, ks)), to_sel_blocks(cat(3, vs))
    sel_map = selection_map(c_end, kb.shape[2])
    fn = lambda qq, pp: nsa_cmp_sel_block(qq, pp, ck, cv, c_end, sel_map, kb, vb, rel_bias)
    o_c, o_s = map_query_blocks(fn, 1, q, (past_len + jnp.arange(t))[None])
    wb = win_buf.shape[1]
    o_w = window_sample(q, jnp.concatenate([win_buf[:, :, 0], kw], axis=1),
                        jnp.concatenate([win_buf[:, :, 1], vw], axis=1), wb, rel_bias, NSA_WINDOW)
    y = nsa_output(x, gates, o_c, o_s, o_w, w_out)
    rows = jnp.stack([kc, vc, ks, vs], axis=2)
    win = jnp.concatenate([win_buf, jnp.stack([kw, vw], axis=2)], axis=1)[:, -min(NSA_WINDOW, wb + t):]
    return y, rows, win


def shared_kv(x, kv_norm, w_kv, k_norm):
    b, t, _ = x.shape
    kv = (rms_norm(x, kv_norm) @ w_kv).reshape(b, t, 2, DIL_KV_HEADS, HEAD_DIM)
    return rms_norm(kv[:, :, 0], k_norm), kv[:, :, 1]


def dil_queries(x, attn_norm, w_q, q_norm):
    b, t, _ = x.shape
    q = (rms_norm(x, attn_norm) @ w_q).reshape(b, t, N_DIL_GROUPS, N_HEADS, HEAD_DIM)
    return rms_norm(q, q_norm[:, None, :]) * ATTN_SCALE


def dilated_group_prompt(q, k, v, rel_bias, window, dil):
    b, s = q.shape[:2]
    L = s // dil
    Lp = -(-L // BAND_BLOCK) * BAND_BLOCK

    def split(t):
        t = t.reshape(b, L, dil, *t.shape[2:]).swapaxes(1, 2).reshape(b * dil, L, *t.shape[2:])
        return jnp.pad(t, ((0, 0), (0, Lp - L), (0, 0), (0, 0)))

    def merge(t):
        t = t[:, :L].reshape(b, dil, L, *t.shape[2:]).swapaxes(1, 2)
        return t.reshape(b, s, *t.shape[3:])

    o, m, den = banded_attention(split(q), split(k), split(v), rel_bias, window // dil, dil)
    return merge(o), merge(m), merge(den)


def dilated_group_sample(q, k_all, v_all, buf_len, rel_bias, window, dil):
    db, t, h, hd = q.shape
    kvh = k_all.shape[2]
    rep = h // kvh
    n_taps = window // dil + 1
    dist = dil * jnp.arange(n_taps)
    idx = buf_len + jnp.arange(t)[:, None] - dist[None, :]
    valid = idx >= 0
    idx_c = jnp.maximum(idx, 0)
    kg = k_all[:, idx_c]
    vg = v_all[:, idx_c]
    bias = rel_bias_lookup(rel_bias, dist).T.reshape(kvh, rep, n_taps)
    qg = q.reshape(db, t, kvh, rep, hd)
    logits = jnp.einsum('btgrd,btngd->btgrn', qg, kg, preferred_element_type=jnp.float32) + bias
    p, m, den = masked_softmax(logits, valid[None, :, None, None, :])
    o = jnp.einsum('btgrn,btngd->btgrd', p.astype(vg.dtype), vg)
    return o.reshape(db, t, h, hd), m.reshape(db, t, h), den.reshape(db, t, h)


def dil_output(x, outs, w_out):
    ms = jnp.stack([m for _, m, _ in outs])
    mx = jnp.max(ms, axis=0)
    w = jnp.stack([den for _, _, den in outs]) * jnp.exp(ms - mx)
    os_ = jnp.stack([o for o, _, _ in outs])
    o = jnp.sum(w[..., None] * os_, axis=0) / jnp.sum(w, axis=0)[..., None]
    return x + o.astype(x.dtype).reshape(*x.shape[:2], -1) @ w_out


def dil_prompt(x, k, v, rel_bias, attn_norm, w_q, q_norm, w_out):
    q = dil_queries(x, attn_norm, w_q, q_norm)
    outs = [dilated_group_prompt(q[:, :, g], k, v, rel_bias, w, d) for g, (w, d) in enumerate(DIL_GROUPS)]
    return dil_output(x, outs, w_out)


def dil_sample(x, k_all, v_all, buf_len, rel_bias, attn_norm, w_q, q_norm, w_out):
    q = dil_queries(x, attn_norm, w_q, q_norm)
    outs = [dilated_group_sample(q[:, :, g], k_all, v_all, buf_len, rel_bias, w, d) for g, (w, d) in enumerate(DIL_GROUPS)]
    return dil_output(x, outs, w_out)


def setup_inputs(seed: int = 0) -> dict:
    key = jax.random.key(seed)
    ks = jax.random.split(key, 24)
    f32 = jnp.float32
    n_pages = PAST_LEN // PAGE_SIZE
    n_pool = (DEC_BATCH * n_pages * 5) // 4
    win_buf = min(NSA_WINDOW, PAST_LEN)
    dil_buf = min(DIL_WINDOW_MAX, PAST_LEN)
    nrm = lambda k, shape, scale: jax.random.normal(k, shape, f32) * scale
    gain = lambda k, shape: 1.0 + 0.02 * jax.random.normal(k, shape, f32)
    page_table = jax.random.permutation(ks[5], n_pool)[:DEC_BATCH * n_pages].reshape(DEC_BATCH, n_pages).astype(jnp.int32)
    return {
        'x_prompt': nrm(ks[0], (BATCH, SEQ, D_MODEL), 1.0),
        'x_sample': nrm(ks[1], (DEC_BATCH, DEC_SEQ, D_MODEL), 1.0),
        'cache_nsa_kv': nrm(ks[2], (N_A_LAYERS, n_pool, PAGE_SIZE, 4, NSA_KV_HEADS, HEAD_DIM), 1.0),
        'cache_win_kv': nrm(ks[3], (N_A_LAYERS, DEC_BATCH, win_buf, 2, NSA_KV_HEADS, HEAD_DIM), 1.0),
        'cache_dil_kv': nrm(ks[4], (DEC_BATCH, dil_buf, 2, DIL_KV_HEADS, HEAD_DIM), 1.0),
        'page_table': page_table,
        'rel_bias': nrm(ks[6], (N_BUCKETS, N_HEADS), 0.5),
        'a_attn_norm': gain(ks[7], (N_A_LAYERS, D_MODEL)),
        'a_w_in': nrm(ks[8], (N_A_LAYERS, D_MODEL, NSA_IN), D_MODEL ** -0.5),
        'a_q_norm': gain(ks[9], (N_A_LAYERS, HEAD_DIM)),
        'a_k_norm': gain(ks[10], (N_A_LAYERS, N_BRANCHES, HEAD_DIM)),
        'a_cmp_pe': nrm(ks[11], (N_A_LAYERS, 2, CMP_LEN, HEAD_DIM), 0.1),
        'a_cmp_w1': nrm(ks[12], (N_A_LAYERS, 2, CMP_LEN, HEAD_DIM, CMP_HIDDEN), (CMP_LEN * HEAD_DIM) ** -0.5),
        'a_cmp_w2': nrm(ks[13], (N_A_LAYERS, 2, CMP_HIDDEN, HEAD_DIM), CMP_HIDDEN ** -0.5),
        'a_w_out': nrm(ks[14], (N_A_LAYERS, N_HEADS * HEAD_DIM, D_MODEL), (N_HEADS * HEAD_DIM) ** -0.5),
        'kv_norm': gain(ks[15], (D_MODEL,)),
        'w_kv_shared': nrm(ks[16], (D_MODEL, 2 * DIL_KV_HEADS * HEAD_DIM), D_MODEL ** -0.5),
        'k_norm_shared': gain(ks[17], (HEAD_DIM,)),
        'b_attn_norm': gain(ks[18], (N_B_LAYERS, D_MODEL)),
        'b_w_q': nrm(ks[19], (N_B_LAYERS, D_MODEL, N_DIL_GROUPS * N_HEADS * HEAD_DIM), D_MODEL ** -0.5),
        'b_q_norm': gain(ks[20], (N_B_LAYERS, N_DIL_GROUPS, HEAD_DIM)),
        'b_w_out': nrm(ks[21], (N_B_LAYERS, N_HEADS * HEAD_DIM, D_MODEL), (N_HEADS * HEAD_DIM) ** -0.5),
        'mlp_norm': gain(ks[22], (DEPTH, D_MODEL)),
        'mlp_w1': nrm(ks[23], (DEPTH, D_MODEL, D_FF), D_MODEL ** -0.5),
        'mlp_w2': nrm(jax.random.fold_in(key, 99), (DEPTH, D_FF, D_MODEL), D_FF ** -0.5),
    }


def reference(x_prompt, x_sample, cache_nsa_kv, cache_win_kv, cache_dil_kv, page_table, rel_bias,
              a_attn_norm, a_w_in, a_q_norm, a_k_norm, a_cmp_pe, a_cmp_w1, a_cmp_w2, a_w_out,
              kv_norm, w_kv_shared, k_norm_shared, b_attn_norm, b_w_q, b_q_norm, b_w_out,
              mlp_norm, mlp_w1, mlp_w2):
    yp, ys = x_prompt, x_sample
    rows_p, rows_s, wins_p, wins_s = [], [], [], []
    for l in range(DEPTH):
        if l < N_A_LAYERS:
            params = (a_attn_norm[l], a_w_in[l], a_q_norm[l], a_k_norm[l],
                      a_cmp_pe[l], a_cmp_w1[l], a_cmp_w2[l], a_w_out[l])
            yp, r, w = nsa_prompt(yp, rel_bias, *params)
            rows_p.append(r)
            wins_p.append(w)
            ys, r, w = nsa_sample(ys, cache_nsa_kv[l], page_table, cache_win_kv[l], rel_bias, *params)
            rows_s.append(r)
            wins_s.append(w)
        else:
            if l == N_A_LAYERS:
                kp, vp = shared_kv(yp, kv_norm, w_kv_shared, k_norm_shared)
                kn, vn = shared_kv(ys, kv_norm, w_kv_shared, k_norm_shared)
                buf_len = cache_dil_kv.shape[1]
                k_all = jnp.concatenate([cache_dil_kv[:, :, 0], kn], axis=1)
                v_all = jnp.concatenate([cache_dil_kv[:, :, 1], vn], axis=1)
                new_dil_kv_prompt = jnp.stack([kp, vp], axis=2)[:, -min(DIL_WINDOW_MAX, kp.shape[1]):]
                new_dil_kv_sample = jnp.concatenate([cache_dil_kv, jnp.stack([kn, vn], axis=2)], axis=1)[:, -min(DIL_WINDOW_MAX, buf_len + kn.shape[1]):]
            i = l - N_A_LAYERS
            params = (b_attn_norm[i], b_w_q[i], b_q_norm[i], b_w_out[i])
            yp = dil_prompt(yp, kp, vp, rel_bias, *params)
            ys = dil_sample(ys, k_all, v_all, buf_len, rel_bias, *params)
        yp = sq_relu_mlp(yp, mlp_norm[l], mlp_w1[l], mlp_w2[l])
        ys = sq_relu_mlp(ys, mlp_norm[l], mlp_w1[l], mlp_w2[l])
    new_nsa_kv_prompt = jnp.stack(rows_p)
    new_nsa_kv_sample = jnp.stack(rows_s)
    new_win_kv_prompt = jnp.stack(wins_p)
    new_win_kv_sample = jnp.stack(wins_s)
    return (yp, ys, new_nsa_kv_prompt, new_nsa_kv_sample, new_win_kv_prompt, new_win_kv_sample, new_dil_kv_prompt, new_dil_kv_sample)
```

```python
import functools
import math

import numpy as np
import jax
import jax.numpy as jnp
from jax import lax
from jax.experimental import pallas as pl
from jax.experimental.pallas import tpu as pltpu

F32 = jnp.float32
BF16 = jnp.bfloat16

D_MODEL = 1024
N_HEADS = 16
HEAD_DIM = 64
KV_HEADS = 4
REP = N_HEADS // KV_HEADS
GROUP_ROWS = REP * HEAD_DIM
D_FF = 4 * D_MODEL
EPS = 1e-6
ATTN_SCALE = HEAD_DIM ** -0.5
N_BUCKETS = 32
MAX_DISTANCE = 2048
CMP_LEN = 32
CMP_STRIDE = 16
CMP_HIDDEN = 2 * HEAD_DIM
SEL_BLOCK = 64
SEL_TOP_N = 16
NSA_WINDOW = 512
N_BRANCHES = 3
NSA_IN = N_HEADS * HEAD_DIM + 6 * KV_HEADS * HEAD_DIM + N_BRANCHES * N_HEADS
DIL_GROUPS = ((128, 1), (512, 4), (2048, 16))
PAGE_SIZE = 128

NEG = -1e30
VMEM_LIMIT = 56 * 1024 * 1024
BIAS_TABLE_LEN = 2048


def _bucket_of_distance(d):
    max_exact = N_BUCKETS // 2
    d = np.maximum(d, 0)
    ratio = np.log(np.maximum(d, 1).astype(np.float32) / np.float32(max_exact)) / np.float32(
        math.log(MAX_DISTANCE / max_exact))
    large = max_exact + (ratio * np.float32(N_BUCKETS - max_exact)).astype(np.int32)
    return np.where(d < max_exact, d, np.minimum(large, N_BUCKETS - 1)).astype(np.int32)


_BUCKETS = _bucket_of_distance(np.arange(BIAS_TABLE_LEN))
BIAS_CONST_FROM = int(np.argmax(_BUCKETS == N_BUCKETS - 1))
assert np.all(_BUCKETS[BIAS_CONST_FROM:] == N_BUCKETS - 1)


def _params(*sem):
    return pltpu.CompilerParams(dimension_semantics=sem, vmem_limit_bytes=VMEM_LIMIT)


def _token_tile(t, tm):
    tm = min(tm, t)
    assert t % tm == 0, (t, tm)
    return tm


def _proj_kernel(x_ref, g_ref, w_ref, n_ref, o_ref, *, segs):
    x = x_ref[...]
    ms = jnp.mean(x * x, axis=0, keepdims=True)
    xn = (x * lax.rsqrt(ms + EPS) * g_ref[...]).astype(BF16)
    tm = x.shape[1]
    for r0, nr, kind, ni, scale in segs:
        z = jnp.dot(w_ref[r0:r0 + nr, :], xn, preferred_element_type=F32)
        if kind == "norm":
            z3 = z.reshape(nr // HEAD_DIM, HEAD_DIM, tm)
            hs = jnp.mean(z3 * z3, axis=1, keepdims=True)
            z = (z3 * lax.rsqrt(hs + EPS) * n_ref[ni][None] * scale).reshape(nr, tm)
        elif kind == "sigmoid":
            z = 1.0 / (1.0 + jnp.exp(-z))
        o_ref[r0:r0 + nr, :] = z


def norm_proj(xT, gain, wT, norms, segs, tm=512):
    d, t = xT.shape
    tm = _token_tile(t, tm)
    n = wT.shape[0]
    return pl.pallas_call(
        functools.partial(_proj_kernel, segs=tuple(segs)),
        out_shape=jax.ShapeDtypeStruct((n, t), F32),
        grid=(t // tm,),
        in_specs=[
            pl.BlockSpec((d, tm), lambda i: (0, i)),
            pl.BlockSpec((d, 1), lambda i: (0, 0)),
            pl.BlockSpec((n, d), lambda i: (0, 0)),
            pl.BlockSpec(norms.shape, lambda i: (0, 0, 0)),
        ],
        out_specs=pl.BlockSpec((n, tm), lambda i: (0, i)),
        compiler_params=_params("parallel"),
        name="norm_proj",
    )(xT, gain.reshape(d, 1), wT, norms)


def _outmlp_kernel(x_ref, o_ref, wo_ref, g_ref, w1_ref, w2_ref, y_ref, x1_s, xn_s, acc_s):
    f = pl.program_id(1)

    @pl.when(f == 0)
    def _():
        x1 = x_ref[...] + jnp.dot(wo_ref[...], o_ref[...], preferred_element_type=F32)
        x1_s[...] = x1
        ms = jnp.mean(x1 * x1, axis=0, keepdims=True)
        xn_s[...] = (x1 * lax.rsqrt(ms + EPS) * g_ref[...]).astype(BF16)
        acc_s[...] = jnp.zeros_like(acc_s)

    h = jnp.maximum(jnp.dot(w1_ref[...], xn_s[...], preferred_element_type=F32), 0.0)
    acc_s[...] += jnp.dot(w2_ref[...], (h * h).astype(BF16), preferred_element_type=F32)

    @pl.when(f == pl.num_programs(1) - 1)
    def _():
        y_ref[...] = x1_s[...] + acc_s[...]


def outproj_mlp(xT, oT, woT, gain, w1T, w2T, tm=512, tf=1024):
    d, t = xT.shape
    tm = _token_tile(t, tm)
    dff = w1T.shape[0]
    return pl.pallas_call(
        _outmlp_kernel,
        out_shape=jax.ShapeDtypeStruct((d, t), F32),
        grid=(t // tm, dff // tf),
        in_specs=[
            pl.BlockSpec((d, tm), lambda i, f: (0, i)),
            pl.BlockSpec((d, tm), lambda i, f: (0, i)),
            pl.BlockSpec((d, d), lambda i, f: (0, 0)),
            pl.BlockSpec((d, 1), lambda i, f: (0, 0)),
            pl.BlockSpec((tf, d), lambda i, f: (f, 0)),
            pl.BlockSpec((d, tf), lambda i, f: (0, f)),
        ],
        out_specs=pl.BlockSpec((d, tm), lambda i, f: (0, i)),
        scratch_shapes=[pltpu.VMEM((d, tm), F32), pltpu.VMEM((d, tm), BF16), pltpu.VMEM((d, tm), F32)],
        compiler_params=_params("parallel", "arbitrary"),
        name="outproj_mlp",
    )(xT, oT, woT, gain.reshape(d, 1), w1T, w2T)


def _compress_kernel(ch_ref, w1_ref, pe_ref, w2_ref, kn_ref, o_ref):
    slot = pl.program_id(0)
    bn, c, _ = ch_ref.shape[1:]
    w1 = w1_ref[0]
    ab = jnp.dot(ch_ref[0].reshape(bn * c, CMP_STRIDE * HEAD_DIM), w1, preferred_element_type=F32)
    pe_a = jnp.dot(pe_ref[0, 0], w1[:, :CMP_HIDDEN], preferred_element_type=F32)[0:1]
    pe_b = jnp.dot(pe_ref[0, 1], w1[:, CMP_HIDDEN:], preferred_element_type=F32)[0:1]
    nxt = pltpu.roll(ab[:, CMP_HIDDEN:], bn * c - 1, 0)
    hid = ab[:, :CMP_HIDDEN] + nxt + (pe_a + pe_b)
    act = hid / (1.0 + jnp.exp(-hid))
    out = jnp.dot(act.astype(BF16), w2_ref[0], preferred_element_type=F32)
    ms = jnp.mean(out * out, axis=-1, keepdims=True)
    normed = out * lax.rsqrt(ms + EPS) * kn_ref[...]
    o_ref[0] = jnp.where(slot == 0, normed, out).reshape(bn, c, HEAD_DIM)


def compress(ch, w1ab, pe, w2, k_gain, bn):
    _, nb, c, w = ch.shape
    return pl.pallas_call(
        _compress_kernel,
        out_shape=jax.ShapeDtypeStruct((2, nb, c, HEAD_DIM), F32),
        grid=(2, nb // bn),
        in_specs=[
            pl.BlockSpec((1, bn, c, w), lambda s, i: (s, i, 0, 0)),
            pl.BlockSpec((1, w, 2 * CMP_HIDDEN), lambda s, i: (s, 0, 0)),
            pl.BlockSpec((1, 2, 16, w), lambda s, i: (s, 0, 0, 0)),
            pl.BlockSpec((1, CMP_HIDDEN, HEAD_DIM), lambda s, i: (s, 0, 0)),
            pl.BlockSpec((1, HEAD_DIM), lambda s, i: (0, 0)),
        ],
        out_specs=pl.BlockSpec((1, bn, c, HEAD_DIM), lambda s, i: (s, i, 0, 0)),
        compiler_params=_params("parallel", "parallel"),
        name="nsa_compress",
    )(ch, w1ab, pe, w2, k_gain.reshape(1, HEAD_DIM))


def compress_weights(cmp_pe, cmp_w1, cmp_w2):
    half = CMP_STRIDE * HEAD_DIM
    w1 = cmp_w1.reshape(2, 2, half, CMP_HIDDEN)
    w1ab = jnp.concatenate([w1[:, 0], w1[:, 1]], axis=-1).astype(BF16)
    pe = jnp.broadcast_to(cmp_pe.reshape(2, 2, 1, half), (2, 2, 16, half)).astype(BF16)
    return w1ab, pe, cmp_w2.astype(BF16)


CMP_CHUNK = 128


def _cmp_topk_kernel(q_ref, ck_ref, cv_ref, map_ref, bias_ref, oc_ref, sel_ref, v_s, *, tq, n_top, k_const):
    i = pl.program_id(2)
    c = ck_ref.shape[2]
    n_sel = map_ref.shape[0]
    n_chunks = c // CMP_CHUNK
    psum = None
    for r in range(REP):
        q = q_ref[r * HEAD_DIM:(r + 1) * HEAD_DIM, :]
        logits = []
        for jc in range(n_chunks):
            k0 = i * (tq // CMP_STRIDE) + jc * CMP_CHUNK
            start = pl.multiple_of(jnp.minimum(k0, k_const), 8)
            s = jnp.dot(ck_ref[0, 0, jc * CMP_CHUNK:(jc + 1) * CMP_CHUNK, :], q, preferred_element_type=F32)
            logits.append(s + bias_ref[r, pl.ds(start, CMP_CHUNK), :])
        m = functools.reduce(jnp.maximum, [jnp.max(l, axis=0, keepdims=True) for l in logits])
        m = jnp.where(m < 0.5 * NEG, 0.0, m)
        p = [jnp.exp(l - m) for l in logits]
        den = functools.reduce(jnp.add, [jnp.sum(x, axis=0, keepdims=True) for x in p])
        inv = 1.0 / jnp.maximum(den, 1e-30)
        p = jnp.concatenate([x * inv for x in p], axis=0)
        oc_ref[r * HEAD_DIM:(r + 1) * HEAD_DIM, :] = jnp.dot(
            cv_ref[0, 0], p.astype(BF16), preferred_element_type=F32)
        psum = p if psum is None else psum + p
    hi = psum.astype(BF16)
    lo = (psum - hi.astype(F32)).astype(BF16)
    imp = (jnp.dot(map_ref[...], hi, preferred_element_type=F32)
           + jnp.dot(map_ref[...], lo, preferred_element_type=F32))
    qpos = i * tq + lax.broadcasted_iota(jnp.int32, (n_sel, tq), 1)
    j = lax.broadcasted_iota(jnp.int32, (n_sel, tq), 0)
    cur = qpos // SEL_BLOCK
    forced = (j == 0) | (j == cur) | (j == cur - 1)
    future = j * SEL_BLOCK > qpos
    v = jnp.where(forced, jnp.inf, jnp.where(future, -jnp.inf, imp))
    v_s[...] = v

    def count(a, cnt):
        va = v_s[pl.ds(a, 1), :]
        ahead = (va > v) | ((va == v) & (j > a))
        return cnt + jnp.where(ahead, 1.0, 0.0)

    rank = lax.fori_loop(0, n_sel, count, jnp.zeros((n_sel, tq), F32))
    sel_ref[0, 0] = jnp.where(rank < n_top, 1.0, 0.0)


def cmp_strip(bias_t, c, tq, rows):
    k = np.arange(rows)[:, None]
    tl = np.arange(tq)[None, :]
    d = CMP_STRIDE * (k - (c - 1)) + tl - (CMP_LEN - 1)
    idx = np.where(d < 0, BIAS_TABLE_LEN, np.minimum(d, BIAS_TABLE_LEN - 1)).astype(np.int32)
    return jnp.take(bias_t, jnp.asarray(idx), axis=1)


def cmp_topk(qT, ckr, cvTr, bias_t, b, s, tq=256):
    c = s // CMP_STRIDE
    n_sel = s // SEL_BLOCK
    n_top = min(SEL_TOP_N, n_sel)
    nq = s // tq
    k_const = (c - 1) + -(-(BIAS_CONST_FROM + CMP_LEN - 1) // CMP_STRIDE)
    k_const = -(-k_const // 8) * 8
    strip = cmp_strip(bias_t, c, tq, k_const + CMP_CHUNK)
    cc = np.arange(c)[::-1]
    c_end = cc * CMP_STRIDE + CMP_LEN - 1
    c_start = c_end - CMP_LEN + 1
    j0 = np.arange(n_sel)[:, None] * SEL_BLOCK
    sel_map = ((c_start[None] <= j0 + SEL_BLOCK - 1) & (c_end[None] >= j0) & (cc[None] < c - 1))
    sel_map = jnp.asarray(sel_map.astype(np.float32), dtype=BF16)
    return pl.pallas_call(
        functools.partial(_cmp_topk_kernel, tq=tq, n_top=n_top, k_const=k_const),
        out_shape=(jax.ShapeDtypeStruct((N_HEADS * HEAD_DIM, b * s), F32),
                   jax.ShapeDtypeStruct((b, KV_HEADS, n_sel, s), F32)),
        grid=(KV_HEADS, b, nq),
        in_specs=[
            pl.BlockSpec((GROUP_ROWS, tq), lambda g, n, i: (g, n * nq + i)),
            pl.BlockSpec((1, 1, c, HEAD_DIM), lambda g, n, i: (n, g, 0, 0)),
            pl.BlockSpec((1, 1, HEAD_DIM, c), lambda g, n, i: (n, g, 0, 0)),
            pl.BlockSpec((n_sel, c), lambda g, n, i: (0, 0)),
            pl.BlockSpec((REP, strip.shape[1], tq), lambda g, n, i: (g, 0, 0)),
        ],
        out_specs=(pl.BlockSpec((GROUP_ROWS, tq), lambda g, n, i: (g, n * nq + i)),
                   pl.BlockSpec((1, 1, n_sel, tq), lambda g, n, i: (n, g, 0, i))),
        scratch_shapes=[pltpu.VMEM((n_sel, tq), F32)],
        compiler_params=_params("parallel", "parallel", "parallel"),
        name="nsa_cmp_topk",
    )(qT, ckr, cvTr, sel_map, strip)


def _flash_kernel(*refs, tq, n_tiles_max, k_const, use_sel, normalize):
    if use_sel:
        q_ref, k_ref, v_ref, bias_ref, sel_ref = refs[:5]
        outs = refs[5:]
    else:
        q_ref, k_ref, v_ref, bias_ref = refs[:4]
        sel_ref = None
        outs = refs[4:]
    if normalize:
        o_ref, acc_s, m_s, l_s = outs
    else:
        o_ref, mo_ref, lo_ref, acc_s, m_s, l_s = outs
    i = pl.program_id(2)
    nkt = k_ref.shape[2]
    tk = k_ref.shape[3]
    first = nkt - 1 - i
    m_s[...] = jnp.full(m_s.shape, NEG, F32)
    l_s[...] = jnp.zeros(l_s.shape, F32)
    acc_s[...] = jnp.zeros(acc_s.shape, F32)

    def tile(jt, carry):
        kt = first + jt
        k = k_ref[0, 0, kt]
        v = v_ref[0, 0, kt]
        start = pl.multiple_of(jnp.minimum(jt * tk, k_const), 8)
        if use_sel:
            rows = sel_ref[0, 0, kt]
            mask = jnp.concatenate(
                [jnp.broadcast_to(jnp.where(rows[bb:bb + 1] > 0.0, 0.0, NEG), (SEL_BLOCK, tq))
                 for bb in range(tk // SEL_BLOCK)], axis=0)
        for r in range(q_ref.shape[0] // HEAD_DIM):
            q = q_ref[r * HEAD_DIM:(r + 1) * HEAD_DIM, :]
            s = jnp.dot(k, q, preferred_element_type=F32) + bias_ref[r, pl.ds(start, tk), :]
            if use_sel:
                s = s + mask
            m_old = m_s[r:r + 1, :]
            m_new = jnp.maximum(m_old, jnp.max(s, axis=0, keepdims=True))
            alpha = jnp.exp(m_old - m_new)
            p = jnp.exp(s - m_new)
            l_s[r:r + 1, :] = alpha * l_s[r:r + 1, :] + jnp.sum(p, axis=0, keepdims=True)
            acc_s[r * HEAD_DIM:(r + 1) * HEAD_DIM, :] = (
                alpha * acc_s[r * HEAD_DIM:(r + 1) * HEAD_DIM, :]
                + jnp.dot(v, p.astype(BF16), preferred_element_type=F32))
            m_s[r:r + 1, :] = m_new
        return carry

    lax.fori_loop(0, jnp.minimum(i + 1, n_tiles_max), tile, 0)
    nh = q_ref.shape[0] // HEAD_DIM
    if normalize:
        for r in range(nh):
            o_ref[r * HEAD_DIM:(r + 1) * HEAD_DIM, :] = (
                acc_s[r * HEAD_DIM:(r + 1) * HEAD_DIM, :] / l_s[r:r + 1, :])
    else:
        o_ref[...] = acc_s[...]
        mo_ref[0] = m_s[0:nh, :]
        lo_ref[0] = l_s[0:nh, :]


def flash_strip(bias_t, tq, rows, dil, window):
    k = np.arange(rows)[:, None]
    tl = np.arange(tq)[None, :]
    d = k + tl - tq + 1
    bad = d < 0
    if window is not None:
        bad = bad | (d > window)
    idx = np.where(bad, BIAS_TABLE_LEN, np.minimum(d * dil, BIAS_TABLE_LEN - 1)).astype(np.int32)
    return jnp.take(bias_t, jnp.asarray(idx), axis=1)


def flash_reversed(qT, kr, vTr, bias_t, nseq, seqlen, sel=None, window=None, dil=1, tq=256, normalize=True):
    tk = tq
    nq = seqlen // tq
    if window is None:
        k_const = -(-(BIAS_CONST_FROM + tq - 1) // 8) * 8
        rows = k_const + tk
        n_tiles_max = nq
    else:
        n_tiles_max = (window + tq - 1) // tk + 1
        rows = n_tiles_max * tk
        k_const = rows - tk
    strip = flash_strip(bias_t, tq, rows, dil, window)
    h_rows = qT.shape[0]
    use_sel = sel is not None
    in_specs = [
        pl.BlockSpec((GROUP_ROWS, tq), lambda g, n, i: (g, n * nq + i)),
        pl.BlockSpec((1, 1) + kr.shape[2:], lambda g, n, i: (n, g, 0, 0, 0)),
        pl.BlockSpec((1, 1) + vTr.shape[2:], lambda g, n, i: (n, g, 0, 0, 0)),
        pl.BlockSpec((REP, rows, tq), lambda g, n, i: (g, 0, 0)),
    ]
    args = [qT, kr, vTr, strip]
    if use_sel:
        in_specs.append(pl.BlockSpec((1, 1) + sel.shape[2:4] + (tq,), lambda g, n, i: (n, g, 0, 0, i)))
        args.append(sel)
    o_spec = pl.BlockSpec((GROUP_ROWS, tq), lambda g, n, i: (g, n * nq + i))
    o_shape = jax.ShapeDtypeStruct((h_rows, nseq * seqlen), F32)
    if normalize:
        out_shape, out_specs = o_shape, o_spec
    else:
        st_shape = jax.ShapeDtypeStruct((KV_HEADS, REP, nseq * seqlen), F32)
        st_spec = pl.BlockSpec((1, REP, tq), lambda g, n, i: (g, 0, n * nq + i))
        out_shape, out_specs = (o_shape, st_shape, st_shape), (o_spec, st_spec, st_spec)
    return pl.pallas_call(
        functools.partial(_flash_kernel, tq=tq, n_tiles_max=n_tiles_max, k_const=k_const,
                          use_sel=use_sel, normalize=normalize),
        out_shape=out_shape,
        grid=(KV_HEADS, nseq, nq),
        in_specs=in_specs,
        out_specs=out_specs,
        scratch_shapes=[pltpu.VMEM((GROUP_ROWS, tq), F32), pltpu.VMEM((8, tq), F32), pltpu.VMEM((8, tq), F32)],
        compiler_params=_params("parallel", "parallel", "parallel"),
        name="flash_sel" if use_sel else ("flash_band" if window is not None else "flash_causal"),
    )(*args)


def _nsa_combine_kernel(oc_ref, os_ref, ow_ref, g_ref, o_ref):
    for h in range(N_HEADS):
        rows = slice(h * HEAD_DIM, (h + 1) * HEAD_DIM)
        o = (g_ref[h:h + 1, :] * oc_ref[rows, :]
             + g_ref[N_HEADS + h:N_HEADS + h + 1, :] * os_ref[rows, :]
             + g_ref[2 * N_HEADS + h:2 * N_HEADS + h + 1, :] * ow_ref[rows, :])
        o_ref[rows, :] = o.astype(BF16)


def nsa_combine(ocT, osT, owT, gatesT, tm=512):
    d, t = ocT.shape
    tm = _token_tile(t, tm)
    big = pl.BlockSpec((d, tm), lambda i: (0, i))
    return pl.pallas_call(
        _nsa_combine_kernel,
        out_shape=jax.ShapeDtypeStruct((d, t), BF16),
        grid=(t // tm,),
        in_specs=[big, big, big, pl.BlockSpec((N_BRANCHES * N_HEADS, tm), lambda i: (0, i))],
        out_specs=big,
        compiler_params=_params("parallel"),
        name="nsa_combine",
    )(ocT, osT, owT, gatesT)


def _dil_merge_kernel(acc_ref, m_ref, l_ref, o_ref):
    ng = acc_ref.shape[0]
    mx = functools.reduce(jnp.maximum, [m_ref[g] for g in range(ng)])
    w = [jnp.exp(m_ref[g] - mx) for g in range(ng)]
    den = functools.reduce(jnp.add, [w[g] * l_ref[g] for g in range(ng)])
    inv = 1.0 / den
    for h in range(N_HEADS):
        rows = slice(h * HEAD_DIM, (h + 1) * HEAD_DIM)
        num = functools.reduce(jnp.add, [w[g][h:h + 1, :] * acc_ref[g, rows, :] for g in range(ng)])
        o_ref[rows, :] = (num * inv[h:h + 1, :]).astype(BF16)


def dil_merge(acc, m, l, tm=512):
    ng, d, t = acc.shape
    tm = _token_tile(t, tm)
    st = pl.BlockSpec((ng, N_HEADS, tm), lambda i: (0, 0, i))
    return pl.pallas_call(
        _dil_merge_kernel,
        out_shape=jax.ShapeDtypeStruct((d, t), BF16),
        grid=(t // tm,),
        in_specs=[pl.BlockSpec((ng, d, tm), lambda i: (0, 0, i)), st, st],
        out_specs=pl.BlockSpec((d, tm), lambda i: (0, i)),
        compiler_params=_params("parallel"),
        name="dil_merge",
    )(acc, m, l)


NSA_SEGS = (
    (0, 512, "norm", 0, ATTN_SCALE), (512, 512, "norm", 0, ATTN_SCALE),
    (1024, 512, "plain", 0, 1.0),
    (1536, 256, "norm", 1, 1.0),
    (1792, 256, "plain", 0, 1.0),
    (2048, 256, "norm", 2, 1.0),
    (2304, 256, "plain", 0, 1.0),
    (2560, N_BRANCHES * N_HEADS, "sigmoid", 0, 1.0),
)
KV_SEGS = ((0, 256, "norm", 0, 1.0), (256, 256, "plain", 0, 1.0))
DIL_Q_SEGS = tuple((512 * j, 512, "norm", j // 2, ATTN_SCALE) for j in range(6))


def bias_by_distance(rel_bias):
    bt = jnp.take(rel_bias, jnp.asarray(_BUCKETS), axis=0).T
    return jnp.concatenate([bt, jnp.full((N_HEADS, 1), NEG, F32)], axis=1)


def reversed_kv_tiles(k, v, tk):
    n, length = k.shape[:2]
    nkt = length // tk
    kr = jnp.flip(k, axis=1).transpose(0, 2, 1, 3).reshape(n, KV_HEADS, nkt, tk, HEAD_DIM)
    vr = jnp.flip(v, axis=1).transpose(0, 2, 1, 3).reshape(n, KV_HEADS, nkt, tk, HEAD_DIM)
    return kr.astype(BF16), vr.transpose(0, 1, 2, 4, 3).astype(BF16)


def nsa_attention_prompt(zT, b, s, bias_t, cw, k_gain_cmp, tq=256):
    t = b * s
    qT = zT[:1024].astype(BF16)
    tok = zT[1024:2560].T
    rows = tok[:, :1024].reshape(b, s, 4, KV_HEADS, HEAD_DIM)
    win = tok[:, 1024:].reshape(b, s, 2, KV_HEADS, HEAD_DIM)
    gatesT = zT[2560:2560 + N_BRANCHES * N_HEADS]
    c = s // CMP_STRIDE
    ch = rows[:, :, 0:2].transpose(2, 0, 3, 1, 4).reshape(2, b * KV_HEADS, c, CMP_STRIDE * HEAD_DIM).astype(BF16)
    cmp = compress(ch, *cw, k_gain_cmp, bn=4)
    cmp = jnp.flip(cmp, axis=2).reshape(2, b, KV_HEADS, c, HEAD_DIM)
    ckr = cmp[0].astype(BF16)
    cvTr = cmp[1].transpose(0, 1, 3, 2).astype(BF16)
    ocT, sel = cmp_topk(qT, ckr, cvTr, bias_t, b, s, tq=tq)
    kr, vTr = reversed_kv_tiles(rows[:, :, 2], rows[:, :, 3], tq)
    selr = jnp.flip(sel, axis=2).reshape(b, KV_HEADS, s // tq, tq // SEL_BLOCK, s)
    osT = flash_reversed(qT, kr, vTr, bias_t, b, s, sel=selr, tq=tq)
    kwr, vwTr = reversed_kv_tiles(win[:, :, 0], win[:, :, 1], tq)
    owT = flash_reversed(qT, kwr, vwTr, bias_t, b, s, window=NSA_WINDOW, tq=tq)
    oT = nsa_combine(ocT, osT, owT, gatesT)
    return oT, rows, win


def dil_attention_prompt(qallT, kp, vp, b, s, bias_t, tq=128):
    accs, ms, ls = [], [], []
    for gi, (window, dil) in enumerate(DIL_GROUPS):
        length = s // dil
        qg = qallT[1024 * gi:1024 * (gi + 1)].astype(BF16)
        qg = qg.reshape(1024, b, length, dil).transpose(0, 1, 3, 2).reshape(1024, b * s)

        def split(x):
            return x.reshape(b, length, dil, KV_HEADS, HEAD_DIM).transpose(0, 2, 1, 3, 4).reshape(
                b * dil, length, KV_HEADS, HEAD_DIM)

        kr, vTr = reversed_kv_tiles(split(kp), split(vp), tq)
        acc, m, l = flash_reversed(qg, kr, vTr, bias_t, b * dil, length, window=window // dil, dil=dil,
                                   tq=tq, normalize=False)

        def merge(x):
            r = x.shape[0]
            return x.reshape(r, b, dil, length).transpose(0, 1, 3, 2).reshape(r, b * s)

        accs.append(merge(acc))
        ms.append(merge(m.reshape(N_HEADS, b * s)))
        ls.append(merge(l.reshape(N_HEADS, b * s)))
    return dil_merge(jnp.stack(accs), jnp.stack(ms), jnp.stack(ls))


def _j_rms(x, g):
    return x * lax.rsqrt(jnp.mean(x * x, axis=-1, keepdims=True) + EPS) * g


def _j_bias(bias_t, dist):
    return jnp.moveaxis(bias_t[:, jnp.clip(dist, 0, BIAS_TABLE_LEN - 1)], 0, -1)


def _j_softmax(logits, mask):
    l = jnp.where(mask, logits, -jnp.inf)
    m = jnp.max(l, axis=-1)
    m = jnp.where(jnp.isfinite(m), m, 0.0)
    p = jnp.exp(l - m[..., None])
    s = jnp.sum(p, axis=-1)
    return p / jnp.maximum(s, 1e-30)[..., None], m, s


def _j_compress(k, v, cmp_pe, cmp_w1, cmp_w2, k_norm_c):
    n_chunks = k.shape[1] // CMP_STRIDE
    n_sub = CMP_LEN // CMP_STRIDE
    n_cmp = n_chunks - n_sub + 1

    def phi(x, i):
        ch = x[:, :n_chunks * CMP_STRIDE].reshape(x.shape[0], n_chunks, CMP_STRIDE, *x.shape[2:])
        w1s = cmp_w1[i].reshape(n_sub, CMP_STRIDE, HEAD_DIM, CMP_HIDDEN)
        hid = jnp.einsum('ld,ldh->h', cmp_pe[i], cmp_w1[i])
        for o in range(n_sub):
            hid = hid + jnp.einsum('nclgd,ldh->ncgh', ch[:, o:o + n_cmp], w1s[o])
        return jnp.einsum('ncgh,hd->ncgd', jax.nn.silu(hid), cmp_w2[i])

    return _j_rms(phi(k, 0), k_norm_c), phi(v, 1), jnp.arange(n_cmp) * CMP_STRIDE + CMP_LEN - 1


def _j_sel_blocks(k):
    n, length, kvh, hd = k.shape
    n_sel = -(-length // SEL_BLOCK)
    k = jnp.pad(k, ((0, 0), (0, n_sel * SEL_BLOCK - length), (0, 0), (0, 0)))
    return k.reshape(n, n_sel, SEL_BLOCK, kvh, hd).transpose(0, 3, 1, 2, 4)


def _j_nsa_sample_attention(q, kc, vc, ks, vs, kw, vw, cache_l, page_table, win_buf, bias_t,
                            cmp_pe, cmp_w1, cmp_w2, k_norm_c):
    db, t = q.shape[:2]
    n_pages = page_table.shape[1]
    past_len = n_pages * cache_l.shape[1]
    past = jnp.take(cache_l.reshape(cache_l.shape[0], -1), page_table.reshape(-1), axis=0)
    past = past.reshape(db, past_len, *cache_l.shape[2:])
    cat = lambda i, new: jnp.concatenate([past[:, :, i], new], axis=1)
    ck, cv, c_end = _j_compress(cat(0, kc), cat(1, vc), cmp_pe, cmp_w1, cmp_w2, k_norm_c)
    kb, vb = _j_sel_blocks(cat(2, ks)), _j_sel_blocks(cat(3, vs))
    n_sel = kb.shape[2]
    n_top = min(SEL_TOP_N, n_sel)
    c_start = c_end - CMP_LEN + 1
    j0 = jnp.arange(n_sel) * SEL_BLOCK
    sel_map = ((c_start[:, None] <= j0[None] + SEL_BLOCK - 1) & (c_end[:, None] >= j0[None])).astype(F32)
    qpos = past_len + jnp.arange(t)
    qg = q.reshape(db, t, KV_HEADS, REP, HEAD_DIM)
    dist_c = qpos[:, None] - c_end[None, :]
    bias_c = _j_bias(bias_t, dist_c).transpose(0, 2, 1).reshape(t, KV_HEADS, REP, -1)
    logits_c = jnp.einsum('nqgrd,ncgd->nqgrc', qg, ck) + bias_c[None]
    p_c, _, _ = _j_softmax(logits_c, (dist_c >= 0)[None, :, None, None, :])
    o_c = jnp.einsum('nqgrc,ncgd->nqgrd', p_c, cv)
    imp = jnp.einsum('nqgrc,cj->nqgj', p_c, sel_map)
    j = jnp.arange(n_sel)
    cur = (qpos // SEL_BLOCK)[:, None]
    forced = (j == 0) | (j == cur) | (j == cur - 1)
    future = j * SEL_BLOCK > qpos[:, None]
    imp = jnp.where(forced[None, :, None], jnp.inf, jnp.where(future[None, :, None], -jnp.inf, imp))
    _, idx = lax.top_k(imp, n_top)
    ni = jnp.arange(db)[:, None, None, None]
    gi = jnp.arange(KV_HEADS)[None, None, :, None]
    k_s = kb[ni, gi, idx]
    v_s = vb[ni, gi, idx]
    kpos = idx[..., None] * SEL_BLOCK + jnp.arange(SEL_BLOCK)
    dist_s = qpos[None, :, None, None, None] - kpos
    bt3 = bias_t[:, :BIAS_TABLE_LEN].reshape(KV_HEADS, REP, BIAS_TABLE_LEN)
    dc = jnp.clip(dist_s, 0, BIAS_TABLE_LEN - 1)
    bias_s = jnp.stack([bt3[g][:, dc[:, :, g]] for g in range(KV_HEADS)], axis=0)
    bias_s = bias_s.transpose(2, 3, 0, 1, 4, 5)
    logits_s = jnp.einsum('nqgrd,nqgtkd->nqgrtk', qg, k_s) + bias_s
    logits_s = logits_s.reshape(db, t, KV_HEADS, REP, n_top * SEL_BLOCK)
    mask_s = (dist_s >= 0).reshape(db, t, KV_HEADS, 1, n_top * SEL_BLOCK)
    p_s, _, _ = _j_softmax(logits_s, mask_s)
    o_s = jnp.einsum('nqgrk,nqgkd->nqgrd', p_s, v_s.reshape(db, t, KV_HEADS, n_top * SEL_BLOCK, HEAD_DIM))
    wb = win_buf.shape[1]
    k_all = jnp.concatenate([win_buf[:, :, 0], kw], axis=1)
    v_all = jnp.concatenate([win_buf[:, :, 1], vw], axis=1)
    length = k_all.shape[1]
    dist = wb + jnp.arange(t)[:, None] - jnp.arange(length)[None, :]
    valid = (dist >= 0) & (dist <= NSA_WINDOW)
    bias_w = _j_bias(bias_t, dist).transpose(0, 2, 1).reshape(t, KV_HEADS, REP, length)
    logits_w = jnp.einsum('btgrd,blgd->btgrl', qg, k_all) + bias_w
    p_w, _, _ = _j_softmax(logits_w, valid[None, :, None, None, :])
    o_w = jnp.einsum('btgrl,blgd->btgrd', p_w, v_all)
    r = lambda o: o.reshape(db, t, N_HEADS * HEAD_DIM)
    return r(o_c), r(o_s), r(o_w)


def _j_dil_sample_attention(qall, k_all, v_all, buf_len, bias_t):
    db, t = qall.shape[:2]
    outs = []
    for gi, (window, dil) in enumerate(DIL_GROUPS):
        n_taps = window // dil + 1
        dist = dil * jnp.arange(n_taps)
        idx = buf_len + jnp.arange(t)[:, None] - dist[None, :]
        valid = idx >= 0
        idx_c = jnp.maximum(idx, 0)
        kg = k_all[:, idx_c]
        vg = v_all[:, idx_c]
        bias = _j_bias(bias_t, dist).T.reshape(KV_HEADS, REP, n_taps)
        qg = qall[:, :, gi].reshape(db, t, KV_HEADS, REP, HEAD_DIM)
        logits = jnp.einsum('btgrd,btngd->btgrn', qg, kg) + bias
        p, m, den = _j_softmax(logits, valid[None, :, None, None, :])
        o = jnp.einsum('btgrn,btngd->btgrd', p, vg)
        outs.append((o.reshape(db, t, N_HEADS, HEAD_DIM), m.reshape(db, t, N_HEADS), den.reshape(db, t, N_HEADS)))
    ms = jnp.stack([m for _, m, _ in outs])
    mx = jnp.max(ms, axis=0)
    w = jnp.stack([den for _, _, den in outs]) * jnp.exp(ms - mx)
    os_ = jnp.stack([o for o, _, _ in outs])
    o = jnp.sum(w[..., None] * os_, axis=0) / jnp.sum(w, axis=0)[..., None]
    return o.reshape(db, t, N_HEADS * HEAD_DIM)


def kernel(x_prompt, x_sample, cache_nsa_kv, cache_win_kv, cache_dil_kv, page_table, rel_bias, a_attn_norm, a_w_in, a_q_norm, a_k_norm, a_cmp_pe, a_cmp_w1, a_cmp_w2, a_w_out, kv_norm, w_kv_shared, k_norm_shared, b_attn_norm, b_w_q, b_q_norm, b_w_out, mlp_norm, mlp_w1, mlp_w2):
    b, s, d = x_prompt.shape
    db, dt, _ = x_sample.shape
    n_a = a_w_in.shape[0]
    n_b = b_w_q.shape[0]
    bias_t = bias_by_distance(rel_bias)
    xpT = x_prompt.reshape(b * s, d).T
    xsT = x_sample.reshape(db * dt, d).T
    tok = lambda zT, *shape: zT.T.reshape(db, dt, *shape)
    rows_p, rows_s, wins_p, wins_s = [], [], [], []
    for l in range(n_a + n_b):
        w1T = mlp_w1[l].T.astype(BF16)
        w2T = mlp_w2[l].T.astype(BF16)
        if l < n_a:
            w_inT = a_w_in[l].T.astype(BF16)
            norms = jnp.stack([a_q_norm[l], a_k_norm[l][1], a_k_norm[l][2]])[..., None]
            cw = compress_weights(a_cmp_pe[l], a_cmp_w1[l], a_cmp_w2[l])
            woT = a_w_out[l].T.astype(BF16)
            zT = norm_proj(xpT, a_attn_norm[l], w_inT, norms, NSA_SEGS)
            oT, rows, win = nsa_attention_prompt(zT, b, s, bias_t, cw, a_k_norm[l][0])
            rows_p.append(rows)
            wins_p.append(win[:, -min(NSA_WINDOW, s):])
            xpT = outproj_mlp(xpT, oT, woT, mlp_norm[l], w1T, w2T)
            zs = tok(norm_proj(xsT, a_attn_norm[l], w_inT, norms, NSA_SEGS), NSA_IN)
            q = zs[..., :1024].reshape(db, dt, N_HEADS, HEAD_DIM)
            kv6 = zs[..., 1024:2560].reshape(db, dt, 6, KV_HEADS, HEAD_DIM)
            gates = zs[..., 2560:].reshape(db, dt, N_BRANCHES, N_HEADS, 1)
            o_c, o_s, o_w = _j_nsa_sample_attention(
                q, kv6[:, :, 0], kv6[:, :, 1], kv6[:, :, 2], kv6[:, :, 3], kv6[:, :, 4], kv6[:, :, 5],
                cache_nsa_kv[l], page_table, cache_win_kv[l], bias_t,
                a_cmp_pe[l], a_cmp_w1[l], a_cmp_w2[l], a_k_norm[l][0])
            hv = lambda o: o.reshape(db, dt, N_HEADS, HEAD_DIM)
            o = gates[:, :, 0] * hv(o_c) + gates[:, :, 1] * hv(o_s) + gates[:, :, 2] * hv(o_w)
            osT = o.reshape(db * dt, d).T.astype(BF16)
            rows_s.append(kv6[:, :, 0:4])
            wb = cache_win_kv.shape[2]
            wins_s.append(jnp.concatenate([cache_win_kv[l], kv6[:, :, 4:6]], axis=1)[:, -min(NSA_WINDOW, wb + dt):])
            xsT = outproj_mlp(xsT, osT, woT, mlp_norm[l], w1T, w2T)
        else:
            i = l - n_a
            if i == 0:
                w_kvT = w_kv_shared.T.astype(BF16)
                kn = k_norm_shared.reshape(1, HEAD_DIM, 1)
                kvp = norm_proj(xpT, kv_norm, w_kvT, kn, KV_SEGS).T.reshape(b, s, 2, KV_HEADS, HEAD_DIM)
                kvs = tok(norm_proj(xsT, kv_norm, w_kvT, kn, KV_SEGS), 2, KV_HEADS, HEAD_DIM)
                buf_len = cache_dil_kv.shape[1]
                dil_all = jnp.concatenate([cache_dil_kv, kvs], axis=1)
                dmax = max(w for w, _ in DIL_GROUPS)
                new_dil_p = kvp[:, -min(dmax, s):]
                new_dil_s = dil_all[:, -min(dmax, buf_len + dt):]
            w_qT = b_w_q[i].T.astype(BF16)
            qn = b_q_norm[i][..., None]
            woT = b_w_out[i].T.astype(BF16)
            qT = norm_proj(xpT, b_attn_norm[i], w_qT, qn, DIL_Q_SEGS)
            oT = dil_attention_prompt(qT, kvp[:, :, 0], kvp[:, :, 1], b, s, bias_t)
            xpT = outproj_mlp(xpT, oT, woT, mlp_norm[l], w1T, w2T)
            qs = tok(norm_proj(xsT, b_attn_norm[i], w_qT, qn, DIL_Q_SEGS), len(DIL_GROUPS), N_HEADS, HEAD_DIM)
            o = _j_dil_sample_attention(qs, dil_all[:, :, 0], dil_all[:, :, 1], buf_len, bias_t)
            xsT = outproj_mlp(xsT, o.reshape(db * dt, d).T.astype(BF16), woT, mlp_norm[l], w1T, w2T)
    return (xpT.T.reshape(b, s, d), xsT.T.reshape(db, dt, d),
            jnp.stack(rows_p), jnp.stack(rows_s), jnp.stack(wins_p), jnp.stack(wins_s),
            new_dil_p, new_dil_s)
```

```python
import functools
import math

import numpy as np
import jax
import jax.numpy as jnp
from jax import lax
from jax.experimental import pallas as pl
from jax.experimental.pallas import tpu as pltpu

F32 = jnp.float32
BF16 = jnp.bfloat16

D_MODEL = 1024
N_HEADS = 16
HEAD_DIM = 64
KV_HEADS = 4
REP = N_HEADS // KV_HEADS
GROUP_ROWS = REP * HEAD_DIM
D_FF = 4 * D_MODEL
EPS = 1e-6
ATTN_SCALE = HEAD_DIM ** -0.5
N_BUCKETS = 32
MAX_DISTANCE = 2048
CMP_LEN = 32
CMP_STRIDE = 16
CMP_HIDDEN = 2 * HEAD_DIM
SEL_BLOCK = 64
SEL_TOP_N = 16
NSA_WINDOW = 512
N_BRANCHES = 3
NSA_IN = N_HEADS * HEAD_DIM + 6 * KV_HEADS * HEAD_DIM + N_BRANCHES * N_HEADS
DIL_GROUPS = ((128, 1), (512, 4), (2048, 16))
PAGE_SIZE = 128

NEG = -1e30
LOG2E = math.log2(math.e)
VMEM_LIMIT = 56 * 1024 * 1024
BIAS_TABLE_LEN = 2048


def _bucket_of_distance(d):
    max_exact = N_BUCKETS // 2
    d = np.maximum(d, 0)
    ratio = np.log(np.maximum(d, 1).astype(np.float32) / np.float32(max_exact)) / np.float32(
        math.log(MAX_DISTANCE / max_exact))
    large = max_exact + (ratio * np.float32(N_BUCKETS - max_exact)).astype(np.int32)
    return np.where(d < max_exact, d, np.minimum(large, N_BUCKETS - 1)).astype(np.int32)


_BUCKETS = _bucket_of_distance(np.arange(BIAS_TABLE_LEN))
BIAS_CONST_FROM = int(np.argmax(_BUCKETS == N_BUCKETS - 1))
assert np.all(_BUCKETS[BIAS_CONST_FROM:] == N_BUCKETS - 1)


def _params(*sem):
    return pltpu.CompilerParams(dimension_semantics=sem, vmem_limit_bytes=VMEM_LIMIT)


def _token_tile(t, tm):
    tm = min(tm, t)
    assert t % tm == 0, (t, tm)
    return tm


def _proj_kernel(x_ref, g_ref, w_ref, n_ref, o_ref, *, segs):
    x = x_ref[...]
    ms = jnp.mean(x * x, axis=0, keepdims=True)
    xn = (x * lax.rsqrt(ms + EPS) * g_ref[...]).astype(BF16)
    tm = x.shape[1]
    for r0, nr, kind, ni, scale in segs:
        z = jnp.dot(w_ref[r0:r0 + nr, :], xn, preferred_element_type=F32)
        if kind == "norm":
            z3 = z.reshape(nr // HEAD_DIM, HEAD_DIM, tm)
            hs = jnp.mean(z3 * z3, axis=1, keepdims=True)
            z = (z3 * lax.rsqrt(hs + EPS) * n_ref[ni][None] * scale).reshape(nr, tm)
        elif kind == "sigmoid":
            z = 1.0 / (1.0 + jnp.exp(-z))
        o_ref[r0:r0 + nr, :] = z


def norm_proj(xT, gain, wT, norms, segs, tm=512):
    d, t = xT.shape
    tm = _token_tile(t, tm)
    n = wT.shape[0]
    return pl.pallas_call(
        functools.partial(_proj_kernel, segs=tuple(segs)),
        out_shape=jax.ShapeDtypeStruct((n, t), F32),
        grid=(t // tm,),
        in_specs=[
            pl.BlockSpec((d, tm), lambda i: (0, i)),
            pl.BlockSpec((d, 1), lambda i: (0, 0)),
            pl.BlockSpec((n, d), lambda i: (0, 0)),
            pl.BlockSpec(norms.shape, lambda i: (0, 0, 0)),
        ],
        out_specs=pl.BlockSpec((n, tm), lambda i: (0, i)),
        compiler_params=_params("parallel"),
        name="norm_proj",
    )(xT, gain.reshape(d, 1), wT, norms)


def _outmlp_kernel(x_ref, o_ref, wo_ref, g_ref, w1_ref, w2_ref, y_ref, x1_s, xn_s, acc_s):
    f = pl.program_id(1)

    @pl.when(f == 0)
    def _():
        x1 = x_ref[...] + jnp.dot(wo_ref[...], o_ref[...], preferred_element_type=F32)
        x1_s[...] = x1
        ms = jnp.mean(x1 * x1, axis=0, keepdims=True)
        xn_s[...] = (x1 * lax.rsqrt(ms + EPS) * g_ref[...]).astype(BF16)
        acc_s[...] = jnp.zeros_like(acc_s)

    h = jnp.maximum(jnp.dot(w1_ref[...], xn_s[...], preferred_element_type=F32), 0.0)
    acc_s[...] += jnp.dot(w2_ref[...], (h * h).astype(BF16), preferred_element_type=F32)

    @pl.when(f == pl.num_programs(1) - 1)
    def _():
        y_ref[...] = x1_s[...] + acc_s[...]


def outproj_mlp(xT, oT, woT, gain, w1T, w2T, tm=512, tf=1024):
    d, t = xT.shape
    tm = _token_tile(t, tm)
    dff = w1T.shape[0]
    return pl.pallas_call(
        _outmlp_kernel,
        out_shape=jax.ShapeDtypeStruct((d, t), F32),
        grid=(t // tm, dff // tf),
        in_specs=[
            pl.BlockSpec((d, tm), lambda i, f: (0, i)),
            pl.BlockSpec((d, tm), lambda i, f: (0, i)),
            pl.BlockSpec((d, d), lambda i, f: (0, 0)),
            pl.BlockSpec((d, 1), lambda i, f: (0, 0)),
            pl.BlockSpec((tf, d), lambda i, f: (f, 0)),
            pl.BlockSpec((d, tf), lambda i, f: (0, f)),
        ],
        out_specs=pl.BlockSpec((d, tm), lambda i, f: (0, i)),
        scratch_shapes=[pltpu.VMEM((d, tm), F32), pltpu.VMEM((d, tm), BF16), pltpu.VMEM((d, tm), F32)],
        compiler_params=_params("parallel", "arbitrary"),
        name="outproj_mlp",
    )(xT, oT, woT, gain.reshape(d, 1), w1T, w2T)


def _compress_kernel(ch_ref, w1_ref, pe_ref, w2_ref, kn_ref, o_ref):
    slot = pl.program_id(0)
    bn, c, _ = ch_ref.shape[1:]
    w1 = w1_ref[0]
    ab = jnp.dot(ch_ref[0].reshape(bn * c, CMP_STRIDE * HEAD_DIM), w1, preferred_element_type=F32)
    pe_a = jnp.dot(pe_ref[0, 0], w1[:, :CMP_HIDDEN], preferred_element_type=F32)[0:1]
    pe_b = jnp.dot(pe_ref[0, 1], w1[:, CMP_HIDDEN:], preferred_element_type=F32)[0:1]
    nxt = pltpu.roll(ab[:, CMP_HIDDEN:], bn * c - 1, 0)
    hid = ab[:, :CMP_HIDDEN] + nxt + (pe_a + pe_b)
    act = hid / (1.0 + jnp.exp(-hid))
    out = jnp.dot(act.astype(BF16), w2_ref[0], preferred_element_type=F32)
    ms = jnp.mean(out * out, axis=-1, keepdims=True)
    normed = out * lax.rsqrt(ms + EPS) * kn_ref[...]
    o_ref[0] = jnp.where(slot == 0, normed, out).reshape(bn, c, HEAD_DIM)


def compress(ch, w1ab, pe, w2, k_gain, bn):
    _, nb, c, w = ch.shape
    return pl.pallas_call(
        _compress_kernel,
        out_shape=jax.ShapeDtypeStruct((2, nb, c, HEAD_DIM), F32),
        grid=(2, nb // bn),
        in_specs=[
            pl.BlockSpec((1, bn, c, w), lambda s, i: (s, i, 0, 0)),
            pl.BlockSpec((1, w, 2 * CMP_HIDDEN), lambda s, i: (s, 0, 0)),
            pl.BlockSpec((1, 2, 16, w), lambda s, i: (s, 0, 0, 0)),
            pl.BlockSpec((1, CMP_HIDDEN, HEAD_DIM), lambda s, i: (s, 0, 0)),
            pl.BlockSpec((1, HEAD_DIM), lambda s, i: (0, 0)),
        ],
        out_specs=pl.BlockSpec((1, bn, c, HEAD_DIM), lambda s, i: (s, i, 0, 0)),
        compiler_params=_params("parallel", "parallel"),
        name="nsa_compress",
    )(ch, w1ab, pe, w2, k_gain.reshape(1, HEAD_DIM))


def compress_weights(cmp_pe, cmp_w1, cmp_w2):
    half = CMP_STRIDE * HEAD_DIM
    w1 = cmp_w1.reshape(2, 2, half, CMP_HIDDEN)
    w1ab = jnp.concatenate([w1[:, 0], w1[:, 1]], axis=-1).astype(BF16)
    pe = jnp.broadcast_to(cmp_pe.reshape(2, 2, 1, half), (2, 2, 16, half)).astype(BF16)
    return w1ab, pe, cmp_w2.astype(BF16)


CMP_CHUNK = 128


def _cmp_topk_kernel(q_ref, ck_ref, cv_ref, map_ref, bias_ref, oc_ref, sel_ref, v_s, *, tq, n_top, k_const):
    i = pl.program_id(2)
    c = ck_ref.shape[2]
    n_sel = map_ref.shape[0]
    n_chunks = c // CMP_CHUNK
    psum = None
    for r in range(REP):
        q = q_ref[r * HEAD_DIM:(r + 1) * HEAD_DIM, :]
        logits = []
        for jc in range(n_chunks):
            k0 = i * (tq // CMP_STRIDE) + jc * CMP_CHUNK
            start = pl.multiple_of(jnp.minimum(k0, k_const), 8)
            s = jnp.dot(ck_ref[0, 0, jc * CMP_CHUNK:(jc + 1) * CMP_CHUNK, :], q, preferred_element_type=F32)
            logits.append(s + bias_ref[r, pl.ds(start, CMP_CHUNK), :])
        m = functools.reduce(jnp.maximum, [jnp.max(l, axis=0, keepdims=True) for l in logits])
        m = jnp.where(m < 0.5 * NEG, 0.0, m)
        p = [jnp.exp2(l - m) for l in logits]
        den = functools.reduce(jnp.add, [jnp.sum(x, axis=0, keepdims=True) for x in p])
        inv = 1.0 / jnp.maximum(den, 1e-30)
        p = jnp.concatenate([x * inv for x in p], axis=0)
        oc_ref[r * HEAD_DIM:(r + 1) * HEAD_DIM, :] = jnp.dot(
            cv_ref[0, 0], p.astype(BF16), preferred_element_type=F32)
        psum = p if psum is None else psum + p
    hi = psum.astype(BF16)
    lo = (psum - hi.astype(F32)).astype(BF16)
    imp = (jnp.dot(map_ref[...], hi, preferred_element_type=F32)
           + jnp.dot(map_ref[...], lo, preferred_element_type=F32))
    qpos = i * tq + lax.broadcasted_iota(jnp.int32, (n_sel, tq), 1)
    j = lax.broadcasted_iota(jnp.int32, (n_sel, tq), 0)
    cur = qpos // SEL_BLOCK
    forced = (j == 0) | (j == cur) | (j == cur - 1)
    future = j * SEL_BLOCK > qpos
    v = jnp.where(forced, jnp.inf, jnp.where(future, -jnp.inf, imp))
    v_s[...] = v

    def count(a, cnt):
        va = v_s[pl.ds(a, 1), :]
        ahead = (va > v) | ((va == v) & (j > a))
        return cnt + jnp.where(ahead, 1.0, 0.0)

    rank = lax.fori_loop(0, n_sel, count, jnp.zeros((n_sel, tq), F32))
    sel_ref[0, 0] = jnp.where(rank < n_top, 1.0, 0.0)


def cmp_strip(bias_t, c, tq, rows):
    k = np.arange(rows)[:, None]
    tl = np.arange(tq)[None, :]
    d = CMP_STRIDE * (k - (c - 1)) + tl - (CMP_LEN - 1)
    idx = np.where(d < 0, BIAS_TABLE_LEN, np.minimum(d, BIAS_TABLE_LEN - 1)).astype(np.int32)
    return jnp.take(bias_t, jnp.asarray(idx), axis=1) * LOG2E


def cmp_topk(qT, ckr, cvTr, bias_t, b, s, tq=256):
    c = s // CMP_STRIDE
    n_sel = s // SEL_BLOCK
    n_top = min(SEL_TOP_N, n_sel)
    nq = s // tq
    k_const = (c - 1) + -(-(BIAS_CONST_FROM + CMP_LEN - 1) // CMP_STRIDE)
    k_const = -(-k_const // 8) * 8
    strip = cmp_strip(bias_t, c, tq, k_const + CMP_CHUNK)
    cc = np.arange(c)[::-1]
    c_end = cc * CMP_STRIDE + CMP_LEN - 1
    c_start = c_end - CMP_LEN + 1
    j0 = np.arange(n_sel)[:, None] * SEL_BLOCK
    sel_map = ((c_start[None] <= j0 + SEL_BLOCK - 1) & (c_end[None] >= j0) & (cc[None] < c - 1))
    sel_map = jnp.asarray(sel_map.astype(np.float32), dtype=BF16)
    return pl.pallas_call(
        functools.partial(_cmp_topk_kernel, tq=tq, n_top=n_top, k_const=k_const),
        out_shape=(jax.ShapeDtypeStruct((N_HEADS * HEAD_DIM, b * s), F32),
                   jax.ShapeDtypeStruct((b, KV_HEADS, n_sel, s), F32)),
        grid=(KV_HEADS, b, nq),
        in_specs=[
            pl.BlockSpec((GROUP_ROWS, tq), lambda g, n, i: (g, n * nq + i)),
            pl.BlockSpec((1, 1, c, HEAD_DIM), lambda g, n, i: (n, g, 0, 0)),
            pl.BlockSpec((1, 1, HEAD_DIM, c), lambda g, n, i: (n, g, 0, 0)),
            pl.BlockSpec((n_sel, c), lambda g, n, i: (0, 0)),
            pl.BlockSpec((REP, strip.shape[1], tq), lambda g, n, i: (g, 0, 0)),
        ],
        out_specs=(pl.BlockSpec((GROUP_ROWS, tq), lambda g, n, i: (g, n * nq + i)),
                   pl.BlockSpec((1, 1, n_sel, tq), lambda g, n, i: (n, g, 0, i))),
        scratch_shapes=[pltpu.VMEM((n_sel, tq), F32)],
        compiler_params=_params("parallel", "parallel", "parallel"),
        name="nsa_cmp_topk",
    )(qT, ckr, cvTr, sel_map, strip)


SEL_ROWS = 16


def _flash_kernel(*refs, tq, n_tiles_max, k_const, use_sel, normalize):
    nh = REP
    if use_sel:
        q_ref, k_ref, v_ref, bias_ref, sel_ref = refs[:5]
        outs = refs[5:]
    else:
        q_ref, k_ref, v_ref, bias_ref = refs[:4]
        sel_ref = None
        outs = refs[4:]
    if normalize:
        o_ref = outs[0]
        accs = outs[1:]
    else:
        o_ref, mo_ref, lo_ref = outs[:3]
        accs = outs[3:]
    i = pl.program_id(2)
    nkt = k_ref.shape[2]
    tk = k_ref.shape[3]
    seqlen = nkt * tk
    first = (seqlen - (i + 1) * tq) // tk
    k0 = i * tq + first * tk - seqlen + tk
    for acc in accs:
        acc[...] = jnp.zeros(acc.shape, F32)

    def tile(jt, carry):
        ms, ls = carry
        kt = first + jt
        k = k_ref[0, 0, kt]
        v = v_ref[0, 0, kt]
        start = pl.multiple_of(jnp.minimum(k0 + jt * tk, k_const), 8)
        new_m, new_l = [], []
        for r in range(nh):
            q = q_ref[r * HEAD_DIM:(r + 1) * HEAD_DIM, :]
            if use_sel:
                q = jnp.concatenate([q, sel_ref[0, 0, kt]], axis=0)
            s = jnp.dot(k, q, preferred_element_type=F32) + bias_ref[r, pl.ds(start, tk), :]
            m_new = jnp.maximum(ms[r], jnp.max(s, axis=0, keepdims=True))
            alpha = jnp.exp2(ms[r] - m_new)
            p = jnp.exp2(s - m_new)
            new_l.append(alpha * ls[r] + jnp.sum(p, axis=0, keepdims=True))
            accs[r][...] = alpha * accs[r][...] + jnp.dot(v, p.astype(BF16), preferred_element_type=F32)
            new_m.append(m_new)
        return tuple(new_m), tuple(new_l)

    init = (tuple(jnp.full((1, tq), NEG, F32) for _ in range(nh)),
            tuple(jnp.zeros((1, tq), F32) for _ in range(nh)))
    ms, ls = lax.fori_loop(0, jnp.minimum(nkt - first, n_tiles_max), tile, init)
    for r in range(nh):
        if normalize:
            o_ref[r * HEAD_DIM:(r + 1) * HEAD_DIM, :] = accs[r][...] * (1.0 / ls[r])
        else:
            o_ref[r * HEAD_DIM:(r + 1) * HEAD_DIM, :] = accs[r][...]
            mo_ref[0, r:r + 1, :] = ms[r]
            lo_ref[0, r:r + 1, :] = ls[r]


def flash_strip(bias_t, tq, tk, rows, dil, window):
    k = np.arange(rows)[:, None]
    tl = np.arange(tq)[None, :]
    d = k + tl - tk + 1
    bad = d < 0
    if window is not None:
        bad = bad | (d > window)
    idx = np.where(bad, BIAS_TABLE_LEN, np.minimum(d * dil, BIAS_TABLE_LEN - 1)).astype(np.int32)
    return jnp.take(bias_t, jnp.asarray(idx), axis=1) * LOG2E


def flash_reversed(qT, kr, vTr, bias_t, nseq, seqlen, sel=None, window=None, dil=1, tq=256, normalize=True):
    nkt, tk = kr.shape[2:4]
    assert tk % tq == 0 and seqlen == nkt * tk
    nq = seqlen // tq
    if window is None:
        k_const = -(-(BIAS_CONST_FROM + tk - 1) // 8) * 8
        rows = k_const + tk
        n_tiles_max = nkt
    else:
        n_tiles_max = (window + tk - 1) // tk + 1
        rows = (n_tiles_max + 1) * tk
        k_const = rows - tk
    strip = flash_strip(bias_t, tq, tk, rows, dil, window)
    h_rows = qT.shape[0]
    use_sel = sel is not None
    in_specs = [
        pl.BlockSpec((GROUP_ROWS, tq), lambda g, n, i: (g, n * nq + i)),
        pl.BlockSpec((1, 1) + kr.shape[2:], lambda g, n, i: (n, g, 0, 0, 0)),
        pl.BlockSpec((1, 1) + vTr.shape[2:], lambda g, n, i: (n, g, 0, 0, 0)),
        pl.BlockSpec((REP, rows, tq), lambda g, n, i: (g, 0, 0)),
    ]
    args = [qT, kr, vTr, strip]
    if use_sel:
        in_specs.append(pl.BlockSpec((1, 1) + sel.shape[2:4] + (tq,), lambda g, n, i: (n, g, 0, 0, i)))
        args.append(sel)
    o_spec = pl.BlockSpec((GROUP_ROWS, tq), lambda g, n, i: (g, n * nq + i))
    o_shape = jax.ShapeDtypeStruct((h_rows, nseq * seqlen), F32)
    if normalize:
        out_shape, out_specs = o_shape, o_spec
    else:
        st_shape = jax.ShapeDtypeStruct((KV_HEADS, REP, nseq * seqlen), F32)
        st_spec = pl.BlockSpec((1, REP, tq), lambda g, n, i: (g, 0, n * nq + i))
        out_shape, out_specs = (o_shape, st_shape, st_shape), (o_spec, st_spec, st_spec)
    return pl.pallas_call(
        functools.partial(_flash_kernel, tq=tq, n_tiles_max=n_tiles_max, k_const=k_const,
                          use_sel=use_sel, normalize=normalize),
        out_shape=out_shape,
        grid=(KV_HEADS, nseq, nq),
        in_specs=in_specs,
        out_specs=out_specs,
        scratch_shapes=[pltpu.VMEM((HEAD_DIM, tq), F32) for _ in range(REP)],
        compiler_params=_params("parallel", "parallel", "parallel"),
        name="flash_sel" if use_sel else ("flash_band" if window is not None else "flash_causal"),
    )(*args)


def _nsa_combine_kernel(oc_ref, os_ref, ow_ref, g_ref, o_ref):
    for h in range(N_HEADS):
        rows = slice(h * HEAD_DIM, (h + 1) * HEAD_DIM)
        o = (g_ref[h:h + 1, :] * oc_ref[rows, :]
             + g_ref[N_HEADS + h:N_HEADS + h + 1, :] * os_ref[rows, :]
             + g_ref[2 * N_HEADS + h:2 * N_HEADS + h + 1, :] * ow_ref[rows, :])
        o_ref[rows, :] = o.astype(BF16)


def nsa_combine(ocT, osT, owT, gatesT, tm=512):
    d, t = ocT.shape
    tm = _token_tile(t, tm)
    big = pl.BlockSpec((d, tm), lambda i: (0, i))
    return pl.pallas_call(
        _nsa_combine_kernel,
        out_shape=jax.ShapeDtypeStruct((d, t), BF16),
        grid=(t // tm,),
        in_specs=[big, big, big, pl.BlockSpec((N_BRANCHES * N_HEADS, tm), lambda i: (0, i))],
        out_specs=big,
        compiler_params=_params("parallel"),
        name="nsa_combine",
    )(ocT, osT, owT, gatesT)


def _dil_merge_kernel(acc_ref, m_ref, l_ref, o_ref):
    ng = acc_ref.shape[0]
    mx = functools.reduce(jnp.maximum, [m_ref[g] for g in range(ng)])
    w = [jnp.exp2(m_ref[g] - mx) for g in range(ng)]
    den = functools.reduce(jnp.add, [w[g] * l_ref[g] for g in range(ng)])
    inv = 1.0 / den
    for h in range(N_HEADS):
        rows = slice(h * HEAD_DIM, (h + 1) * HEAD_DIM)
        num = functools.reduce(jnp.add, [w[g][h:h + 1, :] * acc_ref[g, rows, :] for g in range(ng)])
        o_ref[rows, :] = (num * inv[h:h + 1, :]).astype(BF16)


def dil_merge(acc, m, l, tm=512):
    ng, d, t = acc.shape
    tm = _token_tile(t, tm)
    st = pl.BlockSpec((ng, N_HEADS, tm), lambda i: (0, 0, i))
    return pl.pallas_call(
        _dil_merge_kernel,
        out_shape=jax.ShapeDtypeStruct((d, t), BF16),
        grid=(t // tm,),
        in_specs=[pl.BlockSpec((ng, d, tm), lambda i: (0, 0, i)), st, st],
        out_specs=pl.BlockSpec((d, tm), lambda i: (0, i)),
        compiler_params=_params("parallel"),
        name="dil_merge",
    )(acc, m, l)


NSA_SEGS = (
    (0, 512, "norm", 0, ATTN_SCALE), (512, 512, "norm", 0, ATTN_SCALE),
    (1024, 512, "plain", 0, 1.0),
    (1536, 256, "norm", 1, 1.0),
    (1792, 256, "plain", 0, 1.0),
    (2048, 256, "norm", 2, 1.0),
    (2304, 256, "plain", 0, 1.0),
    (2560, N_BRANCHES * N_HEADS, "sigmoid", 0, 1.0),
)
KV_SEGS = ((0, 256, "norm", 0, 1.0), (256, 256, "plain", 0, 1.0))
DIL_Q_SEGS = tuple((512 * j, 512, "norm", j // 2, ATTN_SCALE) for j in range(6))


def bias_by_distance(rel_bias):
    bt = jnp.take(rel_bias, jnp.asarray(_BUCKETS), axis=0).T
    return jnp.concatenate([bt, jnp.full((N_HEADS, 1), NEG, F32)], axis=1)


def reversed_kv_tiles(k, v, tk, block_columns=False):
    n, length = k.shape[:2]
    nkt = length // tk
    kr = jnp.flip(k, axis=1).transpose(0, 2, 1, 3).reshape(n, KV_HEADS, nkt, tk, HEAD_DIM).astype(BF16)
    vr = jnp.flip(v, axis=1).transpose(0, 2, 1, 3).reshape(n, KV_HEADS, nkt, tk, HEAD_DIM)
    if block_columns:
        ind = (np.arange(tk)[:, None] // SEL_BLOCK == np.arange(SEL_ROWS)[None, :]).astype(np.float32)
        ind = jnp.broadcast_to(jnp.asarray(ind, dtype=BF16), kr.shape[:3] + (tk, SEL_ROWS))
        kr = jnp.concatenate([kr, ind], axis=-1)
    return kr, vr.transpose(0, 1, 2, 4, 3).astype(BF16)


def nsa_attention_prompt(zT, b, s, bias_t, cw, k_gain_cmp, tq=256, tk=256):
    qT = (zT[:1024] * LOG2E).astype(BF16)
    tok = zT[1024:2560].T
    rows = tok[:, :1024].reshape(b, s, 4, KV_HEADS, HEAD_DIM)
    win = tok[:, 1024:].reshape(b, s, 2, KV_HEADS, HEAD_DIM)
    gatesT = zT[2560:2560 + N_BRANCHES * N_HEADS]
    c = s // CMP_STRIDE
    ch = rows[:, :, 0:2].transpose(2, 0, 3, 1, 4).reshape(2, b * KV_HEADS, c, CMP_STRIDE * HEAD_DIM).astype(BF16)
    cmp = compress(ch, *cw, k_gain_cmp, bn=4)
    cmp = jnp.flip(cmp, axis=2).reshape(2, b, KV_HEADS, c, HEAD_DIM)
    ckr = cmp[0].astype(BF16)
    cvTr = cmp[1].transpose(0, 1, 3, 2).astype(BF16)
    ocT, sel = cmp_topk(qT, ckr, cvTr, bias_t, b, s)
    kr, vTr = reversed_kv_tiles(rows[:, :, 2], rows[:, :, 3], tk, block_columns=True)
    blocks = tk // SEL_BLOCK
    selr = jnp.where(jnp.flip(sel, axis=2) > 0.0, 0.0, NEG).reshape(b, KV_HEADS, s // tk, blocks, s)
    selr = jnp.pad(selr, ((0, 0), (0, 0), (0, 0), (0, SEL_ROWS - blocks), (0, 0))).astype(BF16)
    osT = flash_reversed(qT, kr, vTr, bias_t, b, s, sel=selr, tq=tq)
    kwr, vwTr = reversed_kv_tiles(win[:, :, 0], win[:, :, 1], tk)
    owT = flash_reversed(qT, kwr, vwTr, bias_t, b, s, window=NSA_WINDOW, tq=tq)
    oT = nsa_combine(ocT, osT, owT, gatesT)
    return oT, rows, win


def dil_attention_prompt(qallT, kp, vp, b, s, bias_t, tq=128):
    accs, ms, ls = [], [], []
    for gi, (window, dil) in enumerate(DIL_GROUPS):
        length = s // dil
        qg = (qallT[1024 * gi:1024 * (gi + 1)] * LOG2E).astype(BF16)
        qg = qg.reshape(1024, b, length, dil).transpose(0, 1, 3, 2).reshape(1024, b * s)

        def split(x):
            return x.reshape(b, length, dil, KV_HEADS, HEAD_DIM).transpose(0, 2, 1, 3, 4).reshape(
                b * dil, length, KV_HEADS, HEAD_DIM)

        kr, vTr = reversed_kv_tiles(split(kp), split(vp), tq)
        acc, m, l = flash_reversed(qg, kr, vTr, bias_t, b * dil, length, window=window // dil, dil=dil,
                                   tq=tq, normalize=False)

        def merge(x):
            r = x.shape[0]
            return x.reshape(r, b, dil, length).transpose(0, 1, 3, 2).reshape(r, b * s)

        accs.append(merge(acc))
        ms.append(merge(m.reshape(N_HEADS, b * s)))
        ls.append(merge(l.reshape(N_HEADS, b * s)))
    return dil_merge(jnp.stack(accs), jnp.stack(ms), jnp.stack(ls))


NEW_ROWS = 16
KV_WIDTH = KV_HEADS * HEAD_DIM


def _tn_dot(a, b):
    return lax.dot_general(a, b, (((0,), (0,)), ((), ())), preferred_element_type=F32)


def _group_diagonal(oT, lanes_per_group):
    grp = lax.broadcasted_iota(jnp.int32, (HEAD_DIM, oT.shape[1]), 1) // lanes_per_group
    out = jnp.zeros((HEAD_DIM, oT.shape[1]), F32)
    for g in range(KV_HEADS):
        out = jnp.where(grp == g, oT[g * HEAD_DIM:(g + 1) * HEAD_DIM, :], out)
    return out


def _nsa_sample_kernel(pt_ref, *refs, n_pages, page, n_top, past_len, dt):
    del pt_ref
    pages = refs[:n_pages]
    (win_ref, new_ref, q_ref, gate_ref, w1l_ref, w1f_ref, pe_ref, w2_ref, kn_ref, map_ref, gs_ref,
     bc_ref, bs_ref, bw_ref, o_ref, x_s, s_s, v_s) = refs[n_pages:]
    q = q_ref[0]
    lanes = q.shape[1]
    n_ch = past_len // CMP_STRIDE

    pairs = KV_WIDTH // 128
    for p in range(n_pages):
        for jj in range(2 * pairs):
            x_s[jj, p * page:(p + 1) * page, :] = pages[p][0, :, 128 * jj:128 * (jj + 1)]
    comp = []
    for slot in range(2):
        ab = [None] * KV_HEADS
        for l in range(CMP_STRIDE):
            for hp in range(pairs):
                xl = x_s.at[slot * pairs + hp][pl.ds(l, n_ch, stride=CMP_STRIDE), :].astype(BF16)
                for gg in range(2):
                    g = 2 * hp + gg
                    d = jnp.dot(xl[:, gg * HEAD_DIM:(gg + 1) * HEAD_DIM], w1l_ref[slot, l],
                                preferred_element_type=F32)
                    ab[g] = d if ab[g] is None else ab[g] + d
        w1f = w1f_ref[slot]
        pe = (jnp.dot(pe_ref[slot, 0], w1f[:, :CMP_HIDDEN], preferred_element_type=F32)
              + jnp.dot(pe_ref[slot, 1], w1f[:, CMP_HIDDEN:], preferred_element_type=F32))[0:1]
        hid = jnp.concatenate(
            [a[:, :CMP_HIDDEN] + pltpu.roll(a[:, CMP_HIDDEN:], n_ch - 1, 0) + pe for a in ab], axis=1)
        act = hid / (1.0 + jnp.exp(-hid))
        comp.append(jnp.dot(act.astype(BF16), w2_ref[slot], preferred_element_type=F32))
    ck = comp[0]
    lane_g = lax.broadcasted_iota(jnp.int32, ck.shape, 1) // HEAD_DIM
    sq = ck * ck
    scale = jnp.zeros_like(ck)
    for g in range(KV_HEADS):
        ms = jnp.sum(jnp.where(lane_g == g, sq, 0.0), axis=1, keepdims=True) * (1.0 / HEAD_DIM)
        scale = jnp.where(lane_g == g, lax.rsqrt(ms + EPS), scale)
    ck = (ck * scale * kn_ref[...]).astype(BF16)
    cv = comp[1].astype(BF16)

    sc = jnp.dot(ck, q, preferred_element_type=F32) + bc_ref[...]
    m = jnp.max(sc, axis=0, keepdims=True)
    m = jnp.where(m < 0.5 * NEG, 0.0, m)
    p = jnp.exp(sc - m)
    p = p * (1.0 / jnp.maximum(jnp.sum(p, axis=0, keepdims=True), 1e-30))
    ocT = _tn_dot(cv, p.astype(BF16))
    hi = p.astype(BF16)
    lo = (p - hi.astype(F32)).astype(BF16)
    m1 = (jnp.dot(map_ref[...], hi, preferred_element_type=F32)
          + jnp.dot(map_ref[...], lo, preferred_element_type=F32))
    hi = m1.astype(BF16)
    lo = (m1 - hi.astype(F32)).astype(BF16)
    imp = (jnp.dot(hi, gs_ref[...], preferred_element_type=F32)
           + jnp.dot(lo, gs_ref[...], preferred_element_type=F32))
    n_sel_pad = imp.shape[0]
    n_sel = -(-(past_len + dt) // SEL_BLOCK)
    j = lax.broadcasted_iota(jnp.int32, (n_sel_pad, lanes), 0)
    qpos = past_len + lax.broadcasted_iota(jnp.int32, (n_sel_pad, lanes), 1) % dt
    cur = qpos // SEL_BLOCK
    forced = (j == 0) | (j == cur) | (j == cur - 1)
    future = (j * SEL_BLOCK > qpos) | (j >= n_sel)
    v = jnp.where(forced, jnp.inf, jnp.where(future, -jnp.inf, imp))
    v_s[...] = v

    def count(a, cnt):
        va = v_s[pl.ds(a, 1), :]
        ahead = (va > v) | ((va == v) & (j > a))
        return cnt + jnp.where(ahead, 1.0, 0.0)

    rank = lax.fori_loop(0, n_sel, count, jnp.zeros((n_sel_pad, lanes), F32))
    sel = jnp.where((rank < n_top) & (j < n_sel), 0.0, NEG)

    blocks_per_page = page // SEL_BLOCK
    m_run = jnp.full((page, lanes), NEG, F32)
    for pg in range(n_pages):
        k = pages[pg][0, :, 2 * KV_WIDTH:3 * KV_WIDTH].astype(BF16)
        mask = jnp.concatenate(
            [jnp.broadcast_to(sel[blocks_per_page * pg + bb:blocks_per_page * pg + bb + 1], (SEL_BLOCK, lanes))
             for bb in range(blocks_per_page)], axis=0)
        s = jnp.dot(k, q, preferred_element_type=F32) + bs_ref[pg * page:(pg + 1) * page, :] + mask
        s_s[pg * page:(pg + 1) * page, :] = s
        m_run = jnp.maximum(m_run, s)
    nb_past = past_len // SEL_BLOCK
    s_new = (jnp.dot(new_ref[0, 0], q, preferred_element_type=F32) + bs_ref[past_len:past_len + NEW_ROWS, :]
             + sel[nb_past:nb_past + 1])
    m = jnp.maximum(jnp.max(m_run, axis=0, keepdims=True), jnp.max(s_new, axis=0, keepdims=True))
    pn = jnp.exp(s_new - m)
    den = jnp.sum(pn, axis=0, keepdims=True)
    acc = _tn_dot(new_ref[0, 1], pn.astype(BF16))
    for pg in range(n_pages):
        pp = jnp.exp(s_s[pg * page:(pg + 1) * page, :] - m)
        den = den + jnp.sum(pp, axis=0, keepdims=True)
        acc = acc + _tn_dot(pages[pg][0, :, 3 * KV_WIDTH:4 * KV_WIDTH].astype(BF16), pp.astype(BF16))
    osT = acc * (1.0 / den)

    wb = win_ref.shape[1]
    s_w = jnp.dot(win_ref[0, :, 0:KV_WIDTH].astype(BF16), q, preferred_element_type=F32) + bw_ref[0:wb, :]
    s_n = jnp.dot(new_ref[0, 2], q, preferred_element_type=F32) + bw_ref[wb:wb + NEW_ROWS, :]
    m = jnp.maximum(jnp.max(s_w, axis=0, keepdims=True), jnp.max(s_n, axis=0, keepdims=True))
    pw = jnp.exp(s_w - m)
    pn = jnp.exp(s_n - m)
    den = jnp.sum(pw, axis=0, keepdims=True) + jnp.sum(pn, axis=0, keepdims=True)
    acc = (_tn_dot(win_ref[0, :, KV_WIDTH:2 * KV_WIDTH].astype(BF16), pw.astype(BF16))
           + _tn_dot(new_ref[0, 3], pn.astype(BF16)))
    owT = acc * (1.0 / den)

    lpg = lanes // KV_HEADS
    o_ref[0] = (gate_ref[0, 0:1, :] * _group_diagonal(ocT, lpg)
                + gate_ref[0, 1:2, :] * _group_diagonal(osT, lpg)
                + gate_ref[0, 2:3, :] * _group_diagonal(owT, lpg))


def _lane_table(bias_t, dist, valid, dt):
    lanes = np.arange(N_HEADS * dt)
    h, t = lanes // dt, lanes % dt
    idx = np.where(valid[:, t], np.clip(dist[:, t], 0, BIAS_TABLE_LEN - 1), BIAS_TABLE_LEN).astype(np.int32)
    return bias_t[jnp.asarray(h)[None, :], jnp.asarray(idx)]


def nsa_sample_tables(bias_t, past_len, wb, dt):
    t = np.arange(dt)[None, :]
    n_ch = past_len // CMP_STRIDE
    c = np.arange(n_ch)[:, None]
    dist = past_len + t - (CMP_STRIDE * c + CMP_LEN - 1)
    bc = _lane_table(bias_t, dist, (c < n_ch - 1) & (dist >= 0), dt)
    s = np.arange(past_len)[:, None]
    n = np.arange(NEW_ROWS)[:, None]
    dist = np.concatenate([past_len + t - s, t - n])
    valid = np.concatenate([np.ones((past_len, dt), bool), (n < dt) & (t - n >= 0)])
    bs = _lane_table(bias_t, dist, valid, dt)
    l = np.arange(wb)[:, None]
    dist = np.concatenate([wb + t - l, t - n])
    valid = np.concatenate([wb + t - l <= NSA_WINDOW, (n < dt) & (t - n >= 0)])
    bw = _lane_table(bias_t, dist, valid, dt)
    return bc, bs, bw


def nsa_sample_constants(past_len, dt):
    n_ch = past_len // CMP_STRIDE
    n_sel = -(-(past_len + dt) // SEL_BLOCK)
    n_sel_pad = -(-n_sel // 8) * 8
    c = np.arange(n_ch)[None, :]
    j0 = np.arange(n_sel_pad)[:, None] * SEL_BLOCK
    c_start, c_end = CMP_STRIDE * c, CMP_STRIDE * c + CMP_LEN - 1
    sel_map = (c_start <= j0 + SEL_BLOCK - 1) & (c_end >= j0) & (c < n_ch - 1) & (j0 < n_sel * SEL_BLOCK)
    lanes = np.arange(N_HEADS * dt)
    grp, t = lanes // (REP * dt), lanes % dt
    gsum = (grp[:, None] == grp[None, :]) & (t[:, None] == t[None, :])
    return (jnp.asarray(sel_map.astype(np.float32), dtype=BF16), jnp.asarray(gsum.astype(np.float32), dtype=BF16))


def nsa_sample_attention(cache_pages, page_idx, win_cache, win_base, new_rows, qbd, gates, cw, k_gain_cmp, tables):
    db, n_pages = page_idx.shape
    page = cache_pages.shape[1]
    past_len = n_pages * page
    lanes = qbd.shape[2]
    dt = lanes // N_HEADS
    wb = win_cache.shape[1]
    w1ab, pe, w2 = cw
    w1l = w1ab.reshape(2, CMP_STRIDE, HEAD_DIM, 2 * CMP_HIDDEN)
    w2bd = jnp.einsum('shd,gk->sghkd', w2, jnp.eye(KV_HEADS, dtype=w2.dtype)).reshape(
        2, KV_HEADS * CMP_HIDDEN, KV_WIDTH)
    kn = jnp.tile(k_gain_cmp.reshape(1, HEAD_DIM), (1, KV_HEADS))
    sel_map, gsum = nsa_sample_constants(past_len, dt)
    bc, bs, bw = tables
    n_sel = -(-(past_len + dt) // SEL_BLOCK)
    assert past_len % SEL_BLOCK == 0 and dt <= SEL_BLOCK and page % SEL_BLOCK == 0

    def const(a):
        nd = a.ndim
        return pl.BlockSpec(a.shape, lambda i, pt: (0,) * nd)

    page_specs = [pl.BlockSpec((1, page, cache_pages.shape[2]), functools.partial(lambda p, i, pt: (pt[i, p], 0, 0), p))
                  for p in range(n_pages)]
    consts = [w1l, w1ab, pe, w2bd, kn, sel_map, gsum, bc, bs, bw]
    grid_spec = pltpu.PrefetchScalarGridSpec(
        num_scalar_prefetch=1,
        grid=(db,),
        in_specs=page_specs + [
            pl.BlockSpec((1, wb, win_cache.shape[2]), lambda i, pt: (win_base + i, 0, 0)),
            pl.BlockSpec((1,) + new_rows.shape[1:], lambda i, pt: (i, 0, 0, 0)),
            pl.BlockSpec((1,) + qbd.shape[1:], lambda i, pt: (i, 0, 0)),
            pl.BlockSpec((1,) + gates.shape[1:], lambda i, pt: (i, 0, 0)),
        ] + [const(a) for a in consts],
        out_specs=pl.BlockSpec((1, HEAD_DIM, lanes), lambda i, pt: (i, 0, 0)),
        scratch_shapes=[pltpu.VMEM((2 * KV_WIDTH // 128, past_len, 128), F32),
                        pltpu.VMEM((past_len, lanes), F32),
                        pltpu.VMEM((sel_map.shape[0], lanes), F32)],
    )
    return pl.pallas_call(
        functools.partial(_nsa_sample_kernel, n_pages=n_pages, page=page, n_top=min(SEL_TOP_N, n_sel),
                          past_len=past_len, dt=dt),
        out_shape=jax.ShapeDtypeStruct((db, HEAD_DIM, lanes), F32),
        grid_spec=grid_spec,
        compiler_params=_params("parallel"),
        name="nsa_sample",
    )(page_idx, *([cache_pages] * n_pages), win_cache, new_rows, qbd, gates, *consts)


def _dil_sample_kernel(kv_ref, new_ref, q_ref, b0_ref, b1_ref, b2_ref, o_ref, s_s, *, los, buf_len):
    lanes = q_ref.shape[3]
    chunk = 128
    ms, dens, accs = [], [], []
    for gi, (bias_ref, lo) in enumerate(zip((b0_ref, b1_ref, b2_ref), los)):
        q = q_ref[0, gi]
        n_chunks = (buf_len - lo) // chunk
        m_run = jnp.full((chunk, lanes), NEG, F32)
        for c in range(n_chunks):
            r0 = lo + c * chunk
            s = (jnp.dot(kv_ref[0, r0:r0 + chunk, 0:KV_WIDTH].astype(BF16), q, preferred_element_type=F32)
                 + bias_ref[c * chunk:(c + 1) * chunk, :])
            s_s[c * chunk:(c + 1) * chunk, :] = s
            m_run = jnp.maximum(m_run, s)
        s_new = (jnp.dot(new_ref[0, 0], q, preferred_element_type=F32)
                 + bias_ref[n_chunks * chunk:n_chunks * chunk + NEW_ROWS, :])
        m = jnp.maximum(jnp.max(m_run, axis=0, keepdims=True), jnp.max(s_new, axis=0, keepdims=True))
        pn = jnp.exp(s_new - m)
        den = jnp.sum(pn, axis=0, keepdims=True)
        acc = _tn_dot(new_ref[0, 1], pn.astype(BF16))
        for c in range(n_chunks):
            r0 = lo + c * chunk
            pp = jnp.exp(s_s[c * chunk:(c + 1) * chunk, :] - m)
            den = den + jnp.sum(pp, axis=0, keepdims=True)
            acc = acc + _tn_dot(kv_ref[0, r0:r0 + chunk, KV_WIDTH:2 * KV_WIDTH].astype(BF16), pp.astype(BF16))
        ms.append(m)
        dens.append(den)
        accs.append(acc)
    mx = functools.reduce(jnp.maximum, ms)
    w = [jnp.exp(m - mx) for m in ms]
    num = functools.reduce(jnp.add, [wg * a for wg, a in zip(w, accs)])
    den = functools.reduce(jnp.add, [wg * d for wg, d in zip(w, dens)])
    o_ref[0] = _group_diagonal(num * (1.0 / den), lanes // KV_HEADS)


def dil_sample_tables(bias_t, buf_len, dt):
    t = np.arange(dt)[None, :]
    n = np.arange(NEW_ROWS)[:, None]
    tables, los = [], []
    for window, dil in DIL_GROUPS:
        lo = max(0, buf_len - window) // 128 * 128
        s = np.arange(lo, buf_len)[:, None]
        dist = np.concatenate([buf_len + t - s, t - n])
        valid = np.concatenate([np.ones((buf_len - lo, dt), bool), (n < dt) & (t - n >= 0)])
        valid = valid & (dist % dil == 0) & (dist <= window)
        tables.append(_lane_table(bias_t, dist, valid, dt))
        los.append(lo)
    return tables, tuple(los)


def dil_sample_attention(kv_cache, new_rows, qbd3, tables, los):
    db, buf_len, width = kv_cache.shape
    lanes = qbd3.shape[3]
    assert buf_len % 128 == 0
    tspec = [pl.BlockSpec(tb.shape, lambda i: (0, 0)) for tb in tables]
    return pl.pallas_call(
        functools.partial(_dil_sample_kernel, los=los, buf_len=buf_len),
        out_shape=jax.ShapeDtypeStruct((db, HEAD_DIM, lanes), F32),
        grid=(db,),
        in_specs=[
            pl.BlockSpec((1, buf_len, width), lambda i: (i, 0, 0)),
            pl.BlockSpec((1,) + new_rows.shape[1:], lambda i: (i, 0, 0, 0)),
            pl.BlockSpec((1,) + qbd3.shape[1:], lambda i: (i, 0, 0, 0)),
        ] + tspec,
        out_specs=pl.BlockSpec((1, HEAD_DIM, lanes), lambda i: (i, 0, 0)),
        scratch_shapes=[pltpu.VMEM((buf_len, lanes), F32)],
        compiler_params=_params("parallel"),
        name="dil_sample",
    )(kv_cache, new_rows, qbd3, *tables)


def lane_queries(qT, db, dt):
    q = qT.reshape(KV_HEADS, REP, HEAD_DIM, db, dt).transpose(3, 0, 2, 1, 4).reshape(db, KV_HEADS, HEAD_DIM, REP * dt)
    eye = jnp.eye(KV_HEADS, dtype=q.dtype)
    qbd = q[:, :, :, None, :] * eye[None, :, None, :, None]
    return qbd.reshape(db, KV_WIDTH, KV_HEADS * REP * dt).astype(BF16)


def lanes_to_features(o, db, dt):
    o = o.reshape(db, HEAD_DIM, KV_HEADS, REP, dt).transpose(2, 3, 1, 0, 4)
    return o.reshape(N_HEADS * HEAD_DIM, db * dt)


def new_token_rows(zT_rows, db, dt):
    n = zT_rows.shape[0] // KV_WIDTH
    r = zT_rows.reshape(n, KV_WIDTH, db, dt).transpose(2, 0, 3, 1)
    return jnp.pad(r, ((0, 0), (0, 0), (0, NEW_ROWS - dt), (0, 0))).astype(BF16)


def kernel(x_prompt, x_sample, cache_nsa_kv, cache_win_kv, cache_dil_kv, page_table, rel_bias, a_attn_norm, a_w_in, a_q_norm, a_k_norm, a_cmp_pe, a_cmp_w1, a_cmp_w2, a_w_out, kv_norm, w_kv_shared, k_norm_shared, b_attn_norm, b_w_q, b_q_norm, b_w_out, mlp_norm, mlp_w1, mlp_w2):
    b, s, d = x_prompt.shape
    db, dt, _ = x_sample.shape
    n_a = a_w_in.shape[0]
    n_b = b_w_q.shape[0]
    n_pool, page = cache_nsa_kv.shape[1:3]
    wb = cache_win_kv.shape[2]
    buf_len = cache_dil_kv.shape[1]
    past_len = page_table.shape[1] * page
    assert N_HEADS * dt == 128
    bias_t = bias_by_distance(rel_bias)
    xpT = x_prompt.reshape(b * s, d).T
    xsT = x_sample.reshape(db * dt, d).T
    cache_pages = cache_nsa_kv.reshape(n_a * n_pool, page, -1)
    win_cache = cache_win_kv.reshape(n_a * db, wb, -1)
    nsa_tables = nsa_sample_tables(bias_t, past_len, wb, dt)
    rows_p, rows_s, wins_p, wins_s = [], [], [], []
    for l in range(n_a + n_b):
        w1T = mlp_w1[l].T.astype(BF16)
        w2T = mlp_w2[l].T.astype(BF16)
        if l < n_a:
            w_inT = a_w_in[l].T.astype(BF16)
            norms = jnp.stack([a_q_norm[l], a_k_norm[l][1], a_k_norm[l][2]])[..., None]
            cw = compress_weights(a_cmp_pe[l], a_cmp_w1[l], a_cmp_w2[l])
            woT = a_w_out[l].T.astype(BF16)
            zT = norm_proj(xpT, a_attn_norm[l], w_inT, norms, NSA_SEGS)
            oT, rows, win = nsa_attention_prompt(zT, b, s, bias_t, cw, a_k_norm[l][0])
            rows_p.append(rows)
            wins_p.append(win[:, -min(NSA_WINDOW, s):])
            xpT = outproj_mlp(xpT, oT, woT, mlp_norm[l], w1T, w2T)
            zs = norm_proj(xsT, a_attn_norm[l], w_inT, norms, NSA_SEGS)
            gates = zs[2560:2560 + N_BRANCHES * N_HEADS].reshape(N_BRANCHES, KV_HEADS, REP, db, dt)
            gates = gates.transpose(3, 0, 1, 2, 4).reshape(db, N_BRANCHES, N_HEADS * dt)
            o = nsa_sample_attention(
                cache_pages, page_table + l * n_pool, win_cache, l * db, new_token_rows(zs[1536:2560], db, dt),
                lane_queries(zs[:1024], db, dt), gates, cw, a_k_norm[l][0], nsa_tables)
            kv6 = zs[1024:2560].T.reshape(db, dt, 6, KV_HEADS, HEAD_DIM)
            rows_s.append(kv6[:, :, 0:4])
            wins_s.append(jnp.concatenate([cache_win_kv[l], kv6[:, :, 4:6]], axis=1)[:, -min(NSA_WINDOW, wb + dt):])
            xsT = outproj_mlp(xsT, lanes_to_features(o, db, dt).astype(BF16), woT, mlp_norm[l], w1T, w2T)
        else:
            i = l - n_a
            if i == 0:
                w_kvT = w_kv_shared.T.astype(BF16)
                kn = k_norm_shared.reshape(1, HEAD_DIM, 1)
                kvp = norm_proj(xpT, kv_norm, w_kvT, kn, KV_SEGS).T.reshape(b, s, 2, KV_HEADS, HEAD_DIM)
                kvsT = norm_proj(xsT, kv_norm, w_kvT, kn, KV_SEGS)
                kvs = kvsT.T.reshape(db, dt, 2, KV_HEADS, HEAD_DIM)
                dmax = max(w for w, _ in DIL_GROUPS)
                new_dil_p = kvp[:, -min(dmax, s):]
                new_dil_s = jnp.concatenate([cache_dil_kv, kvs], axis=1)[:, -min(dmax, buf_len + dt):]
                dil_cache = cache_dil_kv.reshape(db, buf_len, -1)
                dil_new = new_token_rows(kvsT, db, dt)
                dil_tables, dil_los = dil_sample_tables(bias_t, buf_len, dt)
            w_qT = b_w_q[i].T.astype(BF16)
            qn = b_q_norm[i][..., None]
            woT = b_w_out[i].T.astype(BF16)
            qT = norm_proj(xpT, b_attn_norm[i], w_qT, qn, DIL_Q_SEGS)
            oT = dil_attention_prompt(qT, kvp[:, :, 0], kvp[:, :, 1], b, s, bias_t)
            xpT = outproj_mlp(xpT, oT, woT, mlp_norm[l], w1T, w2T)
            qs = norm_proj(xsT, b_attn_norm[i], w_qT, qn, DIL_Q_SEGS)
            qbd3 = jnp.stack([lane_queries(qs[1024 * gi:1024 * (gi + 1)], db, dt) for gi in range(len(DIL_GROUPS))], axis=1)
            o = dil_sample_attention(dil_cache, dil_new, qbd3, dil_tables, dil_los)
            xsT = outproj_mlp(xsT, lanes_to_features(o, db, dt).astype(BF16), woT, mlp_norm[l], w1T, w2T)
    return (xpT.T.reshape(b, s, d), xsT.T.reshape(db, dt, d),
            jnp.stack(rows_p), jnp.stack(rows_s), jnp.stack(wins_p), jnp.stack(wins_s),
            new_dil_p, new_dil_s)
```

```python
import functools
import math

import numpy as np
import jax
import jax.numpy as jnp
from jax import lax
from jax.experimental import pallas as pl
from jax.experimental.pallas import tpu as pltpu

F32 = jnp.float32
BF16 = jnp.bfloat16

D_MODEL = 1024
N_HEADS = 16
HEAD_DIM = 64
KV_HEADS = 4
REP = N_HEADS // KV_HEADS
GROUP_ROWS = REP * HEAD_DIM
KV_WIDTH = KV_HEADS * HEAD_DIM
D_FF = 4 * D_MODEL
EPS = 1e-6
ATTN_SCALE = HEAD_DIM ** -0.5
N_BUCKETS = 32
MAX_DISTANCE = 2048
CMP_LEN = 32
CMP_STRIDE = 16
CMP_HIDDEN = 2 * HEAD_DIM
SEL_BLOCK = 64
SEL_TOP_N = 16
NSA_WINDOW = 512
N_BRANCHES = 3
NSA_IN = N_HEADS * HEAD_DIM + 6 * KV_HEADS * HEAD_DIM + N_BRANCHES * N_HEADS
DIL_GROUPS = ((128, 1), (512, 4), (2048, 16))
PAGE_SIZE = 128

NEG = -1e30
LOG2E = math.log2(math.e)
VMEM_LIMIT = 56 * 1024 * 1024
BIAS_TABLE_LEN = 2048


def _bucket_of_distance(d):
    max_exact = N_BUCKETS // 2
    d = np.maximum(d, 0)
    ratio = np.log(np.maximum(d, 1).astype(np.float32) / np.float32(max_exact)) / np.float32(
        math.log(MAX_DISTANCE / max_exact))
    large = max_exact + (ratio * np.float32(N_BUCKETS - max_exact)).astype(np.int32)
    return np.where(d < max_exact, d, np.minimum(large, N_BUCKETS - 1)).astype(np.int32)


_BUCKETS = _bucket_of_distance(np.arange(BIAS_TABLE_LEN))
BIAS_CONST_FROM = int(np.argmax(_BUCKETS == N_BUCKETS - 1))
assert np.all(_BUCKETS[BIAS_CONST_FROM:] == N_BUCKETS - 1)


def _params(*sem):
    return pltpu.CompilerParams(dimension_semantics=sem, vmem_limit_bytes=VMEM_LIMIT)


def _token_tile(t, tm):
    tm = min(tm, t)
    assert t % tm == 0, (t, tm)
    return tm


def _tn_dot(a, b):
    return lax.dot_general(a, b, (((0,), (0,)), ((), ())), preferred_element_type=F32)


def _bias_lookup_kernel(bkt_ref, tab_ref, o_ref):
    bkt = bkt_ref[...]
    v = jnp.full(bkt.shape, NEG, F32)
    for bb in range(N_BUCKETS):
        v = jnp.where(bkt == bb, tab_ref[0, bb:bb + 1, :], v)
    o_ref[0] = v


def bias_lookup(dist, valid, tab):
    rows, lanes = dist.shape
    bkt = np.where(valid, _BUCKETS[np.clip(dist, 0, BIAS_TABLE_LEN - 1)], N_BUCKETS).astype(np.int32)
    rb = next(r for r in range(min(rows, 512) // 8 * 8, 0, -8) if rows % r == 0)
    nh = tab.shape[0]
    return pl.pallas_call(
        _bias_lookup_kernel,
        out_shape=jax.ShapeDtypeStruct((nh, rows, lanes), F32),
        grid=(rows // rb, nh),
        in_specs=[pl.BlockSpec((rb, lanes), lambda r, h: (r, 0)),
                  pl.BlockSpec((1, N_BUCKETS, lanes), lambda r, h: (h, 0, 0))],
        out_specs=pl.BlockSpec((1, rb, lanes), lambda r, h: (h, r, 0)),
        compiler_params=_params("parallel", "parallel"),
        name="bias_lookup",
    )(jnp.asarray(bkt), tab)


def _cached(tables, key, build):
    if tables is None:
        return build()
    if key not in tables:
        tables[key] = build()
    return tables[key]


def head_rows(rel_bias, lanes):
    return jnp.broadcast_to((rel_bias.T * LOG2E)[:, :, None], (N_HEADS, N_BUCKETS, lanes))


def _proj_kernel(x_ref, g_ref, w_ref, n_ref, o_ref, *, segs):
    x = x_ref[...]
    ms = jnp.mean(x * x, axis=0, keepdims=True)
    xn = (x * lax.rsqrt(ms + EPS) * g_ref[...]).astype(BF16)
    tm = x.shape[1]
    for r0, nr, kind, ni, scale in segs:
        z = jnp.dot(w_ref[r0:r0 + nr, :], xn, preferred_element_type=F32)
        if kind == "norm":
            z3 = z.reshape(nr // HEAD_DIM, HEAD_DIM, tm)
            hs = jnp.mean(z3 * z3, axis=1, keepdims=True)
            z = (z3 * lax.rsqrt(hs + EPS) * n_ref[ni][None] * scale).reshape(nr, tm)
        elif kind == "sigmoid":
            z = 1.0 / (1.0 + jnp.exp(-z))
        o_ref[r0:r0 + nr, :] = z


def norm_proj(xT, gain, wT, norms, segs, tm=512):
    d, t = xT.shape
    tm = _token_tile(t, tm)
    n = wT.shape[0]
    return pl.pallas_call(
        functools.partial(_proj_kernel, segs=tuple(segs)),
        out_shape=jax.ShapeDtypeStruct((n, t), F32),
        grid=(t // tm,),
        in_specs=[
            pl.BlockSpec((d, tm), lambda i: (0, i)),
            pl.BlockSpec((d, 1), lambda i: (0, 0)),
            pl.BlockSpec((n, d), lambda i: (0, 0)),
            pl.BlockSpec(norms.shape, lambda i: (0, 0, 0)),
        ],
        out_specs=pl.BlockSpec((n, tm), lambda i: (0, i)),
        compiler_params=_params("parallel"),
        name="norm_proj",
    )(xT, gain.reshape(d, 1), wT, norms)


def _outmlp_kernel(x_ref, o_ref, wo_ref, g_ref, w1_ref, w2_ref, y_ref, x1_s, xn_s, acc_s):
    f = pl.program_id(1)

    @pl.when(f == 0)
    def _():
        x1 = x_ref[...] + jnp.dot(wo_ref[...], o_ref[...], preferred_element_type=F32)
        x1_s[...] = x1
        ms = jnp.mean(x1 * x1, axis=0, keepdims=True)
        xn_s[...] = (x1 * lax.rsqrt(ms + EPS) * g_ref[...]).astype(BF16)
        acc_s[...] = jnp.zeros_like(acc_s)

    h = jnp.maximum(jnp.dot(w1_ref[...], xn_s[...], preferred_element_type=F32), 0.0)
    acc_s[...] += jnp.dot(w2_ref[...], (h * h).astype(BF16), preferred_element_type=F32)

    @pl.when(f == pl.num_programs(1) - 1)
    def _():
        y_ref[...] = x1_s[...] + acc_s[...]


def outproj_mlp(xT, oT, woT, gain, w1T, w2T, tm=512, tf=1024):
    d, t = xT.shape
    tm = _token_tile(t, tm)
    dff = w1T.shape[0]
    return pl.pallas_call(
        _outmlp_kernel,
        out_shape=jax.ShapeDtypeStruct((d, t), F32),
        grid=(t // tm, dff // tf),
        in_specs=[
            pl.BlockSpec((d, tm), lambda i, f: (0, i)),
            pl.BlockSpec((d, tm), lambda i, f: (0, i)),
            pl.BlockSpec((d, d), lambda i, f: (0, 0)),
            pl.BlockSpec((d, 1), lambda i, f: (0, 0)),
            pl.BlockSpec((tf, d), lambda i, f: (f, 0)),
            pl.BlockSpec((d, tf), lambda i, f: (0, f)),
        ],
        out_specs=pl.BlockSpec((d, tm), lambda i, f: (0, i)),
        scratch_shapes=[pltpu.VMEM((d, tm), F32), pltpu.VMEM((d, tm), BF16), pltpu.VMEM((d, tm), F32)],
        compiler_params=_params("parallel", "arbitrary"),
        name="outproj_mlp",
    )(xT, oT, woT, gain.reshape(d, 1), w1T, w2T)


def _compress_kernel(ch_ref, w1_ref, pe_ref, w2_ref, kn_ref, o_ref):
    slot = pl.program_id(0)
    bn, c, _ = ch_ref.shape[1:]
    w1 = w1_ref[0]
    ab = jnp.dot(ch_ref[0].reshape(bn * c, CMP_STRIDE * HEAD_DIM), w1, preferred_element_type=F32)
    pe_a = jnp.dot(pe_ref[0, 0], w1[:, :CMP_HIDDEN], preferred_element_type=F32)[0:1]
    pe_b = jnp.dot(pe_ref[0, 1], w1[:, CMP_HIDDEN:], preferred_element_type=F32)[0:1]
    nxt = pltpu.roll(ab[:, CMP_HIDDEN:], bn * c - 1, 0)
    hid = ab[:, :CMP_HIDDEN] + nxt + (pe_a + pe_b)
    act = hid / (1.0 + jnp.exp(-hid))
    out = jnp.dot(act.astype(BF16), w2_ref[0], preferred_element_type=F32)
    ms = jnp.mean(out * out, axis=-1, keepdims=True)
    normed = out * lax.rsqrt(ms + EPS) * kn_ref[...]
    o_ref[0] = jnp.where(slot == 0, normed, out).reshape(bn, c, HEAD_DIM)


def compress(ch, w1ab, pe, w2, k_gain, bn):
    _, nb, c, w = ch.shape
    return pl.pallas_call(
        _compress_kernel,
        out_shape=jax.ShapeDtypeStruct((2, nb, c, HEAD_DIM), F32),
        grid=(2, nb // bn),
        in_specs=[
            pl.BlockSpec((1, bn, c, w), lambda s, i: (s, i, 0, 0)),
            pl.BlockSpec((1, w, 2 * CMP_HIDDEN), lambda s, i: (s, 0, 0)),
            pl.BlockSpec((1, 2, 16, w), lambda s, i: (s, 0, 0, 0)),
            pl.BlockSpec((1, CMP_HIDDEN, HEAD_DIM), lambda s, i: (s, 0, 0)),
            pl.BlockSpec((1, HEAD_DIM), lambda s, i: (0, 0)),
        ],
        out_specs=pl.BlockSpec((1, bn, c, HEAD_DIM), lambda s, i: (s, i, 0, 0)),
        compiler_params=_params("parallel", "parallel"),
        name="nsa_compress",
    )(ch, w1ab, pe, w2, k_gain.reshape(1, HEAD_DIM))


def compress_weights(cmp_pe, cmp_w1, cmp_w2):
    half = CMP_STRIDE * HEAD_DIM
    w1 = cmp_w1.reshape(2, 2, half, CMP_HIDDEN)
    w1ab = jnp.concatenate([w1[:, 0], w1[:, 1]], axis=-1).astype(BF16)
    pe = jnp.broadcast_to(cmp_pe.reshape(2, 2, 1, half), (2, 2, 16, half)).astype(BF16)
    return w1ab, pe, cmp_w2.astype(BF16)


CMP_CHUNK = 128


def _cmp_topk_kernel(q_ref, ck_ref, cv_ref, map_ref, bias_ref, oc_ref, sel_ref, v_s, *, tq, n_top, k_const):
    i = pl.program_id(2)
    c = ck_ref.shape[2]
    n_sel = map_ref.shape[0]
    n_chunks = c // CMP_CHUNK
    psum = None
    for r in range(REP):
        q = q_ref[r * HEAD_DIM:(r + 1) * HEAD_DIM, :]
        logits = []
        for jc in range(n_chunks):
            k0 = i * (tq // CMP_STRIDE) + jc * CMP_CHUNK
            start = pl.multiple_of(jnp.minimum(k0, k_const), 8)
            s = jnp.dot(ck_ref[0, 0, jc * CMP_CHUNK:(jc + 1) * CMP_CHUNK, :], q, preferred_element_type=F32)
            logits.append(s + bias_ref[r, pl.ds(start, CMP_CHUNK), :])
        m = functools.reduce(jnp.maximum, [jnp.max(l, axis=0, keepdims=True) for l in logits])
        m = jnp.where(m < 0.5 * NEG, 0.0, m)
        p = [jnp.exp2(l - m) for l in logits]
        den = functools.reduce(jnp.add, [jnp.sum(x, axis=0, keepdims=True) for x in p])
        inv = 1.0 / jnp.maximum(den, 1e-30)
        p = jnp.concatenate([x * inv for x in p], axis=0)
        oc_ref[r * HEAD_DIM:(r + 1) * HEAD_DIM, :] = jnp.dot(
            cv_ref[0, 0], p.astype(BF16), preferred_element_type=F32)
        psum = p if psum is None else psum + p
    hi = psum.astype(BF16)
    lo = (psum - hi.astype(F32)).astype(BF16)
    imp = (jnp.dot(map_ref[...], hi, preferred_element_type=F32)
           + jnp.dot(map_ref[...], lo, preferred_element_type=F32))
    qpos = i * tq + lax.broadcasted_iota(jnp.int32, (n_sel, tq), 1)
    j = lax.broadcasted_iota(jnp.int32, (n_sel, tq), 0)
    cur = qpos // SEL_BLOCK
    forced = (j == 0) | (j == cur) | (j == cur - 1)
    future = j * SEL_BLOCK > qpos
    v = jnp.where(forced, jnp.inf, jnp.where(future, -jnp.inf, imp))
    v_s[...] = v
    groups = n_sel // 8
    row = lax.broadcasted_iota(jnp.int32, (8, 128), 0)
    for lt in range(tq // 128):
        lanes = slice(lt * 128, (lt + 1) * 128)
        vg = [v_s[8 * gi:8 * gi + 8, lanes] for gi in range(groups)]
        cnt = [jnp.zeros((8, 128), F32) for _ in range(groups)]
        for a in range(n_sel):
            va = v_s[a:a + 1, lanes]
            for gi in range(groups):
                if 8 * gi > a:
                    ahead = va >= vg[gi]
                elif 8 * gi + 7 < a:
                    ahead = va > vg[gi]
                else:
                    ahead = (va > vg[gi]) | ((va == vg[gi]) & (row > a - 8 * gi))
                cnt[gi] = cnt[gi] + jnp.where(ahead, 1.0, 0.0)
        for gi in range(groups):
            sel_ref[0, 0, 8 * gi:8 * gi + 8, lanes] = jnp.where(cnt[gi] < n_top, 1.0, 0.0)


def cmp_strip(rel_bias, c, tq, rows):
    k = np.arange(rows)[:, None]
    tl = np.arange(tq)[None, :]
    d = CMP_STRIDE * (k - (c - 1)) + tl - (CMP_LEN - 1)
    return bias_lookup(d, d >= 0, head_rows(rel_bias, tq))


def cmp_topk(qT, ckr, cvTr, rel_bias, b, s, tq=256, tables=None):
    c = s // CMP_STRIDE
    n_sel = s // SEL_BLOCK
    n_top = min(SEL_TOP_N, n_sel)
    nq = s // tq
    k_const = (c - 1) + -(-(BIAS_CONST_FROM + CMP_LEN - 1) // CMP_STRIDE)
    k_const = -(-k_const // 8) * 8
    strip = _cached(tables, ("cmp", c, tq), lambda: cmp_strip(rel_bias, c, tq, k_const + CMP_CHUNK))
    cc = np.arange(c)[::-1]
    c_end = cc * CMP_STRIDE + CMP_LEN - 1
    c_start = c_end - CMP_LEN + 1
    j0 = np.arange(n_sel)[:, None] * SEL_BLOCK
    sel_map = ((c_start[None] <= j0 + SEL_BLOCK - 1) & (c_end[None] >= j0) & (cc[None] < c - 1))
    sel_map = jnp.asarray(sel_map.astype(np.float32), dtype=BF16)
    return pl.pallas_call(
        functools.partial(_cmp_topk_kernel, tq=tq, n_top=n_top, k_const=k_const),
        out_shape=(jax.ShapeDtypeStruct((N_HEADS * HEAD_DIM, b * s), F32),
                   jax.ShapeDtypeStruct((b, KV_HEADS, n_sel, s), F32)),
        grid=(KV_HEADS, b, nq),
        in_specs=[
            pl.BlockSpec((GROUP_ROWS, tq), lambda g, n, i: (g, n * nq + i)),
            pl.BlockSpec((1, 1, c, HEAD_DIM), lambda g, n, i: (n, g, 0, 0)),
            pl.BlockSpec((1, 1, HEAD_DIM, c), lambda g, n, i: (n, g, 0, 0)),
            pl.BlockSpec((n_sel, c), lambda g, n, i: (0, 0)),
            pl.BlockSpec((REP, strip.shape[1], tq), lambda g, n, i: (g, 0, 0)),
        ],
        out_specs=(pl.BlockSpec((GROUP_ROWS, tq), lambda g, n, i: (g, n * nq + i)),
                   pl.BlockSpec((1, 1, n_sel, tq), lambda g, n, i: (n, g, 0, i))),
        scratch_shapes=[pltpu.VMEM((n_sel, tq), F32)],
        compiler_params=_params("parallel", "parallel", "parallel"),
        name="nsa_cmp_topk",
    )(qT, ckr, cvTr, sel_map, strip)


SEL_ROWS = 16


def _flash_kernel(*refs, tq, n_tiles_max, n_delta, use_sel, normalize):
    nh = REP
    if use_sel:
        q_ref, k_ref, v_ref, bias_ref, sel_ref = refs[:5]
        outs = refs[5:]
    else:
        q_ref, k_ref, v_ref, bias_ref = refs[:4]
        sel_ref = None
        outs = refs[4:]
    if normalize:
        o_ref = outs[0]
        accs = outs[1:]
    else:
        o_ref, mo_ref, lo_ref = outs[:3]
        accs = outs[3:]
    accs, s_bufs = accs[:nh], accs[nh:]
    i = pl.program_id(2)
    n_tiles = jnp.minimum(i + 1, n_tiles_max)
    for acc in accs:
        acc[...] = jnp.zeros(acc.shape, F32)

    def scores(jt, s_buf):
        kt = jnp.maximum(i - jt, 0)
        kT = k_ref[0, 0, kt]
        tile = jnp.where(jt < n_tiles, jnp.minimum(jt, n_delta - 1), n_delta)
        cms = []
        for r in range(nh):
            q = q_ref[r * HEAD_DIM:(r + 1) * HEAD_DIM, :]
            if use_sel:
                q = jnp.concatenate([q, sel_ref[0, 0, kt]], axis=0)
            s = _tn_dot(kT, q) + bias_ref[r, tile]
            s_buf[r] = s
            cms.append(jnp.max(s, axis=0, keepdims=True))
        return tuple(cms)

    def accumulate(jt, s_buf, cms, ms, ls):
        v = v_ref[0, 0, jnp.maximum(i - jt, 0)]
        new_m, new_l = [], []
        for r in range(nh):
            m_new = jnp.maximum(ms[r], cms[r])
            alpha = jnp.exp2(ms[r] - m_new)
            p = jnp.exp2(s_buf[r] - m_new)
            new_l.append(alpha * ls[r] + jnp.sum(p, axis=0, keepdims=True))
            accs[r][...] = alpha * accs[r][...] + jnp.dot(v, p.astype(BF16), preferred_element_type=F32)
            new_m.append(m_new)
        return tuple(new_m), tuple(new_l)

    def pair(jj, carry):
        ms, ls, cm0 = carry
        cm1 = scores(2 * jj + 1, s_bufs[1])
        ms, ls = accumulate(2 * jj, s_bufs[0], cm0, ms, ls)
        cm0 = scores(2 * jj + 2, s_bufs[0])
        ms, ls = accumulate(2 * jj + 1, s_bufs[1], cm1, ms, ls)
        return ms, ls, cm0

    init = (tuple(jnp.full((1, tq), NEG, F32) for _ in range(nh)),
            tuple(jnp.zeros((1, tq), F32) for _ in range(nh)),
            scores(0, s_bufs[0]))
    ms, ls, _ = lax.fori_loop(0, (n_tiles + 1) // 2, pair, init)
    for r in range(nh):
        if normalize:
            o_ref[r * HEAD_DIM:(r + 1) * HEAD_DIM, :] = accs[r][...] * (1.0 / ls[r])
        else:
            o_ref[r * HEAD_DIM:(r + 1) * HEAD_DIM, :] = accs[r][...]
            mo_ref[0, r:r + 1, :] = ms[r]
            lo_ref[0, r:r + 1, :] = ls[r]


def flash_bias_tiles(rel_bias, t, n_delta, dil, window):
    j = np.arange(n_delta + 1)[:, None, None]
    sl = np.arange(t)[None, :, None]
    tl = np.arange(t)[None, None, :]
    d = j * t + tl - sl
    ok = (d >= 0) & (j < n_delta)
    if window is not None:
        ok = ok & (d <= window)
    tiles = bias_lookup((d * dil).reshape(-1, t), ok.reshape(-1, t), head_rows(rel_bias, t))
    return tiles.reshape(N_HEADS, n_delta + 1, t, t)


def flash_attention(qT, kT, vT, rel_bias, nseq, seqlen, sel=None, window=None, dil=1, normalize=True, tables=None):
    nkt, tk = kT.shape[2], kT.shape[4]
    tq = tk
    assert seqlen == nkt * tk
    nq = nkt
    if window is None:
        n_delta = -(-(BIAS_CONST_FROM + tk - 1) // tk) + 1
        n_tiles_max = nkt
    else:
        n_tiles_max = (window + tk - 1) // tk + 1
        n_delta = n_tiles_max
    tiles = _cached(tables, ("flash", tk, n_delta, dil, window),
                    lambda: flash_bias_tiles(rel_bias, tk, n_delta, dil, window))
    h_rows = qT.shape[0]
    use_sel = sel is not None
    in_specs = [
        pl.BlockSpec((GROUP_ROWS, tq), lambda g, n, i: (g, n * nq + i)),
        pl.BlockSpec((1, 1) + kT.shape[2:], lambda g, n, i: (n, g, 0, 0, 0)),
        pl.BlockSpec((1, 1) + vT.shape[2:], lambda g, n, i: (n, g, 0, 0, 0)),
        pl.BlockSpec((REP, n_delta + 1, tk, tq), lambda g, n, i: (g, 0, 0, 0)),
    ]
    args = [qT, kT, vT, tiles]
    if use_sel:
        in_specs.append(pl.BlockSpec((1, 1) + sel.shape[2:4] + (tq,), lambda g, n, i: (n, g, 0, 0, i)))
        args.append(sel)
    o_spec = pl.BlockSpec((GROUP_ROWS, tq), lambda g, n, i: (g, n * nq + i))
    o_shape = jax.ShapeDtypeStruct((h_rows, nseq * seqlen), F32)
    if normalize:
        out_shape, out_specs = o_shape, o_spec
    else:
        st_shape = jax.ShapeDtypeStruct((KV_HEADS, REP, nseq * seqlen), F32)
        st_spec = pl.BlockSpec((1, REP, tq), lambda g, n, i: (g, 0, n * nq + i))
        out_shape, out_specs = (o_shape, st_shape, st_shape), (o_spec, st_spec, st_spec)
    return pl.pallas_call(
        functools.partial(_flash_kernel, tq=tq, n_tiles_max=n_tiles_max, n_delta=n_delta,
                          use_sel=use_sel, normalize=normalize),
        out_shape=out_shape,
        grid=(KV_HEADS, nseq, nq),
        in_specs=in_specs,
        out_specs=out_specs,
        scratch_shapes=([pltpu.VMEM((HEAD_DIM, tq), F32) for _ in range(REP)]
                        + [pltpu.VMEM((REP, tk, tq), F32) for _ in range(2)]),
        compiler_params=_params("parallel", "parallel", "parallel"),
        name="flash_sel" if use_sel else ("flash_band" if window is not None else "flash_causal"),
    )(*args)


def _nsa_combine_kernel(oc_ref, os_ref, ow_ref, g_ref, o_ref):
    for h in range(N_HEADS):
        rows = slice(h * HEAD_DIM, (h + 1) * HEAD_DIM)
        o = (g_ref[h:h + 1, :] * oc_ref[rows, :]
             + g_ref[N_HEADS + h:N_HEADS + h + 1, :] * os_ref[rows, :]
             + g_ref[2 * N_HEADS + h:2 * N_HEADS + h + 1, :] * ow_ref[rows, :])
        o_ref[rows, :] = o.astype(BF16)


def nsa_combine(ocT, osT, owT, gatesT, tm=512):
    d, t = ocT.shape
    tm = _token_tile(t, tm)
    big = pl.BlockSpec((d, tm), lambda i: (0, i))
    return pl.pallas_call(
        _nsa_combine_kernel,
        out_shape=jax.ShapeDtypeStruct((d, t), BF16),
        grid=(t // tm,),
        in_specs=[big, big, big, pl.BlockSpec((N_BRANCHES * N_HEADS, tm), lambda i: (0, i))],
        out_specs=big,
        compiler_params=_params("parallel"),
        name="nsa_combine",
    )(ocT, osT, owT, gatesT)


def _dil_merge_kernel(acc_ref, m_ref, l_ref, o_ref):
    ng = acc_ref.shape[0]
    mx = functools.reduce(jnp.maximum, [m_ref[g] for g in range(ng)])
    w = [jnp.exp2(m_ref[g] - mx) for g in range(ng)]
    den = functools.reduce(jnp.add, [w[g] * l_ref[g] for g in range(ng)])
    inv = 1.0 / den
    for h in range(N_HEADS):
        rows = slice(h * HEAD_DIM, (h + 1) * HEAD_DIM)
        num = functools.reduce(jnp.add, [w[g][h:h + 1, :] * acc_ref[g, rows, :] for g in range(ng)])
        o_ref[rows, :] = (num * inv[h:h + 1, :]).astype(BF16)


def dil_merge(acc, m, l, tm=512):
    ng, d, t = acc.shape
    tm = _token_tile(t, tm)
    st = pl.BlockSpec((ng, N_HEADS, tm), lambda i: (0, 0, i))
    return pl.pallas_call(
        _dil_merge_kernel,
        out_shape=jax.ShapeDtypeStruct((d, t), BF16),
        grid=(t // tm,),
        in_specs=[pl.BlockSpec((ng, d, tm), lambda i: (0, 0, i)), st, st],
        out_specs=pl.BlockSpec((d, tm), lambda i: (0, i)),
        compiler_params=_params("parallel"),
        name="dil_merge",
    )(acc, m, l)


NSA_SEGS = (
    (0, 512, "norm", 0, ATTN_SCALE), (512, 512, "norm", 0, ATTN_SCALE),
    (1024, 512, "plain", 0, 1.0),
    (1536, 256, "norm", 1, 1.0),
    (1792, 256, "plain", 0, 1.0),
    (2048, 256, "norm", 2, 1.0),
    (2304, 256, "plain", 0, 1.0),
    (2560, N_BRANCHES * N_HEADS, "sigmoid", 0, 1.0),
)
KV_SEGS = ((0, 256, "norm", 0, 1.0), (256, 256, "plain", 0, 1.0))
DIL_Q_SEGS = tuple((512 * j, 512, "norm", j // 2, ATTN_SCALE) for j in range(6))


def kv_tiles(xT, nseq, seqlen, t, block_rows=False):
    nt = seqlen // t
    x = xT.reshape(KV_HEADS, HEAD_DIM, nseq, nt, t).transpose(2, 0, 3, 1, 4).astype(BF16)
    if block_rows:
        ind = (np.arange(t)[None, :] // SEL_BLOCK == np.arange(SEL_ROWS)[:, None]).astype(np.float32)
        ind = jnp.broadcast_to(jnp.asarray(ind, dtype=BF16), x.shape[:3] + (SEL_ROWS, t))
        x = jnp.concatenate([x, ind], axis=3)
    return x


def nsa_attention_prompt(zT, b, s, rel_bias, cw, k_gain_cmp, t=256, tables=None):
    qT = (zT[:1024] * LOG2E).astype(BF16)
    tok = zT[1024:2560].T
    rows = tok[:, :1024].reshape(b, s, 4, KV_HEADS, HEAD_DIM)
    win = tok[:, 1024:].reshape(b, s, 2, KV_HEADS, HEAD_DIM)
    gatesT = zT[2560:2560 + N_BRANCHES * N_HEADS]
    c = s // CMP_STRIDE
    ch = rows[:, :, 0:2].transpose(2, 0, 3, 1, 4).reshape(2, b * KV_HEADS, c, CMP_STRIDE * HEAD_DIM).astype(BF16)
    cmp = compress(ch, *cw, k_gain_cmp, bn=4)
    cmp = jnp.flip(cmp, axis=2).reshape(2, b, KV_HEADS, c, HEAD_DIM)
    ckr = cmp[0].astype(BF16)
    cvTr = cmp[1].transpose(0, 1, 3, 2).astype(BF16)
    ocT, sel = cmp_topk(qT, ckr, cvTr, rel_bias, b, s, tables=tables)
    blocks = t // SEL_BLOCK
    mask = jnp.where(sel > 0.0, 0.0, NEG).reshape(b, KV_HEADS, s // t, blocks, s)
    mask = jnp.pad(mask, ((0, 0), (0, 0), (0, 0), (0, SEL_ROWS - blocks), (0, 0))).astype(BF16)
    osT = flash_attention(qT, kv_tiles(zT[1536:1792], b, s, t, block_rows=True), kv_tiles(zT[1792:2048], b, s, t),
                          rel_bias, b, s, sel=mask, tables=tables)
    owT = flash_attention(qT, kv_tiles(zT[2048:2304], b, s, t), kv_tiles(zT[2304:2560], b, s, t),
                          rel_bias, b, s, window=NSA_WINDOW, tables=tables)
    oT = nsa_combine(ocT, osT, owT, gatesT)
    return oT, rows, win


def dil_attention_prompt(qallT, kvT, b, s, rel_bias, t=128, tables=None):
    accs, ms, ls = [], [], []
    for gi, (window, dil) in enumerate(DIL_GROUPS):
        length = s // dil

        def split(x):
            r = x.shape[0]
            return x.reshape(r, b, length, dil).transpose(0, 1, 3, 2).reshape(r, b * s)

        def merge(x):
            r = x.shape[0]
            return x.reshape(r, b, dil, length).transpose(0, 1, 3, 2).reshape(r, b * s)

        qg = split((qallT[1024 * gi:1024 * (gi + 1)] * LOG2E).astype(BF16))
        kvg = split(kvT.astype(BF16))
        acc, m, l = flash_attention(qg, kv_tiles(kvg[:KV_WIDTH], b * dil, length, t),
                                    kv_tiles(kvg[KV_WIDTH:], b * dil, length, t), rel_bias, b * dil, length,
                                    window=window // dil, dil=dil, normalize=False, tables=tables)

        accs.append(merge(acc))
        ms.append(merge(m.reshape(N_HEADS, b * s)))
        ls.append(merge(l.reshape(N_HEADS, b * s)))
    return dil_merge(jnp.stack(accs), jnp.stack(ms), jnp.stack(ls))


NEW_ROWS = 16


def _group_diagonal(oT, lanes_per_group):
    grp = lax.broadcasted_iota(jnp.int32, (HEAD_DIM, oT.shape[1]), 1) // lanes_per_group
    out = jnp.zeros((HEAD_DIM, oT.shape[1]), F32)
    for g in range(KV_HEADS):
        out = jnp.where(grp == g, oT[g * HEAD_DIM:(g + 1) * HEAD_DIM, :], out)
    return out


def _nsa_sample_kernel(pt_ref, *refs, n_pages, page, n_top, past_len, dt):
    del pt_ref
    pages = refs[:n_pages]
    (win_ref, new_ref, q_ref, gate_ref, w1l_ref, w1f_ref, pe_ref, w2_ref, kn_ref, map_ref, gs_ref,
     bc_ref, bs_ref, bw_ref, o_ref, x_s, s_s, v_s) = refs[n_pages:]
    q = q_ref[0]
    lanes = q.shape[1]
    n_ch = past_len // CMP_STRIDE

    pairs = KV_WIDTH // 128
    for p in range(n_pages):
        for jj in range(2 * pairs):
            x_s[jj, p * page:(p + 1) * page, :] = pages[p][0, 128 * jj:128 * (jj + 1), :].T
    comp = []
    for slot in range(2):
        ab = [None] * KV_HEADS
        for l in range(CMP_STRIDE):
            for hp in range(pairs):
                xl = x_s.at[slot * pairs + hp][pl.ds(l, n_ch, stride=CMP_STRIDE), :].astype(BF16)
                for gg in range(2):
                    g = 2 * hp + gg
                    d = jnp.dot(xl[:, gg * HEAD_DIM:(gg + 1) * HEAD_DIM], w1l_ref[slot, l],
                                preferred_element_type=F32)
                    ab[g] = d if ab[g] is None else ab[g] + d
        w1f = w1f_ref[slot]
        pe = (jnp.dot(pe_ref[slot, 0], w1f[:, :CMP_HIDDEN], preferred_element_type=F32)
              + jnp.dot(pe_ref[slot, 1], w1f[:, CMP_HIDDEN:], preferred_element_type=F32))[0:1]
        hid = jnp.concatenate(
            [a[:, :CMP_HIDDEN] + pltpu.roll(a[:, CMP_HIDDEN:], n_ch - 1, 0) + pe for a in ab], axis=1)
        act = hid / (1.0 + jnp.exp(-hid))
        comp.append(jnp.dot(act.astype(BF16), w2_ref[slot], preferred_element_type=F32))
    ck = comp[0]
    lane_g = lax.broadcasted_iota(jnp.int32, ck.shape, 1) // HEAD_DIM
    sq = ck * ck
    scale = jnp.zeros_like(ck)
    for g in range(KV_HEADS):
        ms = jnp.sum(jnp.where(lane_g == g, sq, 0.0), axis=1, keepdims=True) * (1.0 / HEAD_DIM)
        scale = jnp.where(lane_g == g, lax.rsqrt(ms + EPS), scale)
    ck = (ck * scale * kn_ref[...]).astype(BF16)
    cv = comp[1].astype(BF16)

    sc = jnp.dot(ck, q, preferred_element_type=F32) + bc_ref[...]
    m = jnp.max(sc, axis=0, keepdims=True)
    m = jnp.where(m < 0.5 * NEG, 0.0, m)
    p = jnp.exp(sc - m)
    p = p * (1.0 / jnp.maximum(jnp.sum(p, axis=0, keepdims=True), 1e-30))
    ocT = _tn_dot(cv, p.astype(BF16))
    hi = p.astype(BF16)
    lo = (p - hi.astype(F32)).astype(BF16)
    m1 = (jnp.dot(map_ref[...], hi, preferred_element_type=F32)
          + jnp.dot(map_ref[...], lo, preferred_element_type=F32))
    hi = m1.astype(BF16)
    lo = (m1 - hi.astype(F32)).astype(BF16)
    imp = (jnp.dot(hi, gs_ref[...], preferred_element_type=F32)
           + jnp.dot(lo, gs_ref[...], preferred_element_type=F32))
    n_sel_pad = imp.shape[0]
    n_sel = -(-(past_len + dt) // SEL_BLOCK)
    j = lax.broadcasted_iota(jnp.int32, (n_sel_pad, lanes), 0)
    qpos = past_len + lax.broadcasted_iota(jnp.int32, (n_sel_pad, lanes), 1) % dt
    cur = qpos // SEL_BLOCK
    forced = (j == 0) | (j == cur) | (j == cur - 1)
    future = (j * SEL_BLOCK > qpos) | (j >= n_sel)
    v = jnp.where(forced, jnp.inf, jnp.where(future, -jnp.inf, imp))
    v_s[...] = v

    def count(a, cnt):
        va = v_s[pl.ds(a, 1), :]
        ahead = (va > v) | ((va == v) & (j > a))
        return cnt + jnp.where(ahead, 1.0, 0.0)

    rank = lax.fori_loop(0, n_sel, count, jnp.zeros((n_sel_pad, lanes), F32))
    sel = jnp.where((rank < n_top) & (j < n_sel), 0.0, NEG)

    blocks_per_page = page // SEL_BLOCK
    m_run = jnp.full((page, lanes), NEG, F32)
    for pg in range(n_pages):
        kT = pages[pg][0, 2 * KV_WIDTH:3 * KV_WIDTH, :].astype(BF16)
        mask = jnp.concatenate(
            [jnp.broadcast_to(sel[blocks_per_page * pg + bb:blocks_per_page * pg + bb + 1], (SEL_BLOCK, lanes))
             for bb in range(blocks_per_page)], axis=0)
        s = _tn_dot(kT, q) + bs_ref[pg * page:(pg + 1) * page, :] + mask
        s_s[pg * page:(pg + 1) * page, :] = s
        m_run = jnp.maximum(m_run, s)
    nb_past = past_len // SEL_BLOCK
    s_new = (jnp.dot(new_ref[0, 0], q, preferred_element_type=F32) + bs_ref[past_len:past_len + NEW_ROWS, :]
             + sel[nb_past:nb_past + 1])
    m = jnp.maximum(jnp.max(m_run, axis=0, keepdims=True), jnp.max(s_new, axis=0, keepdims=True))
    pn = jnp.exp(s_new - m)
    den = jnp.sum(pn, axis=0, keepdims=True)
    acc = _tn_dot(new_ref[0, 1], pn.astype(BF16))
    for pg in range(n_pages):
        pp = jnp.exp(s_s[pg * page:(pg + 1) * page, :] - m)
        den = den + jnp.sum(pp, axis=0, keepdims=True)
        acc = acc + jnp.dot(pages[pg][0, 3 * KV_WIDTH:4 * KV_WIDTH, :].astype(BF16), pp.astype(BF16),
                            preferred_element_type=F32)
    osT = acc * (1.0 / den)

    wb = win_ref.shape[2]
    s_w = _tn_dot(win_ref[0, 0:KV_WIDTH, :].astype(BF16), q) + bw_ref[0:wb, :]
    s_n = jnp.dot(new_ref[0, 2], q, preferred_element_type=F32) + bw_ref[wb:wb + NEW_ROWS, :]
    m = jnp.maximum(jnp.max(s_w, axis=0, keepdims=True), jnp.max(s_n, axis=0, keepdims=True))
    pw = jnp.exp(s_w - m)
    pn = jnp.exp(s_n - m)
    den = jnp.sum(pw, axis=0, keepdims=True) + jnp.sum(pn, axis=0, keepdims=True)
    acc = (jnp.dot(win_ref[0, KV_WIDTH:2 * KV_WIDTH, :].astype(BF16), pw.astype(BF16), preferred_element_type=F32)
           + _tn_dot(new_ref[0, 3], pn.astype(BF16)))
    owT = acc * (1.0 / den)

    lpg = lanes // KV_HEADS
    o_ref[0] = (gate_ref[0, 0:1, :] * _group_diagonal(ocT, lpg)
                + gate_ref[0, 1:2, :] * _group_diagonal(osT, lpg)
                + gate_ref[0, 2:3, :] * _group_diagonal(owT, lpg))


def _lane_table(rel_bias, dist, valid, dt):
    lanes = np.arange(N_HEADS * dt)
    h, t = lanes // dt, lanes % dt
    return bias_lookup(dist[:, t], valid[:, t], rel_bias[:, jnp.asarray(h)][None])[0]


def nsa_sample_tables(rel_bias, past_len, wb, dt):
    t = np.arange(dt)[None, :]
    n_ch = past_len // CMP_STRIDE
    c = np.arange(n_ch)[:, None]
    dist = past_len + t - (CMP_STRIDE * c + CMP_LEN - 1)
    bc = _lane_table(rel_bias, dist, (c < n_ch - 1) & (dist >= 0), dt)
    s = np.arange(past_len)[:, None]
    n = np.arange(NEW_ROWS)[:, None]
    dist = np.concatenate([past_len + t - s, t - n])
    valid = np.concatenate([np.ones((past_len, dt), bool), (n < dt) & (t - n >= 0)])
    bs = _lane_table(rel_bias, dist, valid, dt)
    l = np.arange(wb)[:, None]
    dist = np.concatenate([wb + t - l, t - n])
    valid = np.concatenate([wb + t - l <= NSA_WINDOW, (n < dt) & (t - n >= 0)])
    bw = _lane_table(rel_bias, dist, valid, dt)
    return bc, bs, bw


def nsa_sample_constants(past_len, dt):
    n_ch = past_len // CMP_STRIDE
    n_sel = -(-(past_len + dt) // SEL_BLOCK)
    n_sel_pad = -(-n_sel // 8) * 8
    c = np.arange(n_ch)[None, :]
    j0 = np.arange(n_sel_pad)[:, None] * SEL_BLOCK
    c_start, c_end = CMP_STRIDE * c, CMP_STRIDE * c + CMP_LEN - 1
    sel_map = (c_start <= j0 + SEL_BLOCK - 1) & (c_end >= j0) & (c < n_ch - 1) & (j0 < n_sel * SEL_BLOCK)
    lanes = np.arange(N_HEADS * dt)
    grp, t = lanes // (REP * dt), lanes % dt
    gsum = (grp[:, None] == grp[None, :]) & (t[:, None] == t[None, :])
    return (jnp.asarray(sel_map.astype(np.float32), dtype=BF16), jnp.asarray(gsum.astype(np.float32), dtype=BF16))


def nsa_sample_attention(cache_pages, page_idx, win_cache, win_base, new_rows, qbd, gates, cw, k_gain_cmp, tables):
    db, n_pages = page_idx.shape
    page = cache_pages.shape[2]
    past_len = n_pages * page
    lanes = qbd.shape[2]
    dt = lanes // N_HEADS
    wb = win_cache.shape[2]
    w1ab, pe, w2 = cw
    w1l = w1ab.reshape(2, CMP_STRIDE, HEAD_DIM, 2 * CMP_HIDDEN)
    w2bd = jnp.einsum('shd,gk->sghkd', w2, jnp.eye(KV_HEADS, dtype=w2.dtype)).reshape(
        2, KV_HEADS * CMP_HIDDEN, KV_WIDTH)
    kn = jnp.tile(k_gain_cmp.reshape(1, HEAD_DIM), (1, KV_HEADS))
    sel_map, gsum = nsa_sample_constants(past_len, dt)
    bc, bs, bw = tables
    n_sel = -(-(past_len + dt) // SEL_BLOCK)
    assert past_len % SEL_BLOCK == 0 and dt <= SEL_BLOCK and page % SEL_BLOCK == 0

    def const(a):
        nd = a.ndim
        return pl.BlockSpec(a.shape, lambda i, pt: (0,) * nd)

    page_specs = [pl.BlockSpec((1, cache_pages.shape[1], page), functools.partial(lambda p, i, pt: (pt[i, p], 0, 0), p))
                  for p in range(n_pages)]
    consts = [w1l, w1ab, pe, w2bd, kn, sel_map, gsum, bc, bs, bw]
    grid_spec = pltpu.PrefetchScalarGridSpec(
        num_scalar_prefetch=1,
        grid=(db,),
        in_specs=page_specs + [
            pl.BlockSpec((1, win_cache.shape[1], wb), lambda i, pt: (win_base + i, 0, 0)),
            pl.BlockSpec((1,) + new_rows.shape[1:], lambda i, pt: (i, 0, 0, 0)),
            pl.BlockSpec((1,) + qbd.shape[1:], lambda i, pt: (i, 0, 0)),
            pl.BlockSpec((1,) + gates.shape[1:], lambda i, pt: (i, 0, 0)),
        ] + [const(a) for a in consts],
        out_specs=pl.BlockSpec((1, HEAD_DIM, lanes), lambda i, pt: (i, 0, 0)),
        scratch_shapes=[pltpu.VMEM((2 * KV_WIDTH // 128, past_len, 128), F32),
                        pltpu.VMEM((past_len, lanes), F32),
                        pltpu.VMEM((sel_map.shape[0], lanes), F32)],
    )
    return pl.pallas_call(
        functools.partial(_nsa_sample_kernel, n_pages=n_pages, page=page, n_top=min(SEL_TOP_N, n_sel),
                          past_len=past_len, dt=dt),
        out_shape=jax.ShapeDtypeStruct((db, HEAD_DIM, lanes), F32),
        grid_spec=grid_spec,
        compiler_params=_params("parallel"),
        name="nsa_sample",
    )(page_idx, *([cache_pages] * n_pages), win_cache, new_rows, qbd, gates, *consts)


def _dil_sample_kernel(kv_ref, new_ref, q_ref, b0_ref, b1_ref, b2_ref, o_ref, s_s, *, los, buf_len):
    lanes = q_ref.shape[3]
    chunk = 128
    ms, dens, accs = [], [], []
    for gi, (bias_ref, lo) in enumerate(zip((b0_ref, b1_ref, b2_ref), los)):
        q = q_ref[0, gi]
        n_chunks = (buf_len - lo) // chunk
        m_run = jnp.full((chunk, lanes), NEG, F32)
        for c in range(n_chunks):
            r0 = lo + c * chunk
            s = (_tn_dot(kv_ref[0, 0:KV_WIDTH, r0:r0 + chunk].astype(BF16), q)
                 + bias_ref[c * chunk:(c + 1) * chunk, :])
            s_s[c * chunk:(c + 1) * chunk, :] = s
            m_run = jnp.maximum(m_run, s)
        s_new = (jnp.dot(new_ref[0, 0], q, preferred_element_type=F32)
                 + bias_ref[n_chunks * chunk:n_chunks * chunk + NEW_ROWS, :])
        m = jnp.maximum(jnp.max(m_run, axis=0, keepdims=True), jnp.max(s_new, axis=0, keepdims=True))
        pn = jnp.exp(s_new - m)
        den = jnp.sum(pn, axis=0, keepdims=True)
        acc = _tn_dot(new_ref[0, 1], pn.astype(BF16))
        for c in range(n_chunks):
            r0 = lo + c * chunk
            pp = jnp.exp(s_s[c * chunk:(c + 1) * chunk, :] - m)
            den = den + jnp.sum(pp, axis=0, keepdims=True)
            acc = acc + jnp.dot(kv_ref[0, KV_WIDTH:2 * KV_WIDTH, r0:r0 + chunk].astype(BF16), pp.astype(BF16),
                                preferred_element_type=F32)
        ms.append(m)
        dens.append(den)
        accs.append(acc)
    mx = functools.reduce(jnp.maximum, ms)
    w = [jnp.exp(m - mx) for m in ms]
    num = functools.reduce(jnp.add, [wg * a for wg, a in zip(w, accs)])
    den = functools.reduce(jnp.add, [wg * d for wg, d in zip(w, dens)])
    o_ref[0] = _group_diagonal(num * (1.0 / den), lanes // KV_HEADS)


def dil_sample_tables(rel_bias, buf_len, dt):
    t = np.arange(dt)[None, :]
    n = np.arange(NEW_ROWS)[:, None]
    tables, los = [], []
    for window, dil in DIL_GROUPS:
        lo = max(0, buf_len - window) // 128 * 128
        s = np.arange(lo, buf_len)[:, None]
        dist = np.concatenate([buf_len + t - s, t - n])
        valid = np.concatenate([np.ones((buf_len - lo, dt), bool), (n < dt) & (t - n >= 0)])
        valid = valid & (dist % dil == 0) & (dist <= window)
        tables.append(_lane_table(rel_bias, dist, valid, dt))
        los.append(lo)
    return tables, tuple(los)


def dil_sample_attention(kv_cache, new_rows, qbd3, tables, los):
    db, width, buf_len = kv_cache.shape
    lanes = qbd3.shape[3]
    assert buf_len % 128 == 0
    tspec = [pl.BlockSpec(tb.shape, lambda i: (0, 0)) for tb in tables]
    return pl.pallas_call(
        functools.partial(_dil_sample_kernel, los=los, buf_len=buf_len),
        out_shape=jax.ShapeDtypeStruct((db, HEAD_DIM, lanes), F32),
        grid=(db,),
        in_specs=[
            pl.BlockSpec((1, width, buf_len), lambda i: (i, 0, 0)),
            pl.BlockSpec((1,) + new_rows.shape[1:], lambda i: (i, 0, 0, 0)),
            pl.BlockSpec((1,) + qbd3.shape[1:], lambda i: (i, 0, 0, 0)),
        ] + tspec,
        out_specs=pl.BlockSpec((1, HEAD_DIM, lanes), lambda i: (i, 0, 0)),
        scratch_shapes=[pltpu.VMEM((buf_len, lanes), F32)],
        compiler_params=_params("parallel"),
        name="dil_sample",
    )(kv_cache, new_rows, qbd3, *tables)


def lane_queries(qT, db, dt):
    q = qT.reshape(KV_HEADS, REP, HEAD_DIM, db, dt).transpose(3, 0, 2, 1, 4).reshape(db, KV_HEADS, HEAD_DIM, REP * dt)
    eye = jnp.eye(KV_HEADS, dtype=q.dtype)
    qbd = q[:, :, :, None, :] * eye[None, :, None, :, None]
    return qbd.reshape(db, KV_WIDTH, KV_HEADS * REP * dt).astype(BF16)


def lanes_to_features(o, db, dt):
    o = o.reshape(db, HEAD_DIM, KV_HEADS, REP, dt).transpose(2, 3, 1, 0, 4)
    return o.reshape(N_HEADS * HEAD_DIM, db * dt)


def new_token_rows(zT_rows, db, dt):
    n = zT_rows.shape[0] // KV_WIDTH
    r = zT_rows.reshape(n, KV_WIDTH, db, dt).transpose(2, 0, 3, 1)
    return jnp.pad(r, ((0, 0), (0, 0), (0, NEW_ROWS - dt), (0, 0))).astype(BF16)


def kernel(x_prompt, x_sample, cache_nsa_kv, cache_win_kv, cache_dil_kv, page_table, rel_bias, a_attn_norm, a_w_in, a_q_norm, a_k_norm, a_cmp_pe, a_cmp_w1, a_cmp_w2, a_w_out, kv_norm, w_kv_shared, k_norm_shared, b_attn_norm, b_w_q, b_q_norm, b_w_out, mlp_norm, mlp_w1, mlp_w2):
    b, s, d = x_prompt.shape
    db, dt, _ = x_sample.shape
    n_a = a_w_in.shape[0]
    n_b = b_w_q.shape[0]
    n_pool, page = cache_nsa_kv.shape[1:3]
    wb = cache_win_kv.shape[2]
    buf_len = cache_dil_kv.shape[1]
    past_len = page_table.shape[1] * page
    assert N_HEADS * dt == 128
    tables = {}
    xpT = x_prompt.reshape(b * s, d).T
    xsT = x_sample.reshape(db * dt, d).T
    cache_pages = cache_nsa_kv.transpose(0, 1, 3, 4, 5, 2).reshape(n_a * n_pool, -1, page)
    win_cache = cache_win_kv.transpose(0, 1, 3, 4, 5, 2).reshape(n_a * db, -1, wb)
    nsa_tables = nsa_sample_tables(rel_bias, past_len, wb, dt)
    rows_p, rows_s, wins_p, wins_s = [], [], [], []
    for l in range(n_a + n_b):
        w1T = mlp_w1[l].T.astype(BF16)
        w2T = mlp_w2[l].T.astype(BF16)
        if l < n_a:
            w_inT = a_w_in[l].T.astype(BF16)
            norms = jnp.stack([a_q_norm[l], a_k_norm[l][1], a_k_norm[l][2]])[..., None]
            cw = compress_weights(a_cmp_pe[l], a_cmp_w1[l], a_cmp_w2[l])
            woT = a_w_out[l].T.astype(BF16)
            zT = norm_proj(xpT, a_attn_norm[l], w_inT, norms, NSA_SEGS)
            oT, rows, win = nsa_attention_prompt(zT, b, s, rel_bias, cw, a_k_norm[l][0], tables=tables)
            rows_p.append(rows)
            wins_p.append(win[:, -min(NSA_WINDOW, s):])
            xpT = outproj_mlp(xpT, oT, woT, mlp_norm[l], w1T, w2T)
            zs = norm_proj(xsT, a_attn_norm[l], w_inT, norms, NSA_SEGS)
            gates = zs[2560:2560 + N_BRANCHES * N_HEADS].reshape(N_BRANCHES, KV_HEADS, REP, db, dt)
            gates = gates.transpose(3, 0, 1, 2, 4).reshape(db, N_BRANCHES, N_HEADS * dt)
            o = nsa_sample_attention(
                cache_pages, page_table + l * n_pool, win_cache, l * db, new_token_rows(zs[1536:2560], db, dt),
                lane_queries(zs[:1024], db, dt), gates, cw, a_k_norm[l][0], nsa_tables)
            kv6 = zs[1024:2560].T.reshape(db, dt, 6, KV_HEADS, HEAD_DIM)
            rows_s.append(kv6[:, :, 0:4])
            wins_s.append(jnp.concatenate([cache_win_kv[l], kv6[:, :, 4:6]], axis=1)[:, -min(NSA_WINDOW, wb + dt):])
            xsT = outproj_mlp(xsT, lanes_to_features(o, db, dt).astype(BF16), woT, mlp_norm[l], w1T, w2T)
        else:
            i = l - n_a
            if i == 0:
                w_kvT = w_kv_shared.T.astype(BF16)
                kn = k_norm_shared.reshape(1, HEAD_DIM, 1)
                kvpT = norm_proj(xpT, kv_norm, w_kvT, kn, KV_SEGS)
                kvp = kvpT.T.reshape(b, s, 2, KV_HEADS, HEAD_DIM)
                kvsT = norm_proj(xsT, kv_norm, w_kvT, kn, KV_SEGS)
                kvs = kvsT.T.reshape(db, dt, 2, KV_HEADS, HEAD_DIM)
                dmax = max(w for w, _ in DIL_GROUPS)
                new_dil_p = kvp[:, -min(dmax, s):]
                new_dil_s = jnp.concatenate([cache_dil_kv, kvs], axis=1)[:, -min(dmax, buf_len + dt):]
                dil_cache = cache_dil_kv.transpose(0, 2, 3, 4, 1).reshape(db, -1, buf_len)
                dil_new = new_token_rows(kvsT, db, dt)
                dil_tables, dil_los = dil_sample_tables(rel_bias, buf_len, dt)
            w_qT = b_w_q[i].T.astype(BF16)
            qn = b_q_norm[i][..., None]
            woT = b_w_out[i].T.astype(BF16)
            qT = norm_proj(xpT, b_attn_norm[i], w_qT, qn, DIL_Q_SEGS)
            oT = dil_attention_prompt(qT, kvpT, b, s, rel_bias, tables=tables)
            xpT = outproj_mlp(xpT, oT, woT, mlp_norm[l], w1T, w2T)
            qs = norm_proj(xsT, b_attn_norm[i], w_qT, qn, DIL_Q_SEGS)
            qbd3 = jnp.stack([lane_queries(qs[1024 * gi:1024 * (gi + 1)], db, dt) for gi in range(len(DIL_GROUPS))], axis=1)
            o = dil_sample_attention(dil_cache, dil_new, qbd3, dil_tables, dil_los)
            xsT = outproj_mlp(xsT, lanes_to_features(o, db, dt).astype(BF16), woT, mlp_norm[l], w1T, w2T)
    return (xpT.T.reshape(b, s, d), xsT.T.reshape(db, dt, d),
            jnp.stack(rows_p), jnp.stack(rows_s), jnp.stack(wins_p), jnp.stack(wins_s),
            new_dil_p, new_dil_s)
```

```python
import functools
import math

import numpy as np
import jax
import jax.numpy as jnp
from jax import lax
from jax.experimental import pallas as pl
from jax.experimental.pallas import tpu as pltpu

F32 = jnp.float32
BF16 = jnp.bfloat16

D_MODEL = 1024
N_HEADS = 16
HEAD_DIM = 64
KV_HEADS = 4
REP = N_HEADS // KV_HEADS
GROUP_ROWS = REP * HEAD_DIM
KV_WIDTH = KV_HEADS * HEAD_DIM
D_FF = 4 * D_MODEL
EPS = 1e-6
ATTN_SCALE = HEAD_DIM ** -0.5
N_BUCKETS = 32
MAX_DISTANCE = 2048
CMP_LEN = 32
CMP_STRIDE = 16
CMP_HIDDEN = 2 * HEAD_DIM
SEL_BLOCK = 64
SEL_TOP_N = 16
NSA_WINDOW = 512
N_BRANCHES = 3
NSA_IN = N_HEADS * HEAD_DIM + 6 * KV_HEADS * HEAD_DIM + N_BRANCHES * N_HEADS
DIL_GROUPS = ((128, 1), (512, 4), (2048, 16))
PAGE_SIZE = 128

NEG = -1e30
LOG2E = math.log2(math.e)
VMEM_LIMIT = 56 * 1024 * 1024
BIAS_TABLE_LEN = 2048


def _bucket_of_distance(d):
    max_exact = N_BUCKETS // 2
    d = np.maximum(d, 0)
    ratio = np.log(np.maximum(d, 1).astype(np.float32) / np.float32(max_exact)) / np.float32(
        math.log(MAX_DISTANCE / max_exact))
    large = max_exact + (ratio * np.float32(N_BUCKETS - max_exact)).astype(np.int32)
    return np.where(d < max_exact, d, np.minimum(large, N_BUCKETS - 1)).astype(np.int32)


_BUCKETS = _bucket_of_distance(np.arange(BIAS_TABLE_LEN))
BIAS_CONST_FROM = int(np.argmax(_BUCKETS == N_BUCKETS - 1))
assert np.all(_BUCKETS[BIAS_CONST_FROM:] == N_BUCKETS - 1)


def _params(*sem):
    return pltpu.CompilerParams(dimension_semantics=sem, vmem_limit_bytes=VMEM_LIMIT)


def _token_tile(t, tm):
    tm = min(tm, t)
    assert t % tm == 0, (t, tm)
    return tm


def _tn_dot(a, b):
    return lax.dot_general(a, b, (((0,), (0,)), ((), ())), preferred_element_type=F32)


def _bias_lookup_kernel(bkt_ref, tab_ref, o_ref):
    bkt = bkt_ref[...]
    v = jnp.full(bkt.shape, NEG, F32)
    for bb in range(N_BUCKETS):
        v = jnp.where(bkt == bb, tab_ref[0, bb:bb + 1, :], v)
    o_ref[0] = v


def bias_lookup(dist, valid, tab):
    rows, lanes = dist.shape
    bkt = np.where(valid, _BUCKETS[np.clip(dist, 0, BIAS_TABLE_LEN - 1)], N_BUCKETS).astype(np.int32)
    rb = next(r for r in range(min(rows, 512) // 8 * 8, 0, -8) if rows % r == 0)
    nh = tab.shape[0]
    return pl.pallas_call(
        _bias_lookup_kernel,
        out_shape=jax.ShapeDtypeStruct((nh, rows, lanes), F32),
        grid=(rows // rb, nh),
        in_specs=[pl.BlockSpec((rb, lanes), lambda r, h: (r, 0)),
                  pl.BlockSpec((1, N_BUCKETS, lanes), lambda r, h: (h, 0, 0))],
        out_specs=pl.BlockSpec((1, rb, lanes), lambda r, h: (h, r, 0)),
        compiler_params=_params("parallel", "parallel"),
        name="bias_lookup",
    )(jnp.asarray(bkt), tab)


def _cached(tables, key, build):
    if tables is None:
        return build()
    if key not in tables:
        tables[key] = build()
    return tables[key]


def head_rows(rel_bias, lanes):
    return jnp.broadcast_to((rel_bias.T * LOG2E)[:, :, None], (N_HEADS, N_BUCKETS, lanes))


def _proj_kernel(x_ref, g_ref, w_ref, n_ref, o_ref, *, segs):
    x = x_ref[...]
    ms = jnp.mean(x * x, axis=0, keepdims=True)
    xn = (x * lax.rsqrt(ms + EPS) * g_ref[...]).astype(BF16)
    tm = x.shape[1]
    for r0, nr, kind, ni, scale in segs:
        z = jnp.dot(w_ref[r0:r0 + nr, :], xn, preferred_element_type=F32)
        if kind == "norm":
            z3 = z.reshape(nr // HEAD_DIM, HEAD_DIM, tm)
            hs = jnp.mean(z3 * z3, axis=1, keepdims=True)
            z = (z3 * lax.rsqrt(hs + EPS) * n_ref[ni][None] * scale).reshape(nr, tm)
        elif kind == "sigmoid":
            z = 1.0 / (1.0 + jnp.exp(-z))
        o_ref[r0:r0 + nr, :] = z


def norm_proj(xT, gain, wT, norms, segs, tm=512):
    d, t = xT.shape
    tm = _token_tile(t, tm)
    n = wT.shape[0]
    return pl.pallas_call(
        functools.partial(_proj_kernel, segs=tuple(segs)),
        out_shape=jax.ShapeDtypeStruct((n, t), F32),
        grid=(t // tm,),
        in_specs=[
            pl.BlockSpec((d, tm), lambda i: (0, i)),
            pl.BlockSpec((d, 1), lambda i: (0, 0)),
            pl.BlockSpec((n, d), lambda i: (0, 0)),
            pl.BlockSpec(norms.shape, lambda i: (0, 0, 0)),
        ],
        out_specs=pl.BlockSpec((n, tm), lambda i: (0, i)),
        compiler_params=_params("parallel"),
        name="norm_proj",
    )(xT, gain.reshape(d, 1), wT, norms)


def _outmlp_kernel(x_ref, o_ref, wo_ref, g_ref, w1_ref, w2_ref, y_ref, x1_s, xn_s, acc_s):
    f = pl.program_id(1)

    @pl.when(f == 0)
    def _():
        x1 = x_ref[...] + jnp.dot(wo_ref[...], o_ref[...], preferred_element_type=F32)
        x1_s[...] = x1
        ms = jnp.mean(x1 * x1, axis=0, keepdims=True)
        xn_s[...] = (x1 * lax.rsqrt(ms + EPS) * g_ref[...]).astype(BF16)
        acc_s[...] = jnp.zeros_like(acc_s)

    h = jnp.maximum(jnp.dot(w1_ref[...], xn_s[...], preferred_element_type=F32), 0.0)
    acc_s[...] += jnp.dot(w2_ref[...], (h * h).astype(BF16), preferred_element_type=F32)

    @pl.when(f == pl.num_programs(1) - 1)
    def _():
        y_ref[...] = x1_s[...] + acc_s[...]


def outproj_mlp(xT, oT, woT, gain, w1T, w2T, tm=512, tf=1024):
    d, t = xT.shape
    tm = _token_tile(t, tm)
    dff = w1T.shape[0]
    return pl.pallas_call(
        _outmlp_kernel,
        out_shape=jax.ShapeDtypeStruct((d, t), F32),
        grid=(t // tm, dff // tf),
        in_specs=[
            pl.BlockSpec((d, tm), lambda i, f: (0, i)),
            pl.BlockSpec((d, tm), lambda i, f: (0, i)),
            pl.BlockSpec((d, d), lambda i, f: (0, 0)),
            pl.BlockSpec((d, 1), lambda i, f: (0, 0)),
            pl.BlockSpec((tf, d), lambda i, f: (f, 0)),
            pl.BlockSpec((d, tf), lambda i, f: (0, f)),
        ],
        out_specs=pl.BlockSpec((d, tm), lambda i, f: (0, i)),
        scratch_shapes=[pltpu.VMEM((d, tm), F32), pltpu.VMEM((d, tm), BF16), pltpu.VMEM((d, tm), F32)],
        compiler_params=_params("parallel", "arbitrary"),
        name="outproj_mlp",
    )(xT, oT, woT, gain.reshape(d, 1), w1T, w2T)


def _compress_kernel(ch_ref, w1_ref, pe_ref, w2_ref, kn_ref, o_ref):
    slot = pl.program_id(0)
    bn, c, _ = ch_ref.shape[1:]
    w1 = w1_ref[0]
    ab = jnp.dot(ch_ref[0].reshape(bn * c, CMP_STRIDE * HEAD_DIM), w1, preferred_element_type=F32)
    pe_a = jnp.dot(pe_ref[0, 0], w1[:, :CMP_HIDDEN], preferred_element_type=F32)[0:1]
    pe_b = jnp.dot(pe_ref[0, 1], w1[:, CMP_HIDDEN:], preferred_element_type=F32)[0:1]
    nxt = pltpu.roll(ab[:, CMP_HIDDEN:], bn * c - 1, 0)
    hid = ab[:, :CMP_HIDDEN] + nxt + (pe_a + pe_b)
    act = hid / (1.0 + jnp.exp(-hid))
    out = jnp.dot(act.astype(BF16), w2_ref[0], preferred_element_type=F32)
    ms = jnp.mean(out * out, axis=-1, keepdims=True)
    normed = out * lax.rsqrt(ms + EPS) * kn_ref[...]
    o_ref[0] = jnp.where(slot == 0, normed, out).reshape(bn, c, HEAD_DIM)


def compress(ch, w1ab, pe, w2, k_gain, bn):
    _, nb, c, w = ch.shape
    return pl.pallas_call(
        _compress_kernel,
        out_shape=jax.ShapeDtypeStruct((2, nb, c, HEAD_DIM), F32),
        grid=(2, nb // bn),
        in_specs=[
            pl.BlockSpec((1, bn, c, w), lambda s, i: (s, i, 0, 0)),
            pl.BlockSpec((1, w, 2 * CMP_HIDDEN), lambda s, i: (s, 0, 0)),
            pl.BlockSpec((1, 2, 16, w), lambda s, i: (s, 0, 0, 0)),
            pl.BlockSpec((1, CMP_HIDDEN, HEAD_DIM), lambda s, i: (s, 0, 0)),
            pl.BlockSpec((1, HEAD_DIM), lambda s, i: (0, 0)),
        ],
        out_specs=pl.BlockSpec((1, bn, c, HEAD_DIM), lambda s, i: (s, i, 0, 0)),
        compiler_params=_params("parallel", "parallel"),
        name="nsa_compress",
    )(ch, w1ab, pe, w2, k_gain.reshape(1, HEAD_DIM))


def compress_weights(cmp_pe, cmp_w1, cmp_w2):
    half = CMP_STRIDE * HEAD_DIM
    w1 = cmp_w1.reshape(2, 2, half, CMP_HIDDEN)
    w1ab = jnp.concatenate([w1[:, 0], w1[:, 1]], axis=-1).astype(BF16)
    pe = jnp.broadcast_to(cmp_pe.reshape(2, 2, 1, half), (2, 2, 16, half)).astype(BF16)
    return w1ab, pe, cmp_w2.astype(BF16)


CMP_CHUNK = 128


def _cmp_topk_kernel(q_ref, ck_ref, cv_ref, map_ref, bias_ref, oc_ref, sel_ref, v_s, s_s, *, tq, n_top, k_const):
    i = pl.program_id(2)
    c = ck_ref.shape[2]
    n_sel = map_ref.shape[0]
    n_chunks = c // CMP_CHUNK
    ms = []
    for r in range(REP):
        q = q_ref[r * HEAD_DIM:(r + 1) * HEAD_DIM, :]
        cm = None
        for jc in range(n_chunks):
            k0 = i * (tq // CMP_STRIDE) + jc * CMP_CHUNK
            start = pl.multiple_of(jnp.minimum(k0, k_const), 8)
            s = (jnp.dot(ck_ref[0, 0, jc * CMP_CHUNK:(jc + 1) * CMP_CHUNK, :], q, preferred_element_type=F32)
                 + bias_ref[r, pl.ds(start, CMP_CHUNK), :])
            s_s[r, jc * CMP_CHUNK:(jc + 1) * CMP_CHUNK, :] = s
            mx = jnp.max(s, axis=0, keepdims=True)
            cm = mx if cm is None else jnp.maximum(cm, mx)
        ms.append(jnp.where(cm < 0.5 * NEG, 0.0, cm))
    psum = None
    for r in range(REP):
        p = jnp.exp2(s_s[r] - ms[r])
        p = p * (1.0 / jnp.maximum(jnp.sum(p, axis=0, keepdims=True), 1e-30))
        oc_ref[r * HEAD_DIM:(r + 1) * HEAD_DIM, :] = jnp.dot(
            cv_ref[0, 0], p.astype(BF16), preferred_element_type=F32)
        psum = p if psum is None else psum + p
    hi = psum.astype(BF16)
    lo = (psum - hi.astype(F32)).astype(BF16)
    imp = (jnp.dot(map_ref[...], hi, preferred_element_type=F32)
           + jnp.dot(map_ref[...], lo, preferred_element_type=F32))
    qpos = i * tq + lax.broadcasted_iota(jnp.int32, (n_sel, tq), 1)
    j = lax.broadcasted_iota(jnp.int32, (n_sel, tq), 0)
    cur = qpos // SEL_BLOCK
    forced = (j == 0) | (j == cur) | (j == cur - 1)
    future = j * SEL_BLOCK > qpos
    v = jnp.where(forced, jnp.inf, jnp.where(future, -jnp.inf, imp))
    v_s[...] = v
    groups = n_sel // 8
    row = lax.broadcasted_iota(jnp.int32, (8, 128), 0)
    for lt in range(tq // 128):
        lanes = slice(lt * 128, (lt + 1) * 128)
        vg = [v_s[8 * gi:8 * gi + 8, lanes] for gi in range(groups)]
        cnt = [jnp.zeros((8, 128), F32) for _ in range(groups)]
        for a in range(n_sel):
            va = v_s[a:a + 1, lanes]
            for gi in range(groups):
                if 8 * gi > a:
                    ahead = va >= vg[gi]
                elif 8 * gi + 7 < a:
                    ahead = va > vg[gi]
                else:
                    ahead = (va > vg[gi]) | ((va == vg[gi]) & (row > a - 8 * gi))
                cnt[gi] = cnt[gi] + jnp.where(ahead, 1.0, 0.0)
        bpt = n_sel // sel_ref.shape[2]
        for gi in range(groups):
            mask = jnp.where(cnt[gi] < n_top, 0.0, NEG)
            for hh in range(8 // bpt):
                part = mask if hh == 0 else pltpu.roll(mask, 8 - hh * bpt, 0)
                part = jnp.where(row < bpt, part, 0.0)
                sel_ref[0, 0, gi * (8 // bpt) + hh, :, lanes] = jnp.concatenate(
                    [part, jnp.zeros((SEL_ROWS - 8, 128), F32)], axis=0).astype(BF16)


def cmp_strip(rel_bias, c, tq, rows):
    k = np.arange(rows)[:, None]
    tl = np.arange(tq)[None, :]
    d = CMP_STRIDE * (k - (c - 1)) + tl - (CMP_LEN - 1)
    return bias_lookup(d, d >= 0, head_rows(rel_bias, tq))


def cmp_topk(qT, ckr, cvTr, rel_bias, b, s, tq=256, key_tile=256, tables=None):
    c = s // CMP_STRIDE
    n_sel = s // SEL_BLOCK
    nkt = s // key_tile
    assert 8 % (key_tile // SEL_BLOCK) == 0 and n_sel % 8 == 0
    n_top = min(SEL_TOP_N, n_sel)
    nq = s // tq
    k_const = (c - 1) + -(-(BIAS_CONST_FROM + CMP_LEN - 1) // CMP_STRIDE)
    k_const = -(-k_const // 8) * 8
    strip = _cached(tables, ("cmp", c, tq), lambda: cmp_strip(rel_bias, c, tq, k_const + CMP_CHUNK))
    cc = np.arange(c)[::-1]
    c_end = cc * CMP_STRIDE + CMP_LEN - 1
    c_start = c_end - CMP_LEN + 1
    j0 = np.arange(n_sel)[:, None] * SEL_BLOCK
    sel_map = ((c_start[None] <= j0 + SEL_BLOCK - 1) & (c_end[None] >= j0) & (cc[None] < c - 1))
    sel_map = jnp.asarray(sel_map.astype(np.float32), dtype=BF16)
    return pl.pallas_call(
        functools.partial(_cmp_topk_kernel, tq=tq, n_top=n_top, k_const=k_const),
        out_shape=(jax.ShapeDtypeStruct((N_HEADS * HEAD_DIM, b * s), F32),
                   jax.ShapeDtypeStruct((b, KV_HEADS, nkt, SEL_ROWS, s), BF16)),
        grid=(KV_HEADS, b, nq),
        in_specs=[
            pl.BlockSpec((GROUP_ROWS, tq), lambda g, n, i: (g, n * nq + i)),
            pl.BlockSpec((1, 1, c, HEAD_DIM), lambda g, n, i: (n, g, 0, 0)),
            pl.BlockSpec((1, 1, HEAD_DIM, c), lambda g, n, i: (n, g, 0, 0)),
            pl.BlockSpec((n_sel, c), lambda g, n, i: (0, 0)),
            pl.BlockSpec((REP, strip.shape[1], tq), lambda g, n, i: (g, 0, 0)),
        ],
        out_specs=(pl.BlockSpec((GROUP_ROWS, tq), lambda g, n, i: (g, n * nq + i)),
                   pl.BlockSpec((1, 1, nkt, SEL_ROWS, tq), lambda g, n, i: (n, g, 0, 0, i))),
        scratch_shapes=[pltpu.VMEM((n_sel, tq), F32), pltpu.VMEM((REP, c, tq), F32)],
        compiler_params=_params("parallel", "parallel", "parallel"),
        name="nsa_cmp_topk",
    )(qT, ckr, cvTr, sel_map, strip)


SEL_ROWS = 16


def _flash_kernel(*refs, tq, n_tiles_max, n_delta, use_sel):
    nh = REP
    if use_sel:
        q_ref, k_ref, v_ref, bias_ref, sel_ref = refs[:5]
        outs = refs[5:]
    else:
        q_ref, k_ref, v_ref, bias_ref = refs[:4]
        sel_ref = None
        outs = refs[4:]
    o_ref = outs[0]
    accs, s_bufs = outs[1:1 + nh], outs[1 + nh:]
    i = pl.program_id(2)
    n_tiles = jnp.minimum(i + 1, n_tiles_max)
    for acc in accs:
        acc[...] = jnp.zeros(acc.shape, F32)

    def scores(jt, s_buf):
        kt = jnp.maximum(i - jt, 0)
        kT = k_ref[0, 0, kt]
        tile = jnp.where(jt < n_tiles, jnp.minimum(jt, n_delta - 1), n_delta)
        cms = []
        for r in range(nh):
            q = q_ref[r * HEAD_DIM:(r + 1) * HEAD_DIM, :]
            if use_sel:
                q = jnp.concatenate([q, sel_ref[0, 0, kt]], axis=0)
            s = _tn_dot(kT, q) + bias_ref[r, tile]
            s_buf[r] = s
            cms.append(jnp.max(s, axis=0, keepdims=True))
        return tuple(cms)

    def accumulate(jt, s_buf, cms, ms):
        v = v_ref[0, 0, jnp.maximum(i - jt, 0)]
        new_m = []
        for r in range(nh):
            m_new = jnp.maximum(ms[r], cms[r])
            alpha = jnp.exp2(ms[r] - m_new)
            p = jnp.exp2((s_buf[r] - m_new).astype(BF16))
            accs[r][...] = alpha * accs[r][...] + jnp.dot(v, p, preferred_element_type=F32)
            new_m.append(m_new)
        return tuple(new_m)

    def pair(jj, carry):
        ms, cm0 = carry
        cm1 = scores(2 * jj + 1, s_bufs[1])
        ms = accumulate(2 * jj, s_bufs[0], cm0, ms)
        cm0 = scores(2 * jj + 2, s_bufs[0])
        ms = accumulate(2 * jj + 1, s_bufs[1], cm1, ms)
        return ms, cm0

    init = (tuple(jnp.full((1, tq), NEG, F32) for _ in range(nh)), scores(0, s_bufs[0]))
    lax.fori_loop(0, (n_tiles + 1) // 2, pair, init)
    for r in range(nh):
        o_ref[r * HEAD_DIM:(r + 1) * HEAD_DIM, :] = (
            accs[r][0:HEAD_DIM, :] * (1.0 / accs[r][HEAD_DIM:HEAD_DIM + 1, :]))


def flash_bias_tiles(rel_bias, t, n_delta, dil, window):
    j = np.arange(n_delta + 1)[:, None, None]
    sl = np.arange(t)[None, :, None]
    tl = np.arange(t)[None, None, :]
    d = j * t + tl - sl
    ok = (d >= 0) & (j < n_delta)
    if window is not None:
        ok = ok & (d <= window)
    tiles = bias_lookup((d * dil).reshape(-1, t), ok.reshape(-1, t), head_rows(rel_bias, t))
    return tiles.reshape(N_HEADS, n_delta + 1, t, t)


def flash_attention(qT, kT, vT, rel_bias, nseq, seqlen, sel=None, window=None, dil=1, tables=None):
    nkt, tk = kT.shape[2], kT.shape[4]
    tq = tk
    assert seqlen == nkt * tk
    nq = nkt
    if window is None:
        n_delta = -(-(BIAS_CONST_FROM + tk - 1) // tk) + 1
        n_tiles_max = nkt
    else:
        n_tiles_max = (window + tk - 1) // tk + 1
        n_delta = n_tiles_max
    tiles = _cached(tables, ("flash", tk, n_delta, dil, window),
                    lambda: flash_bias_tiles(rel_bias, tk, n_delta, dil, window))
    h_rows = qT.shape[0]
    use_sel = sel is not None
    in_specs = [
        pl.BlockSpec((GROUP_ROWS, tq), lambda g, n, i: (g, n * nq + i)),
        pl.BlockSpec((1, 1) + kT.shape[2:], lambda g, n, i: (n, g, 0, 0, 0)),
        pl.BlockSpec((1, 1) + vT.shape[2:], lambda g, n, i: (n, g, 0, 0, 0)),
        pl.BlockSpec((REP, n_delta + 1, tk, tq), lambda g, n, i: (g, 0, 0, 0)),
    ]
    args = [qT, kT, vT, tiles]
    if use_sel:
        in_specs.append(pl.BlockSpec((1, 1) + sel.shape[2:4] + (tq,), lambda g, n, i: (n, g, 0, 0, i)))
        args.append(sel)
    return pl.pallas_call(
        functools.partial(_flash_kernel, tq=tq, n_tiles_max=n_tiles_max, n_delta=n_delta, use_sel=use_sel),
        out_shape=jax.ShapeDtypeStruct((h_rows, nseq * seqlen), F32),
        grid=(KV_HEADS, nseq, nq),
        in_specs=in_specs,
        out_specs=pl.BlockSpec((GROUP_ROWS, tq), lambda g, n, i: (g, n * nq + i)),
        scratch_shapes=([pltpu.VMEM((vT.shape[3], tq), F32) for _ in range(REP)]
                        + [pltpu.VMEM((REP, tk, tq), F32) for _ in range(2)]),
        compiler_params=_params("parallel", "parallel", "parallel"),
        name="flash_sel" if use_sel else ("flash_band" if window is not None else "flash_causal"),
    )(*args)


def _band_kernel(q_ref, k_ref, v_ref, bias_ref, o_ref, mo_ref, lo_ref, *, t, n_tiles, sub):
    step = pl.program_id(2)
    for u in range(sub):
        i = step * sub + u
        cols = slice(u * t, (u + 1) * t)
        kts = [jnp.maximum(i - jt, 0) for jt in range(n_tiles)]
        tiles = [jnp.where(jt <= i, jt, n_tiles) for jt in range(n_tiles)]
        kTs = [k_ref[0, 0, kt] for kt in kts]
        vTs = [v_ref[0, 0, kt] for kt in kts]
        for r in range(REP):
            rows = slice(r * HEAD_DIM, (r + 1) * HEAD_DIM)
            q = q_ref[rows, cols]
            s = [_tn_dot(kTs[jt], q) + bias_ref[r, tiles[jt]] for jt in range(n_tiles)]
            m = functools.reduce(jnp.maximum, [jnp.max(x, axis=0, keepdims=True) for x in s])
            p = [jnp.exp2(x - m) for x in s]
            den = functools.reduce(jnp.add, [jnp.sum(x, axis=0, keepdims=True) for x in p])
            acc = functools.reduce(jnp.add, [jnp.dot(vTs[jt], p[jt].astype(BF16), preferred_element_type=F32)
                                             for jt in range(n_tiles)])
            o_ref[rows, cols] = acc
            mo_ref[0, r:r + 1, cols] = m
            lo_ref[0, r:r + 1, cols] = den


def band_attention(qT, kT, vT, rel_bias, nseq, seqlen, window, dil, sub=8, tables=None):
    nkt, t = kT.shape[2], kT.shape[4]
    n_tiles = (window + t - 1) // t + 1
    sub = min(sub, nkt)
    assert seqlen == nkt * t and nkt % sub == 0
    nq = nkt // sub
    tiles = _cached(tables, ("flash", t, n_tiles, dil, window),
                    lambda: flash_bias_tiles(rel_bias, t, n_tiles, dil, window))
    o_spec = pl.BlockSpec((GROUP_ROWS, sub * t), lambda g, n, i: (g, n * nq + i))
    st_spec = pl.BlockSpec((1, REP, sub * t), lambda g, n, i: (g, 0, n * nq + i))
    st_shape = jax.ShapeDtypeStruct((KV_HEADS, REP, nseq * seqlen), F32)
    return pl.pallas_call(
        functools.partial(_band_kernel, t=t, n_tiles=n_tiles, sub=sub),
        out_shape=(jax.ShapeDtypeStruct((qT.shape[0], nseq * seqlen), F32), st_shape, st_shape),
        grid=(KV_HEADS, nseq, nq),
        in_specs=[
            pl.BlockSpec((GROUP_ROWS, sub * t), lambda g, n, i: (g, n * nq + i)),
            pl.BlockSpec((1, 1) + kT.shape[2:], lambda g, n, i: (n, g, 0, 0, 0)),
            pl.BlockSpec((1, 1) + vT.shape[2:], lambda g, n, i: (n, g, 0, 0, 0)),
            pl.BlockSpec((REP, n_tiles + 1, t, t), lambda g, n, i: (g, 0, 0, 0)),
        ],
        out_specs=(o_spec, st_spec, st_spec),
        compiler_params=_params("parallel", "parallel", "parallel"),
        name="band_attention",
    )(qT, kT, vT, tiles)


def _nsa_combine_kernel(oc_ref, os_ref, ow_ref, g_ref, o_ref):
    for h in range(N_HEADS):
        rows = slice(h * HEAD_DIM, (h + 1) * HEAD_DIM)
        o = (g_ref[h:h + 1, :] * oc_ref[rows, :]
             + g_ref[N_HEADS + h:N_HEADS + h + 1, :] * os_ref[rows, :]
             + g_ref[2 * N_HEADS + h:2 * N_HEADS + h + 1, :] * ow_ref[rows, :])
        o_ref[rows, :] = o.astype(BF16)


def nsa_combine(ocT, osT, owT, gatesT, tm=512):
    d, t = ocT.shape
    tm = _token_tile(t, tm)
    big = pl.BlockSpec((d, tm), lambda i: (0, i))
    return pl.pallas_call(
        _nsa_combine_kernel,
        out_shape=jax.ShapeDtypeStruct((d, t), BF16),
        grid=(t // tm,),
        in_specs=[big, big, big, pl.BlockSpec((N_BRANCHES * N_HEADS, tm), lambda i: (0, i))],
        out_specs=big,
        compiler_params=_params("parallel"),
        name="nsa_combine",
    )(ocT, osT, owT, gatesT)


def _dil_merge_kernel(*refs):
    ng = (len(refs) - 3)
    acc_refs, (m_ref, l_ref, o_ref) = refs[:ng], refs[ng:]
    mx = functools.reduce(jnp.maximum, [m_ref[g] for g in range(ng)])
    w = [jnp.exp2(m_ref[g] - mx) for g in range(ng)]
    den = functools.reduce(jnp.add, [w[g] * l_ref[g] for g in range(ng)])
    inv = 1.0 / den
    for h in range(N_HEADS):
        rows = slice(h * HEAD_DIM, (h + 1) * HEAD_DIM)
        num = functools.reduce(jnp.add, [w[g][h:h + 1, :] * acc_refs[g][rows, :] for g in range(ng)])
        o_ref[rows, :] = (num * inv[h:h + 1, :]).astype(BF16)


def dil_merge(accs, m, l, tm=512):
    ng = len(accs)
    d, t = accs[0].shape
    tm = _token_tile(t, tm)
    st = pl.BlockSpec((ng, N_HEADS, tm), lambda i: (0, 0, i))
    big = pl.BlockSpec((d, tm), lambda i: (0, i))
    return pl.pallas_call(
        _dil_merge_kernel,
        out_shape=jax.ShapeDtypeStruct((d, t), BF16),
        grid=(t // tm,),
        in_specs=[big] * ng + [st, st],
        out_specs=big,
        compiler_params=_params("parallel"),
        name="dil_merge",
    )(*accs, m, l)


NSA_SEGS = (
    (0, 512, "norm", 0, ATTN_SCALE), (512, 512, "norm", 0, ATTN_SCALE),
    (1024, 512, "plain", 0, 1.0),
    (1536, 256, "norm", 1, 1.0),
    (1792, 256, "plain", 0, 1.0),
    (2048, 256, "norm", 2, 1.0),
    (2304, 256, "plain", 0, 1.0),
    (2560, N_BRANCHES * N_HEADS, "sigmoid", 0, 1.0),
)
KV_SEGS = ((0, 256, "norm", 0, 1.0), (256, 256, "plain", 0, 1.0))
DIL_Q_SEGS = tuple((512 * j, 512, "norm", j // 2, ATTN_SCALE) for j in range(6))


def kv_tiles(xT, nseq, seqlen, t, extra=None):
    nt = seqlen // t
    x = xT.reshape(KV_HEADS, HEAD_DIM, nseq, nt, t).transpose(2, 0, 3, 1, 4).astype(BF16)
    if extra is not None:
        col = np.arange(t)[None, :] // SEL_BLOCK if extra == "blocks" else np.zeros((1, t), np.int64)
        ind = (col == np.arange(SEL_ROWS)[:, None]).astype(np.float32)
        ind = jnp.broadcast_to(jnp.asarray(ind, dtype=BF16), x.shape[:3] + (SEL_ROWS, t))
        x = jnp.concatenate([x, ind], axis=3)
    return x


def nsa_attention_prompt(zT, b, s, rel_bias, cw, k_gain_cmp, t=256, tables=None):
    qT = (zT[:1024] * LOG2E).astype(BF16)
    tok = zT[1024:2560].T
    rows = tok[:, :1024].reshape(b, s, 4, KV_HEADS, HEAD_DIM)
    win = tok[:, 1024:].reshape(b, s, 2, KV_HEADS, HEAD_DIM)
    gatesT = zT[2560:2560 + N_BRANCHES * N_HEADS]
    c = s // CMP_STRIDE
    ch = rows[:, :, 0:2].transpose(2, 0, 3, 1, 4).reshape(2, b * KV_HEADS, c, CMP_STRIDE * HEAD_DIM).astype(BF16)
    cmp = compress(ch, *cw, k_gain_cmp, bn=4)
    cmp = jnp.flip(cmp, axis=2).reshape(2, b, KV_HEADS, c, HEAD_DIM)
    ckr = cmp[0].astype(BF16)
    cvTr = cmp[1].transpose(0, 1, 3, 2).astype(BF16)
    ocT, mask = cmp_topk(qT, ckr, cvTr, rel_bias, b, s, key_tile=t, tables=tables)
    osT = flash_attention(qT, kv_tiles(zT[1536:1792], b, s, t, "blocks"), kv_tiles(zT[1792:2048], b, s, t, "ones"),
                          rel_bias, b, s, sel=mask, tables=tables)
    owT = flash_attention(qT, kv_tiles(zT[2048:2304], b, s, t), kv_tiles(zT[2304:2560], b, s, t, "ones"),
                          rel_bias, b, s, window=NSA_WINDOW, tables=tables)
    oT = nsa_combine(ocT, osT, owT, gatesT)
    return oT, rows, win


def dil_attention_prompt(qallT, kvT, b, s, rel_bias, t=128, tables=None):
    accs, ms, ls = [], [], []
    for gi, (window, dil) in enumerate(DIL_GROUPS):
        length = s // dil

        def split(x):
            r = x.shape[0]
            return x.reshape(r, b, length, dil).transpose(0, 1, 3, 2).reshape(r, b * s)

        def merge(x):
            r = x.shape[0]
            return x.reshape(r, b, dil, length).transpose(0, 1, 3, 2).reshape(r, b * s)

        qg = split((qallT[1024 * gi:1024 * (gi + 1)] * LOG2E).astype(BF16))
        kvg = split(kvT.astype(BF16))
        acc, m, l = band_attention(qg, kv_tiles(kvg[:KV_WIDTH], b * dil, length, t),
                                   kv_tiles(kvg[KV_WIDTH:], b * dil, length, t), rel_bias, b * dil, length,
                                   window // dil, dil, tables=tables)

        accs.append(merge(acc))
        ms.append(merge(m.reshape(N_HEADS, b * s)))
        ls.append(merge(l.reshape(N_HEADS, b * s)))
    return dil_merge(accs, jnp.stack(ms), jnp.stack(ls))


NEW_ROWS = 16


def _group_diagonal(oT, lanes_per_group):
    grp = lax.broadcasted_iota(jnp.int32, (HEAD_DIM, oT.shape[1]), 1) // lanes_per_group
    out = jnp.zeros((HEAD_DIM, oT.shape[1]), F32)
    for g in range(KV_HEADS):
        out = jnp.where(grp == g, oT[g * HEAD_DIM:(g + 1) * HEAD_DIM, :], out)
    return out


def _nsa_sample_kernel(pt_ref, *refs, n_pages, page, n_top, past_len, dt):
    del pt_ref
    pages = refs[:n_pages]
    (win_ref, new_ref, q_ref, gate_ref, w1l_ref, w1f_ref, pe_ref, w2_ref, kn_ref, map_ref, gs_ref,
     bc_ref, bs_ref, bw_ref, o_ref, x_s, s_s, v_s) = refs[n_pages:]
    q = q_ref[0]
    lanes = q.shape[1]
    n_ch = past_len // CMP_STRIDE

    pairs = KV_WIDTH // 128
    for p in range(n_pages):
        for jj in range(2 * pairs):
            x_s[jj, p * page:(p + 1) * page, :] = pages[p][0, 128 * jj:128 * (jj + 1), :].T
    comp = []
    for slot in range(2):
        ab = [None] * KV_HEADS
        for l in range(CMP_STRIDE):
            for hp in range(pairs):
                xl = x_s.at[slot * pairs + hp][pl.ds(l, n_ch, stride=CMP_STRIDE), :].astype(BF16)
                for gg in range(2):
                    g = 2 * hp + gg
                    d = jnp.dot(xl[:, gg * HEAD_DIM:(gg + 1) * HEAD_DIM], w1l_ref[slot, l],
                                preferred_element_type=F32)
                    ab[g] = d if ab[g] is None else ab[g] + d
        w1f = w1f_ref[slot]
        pe = (jnp.dot(pe_ref[slot, 0], w1f[:, :CMP_HIDDEN], preferred_element_type=F32)
              + jnp.dot(pe_ref[slot, 1], w1f[:, CMP_HIDDEN:], preferred_element_type=F32))[0:1]
        hid = jnp.concatenate(
            [a[:, :CMP_HIDDEN] + pltpu.roll(a[:, CMP_HIDDEN:], n_ch - 1, 0) + pe for a in ab], axis=1)
        act = hid / (1.0 + jnp.exp(-hid))
        comp.append(jnp.dot(act.astype(BF16), w2_ref[slot], preferred_element_type=F32))
    ck = comp[0]
    lane_g = lax.broadcasted_iota(jnp.int32, ck.shape, 1) // HEAD_DIM
    sq = ck * ck
    scale = jnp.zeros_like(ck)
    for g in range(KV_HEADS):
        ms = jnp.sum(jnp.where(lane_g == g, sq, 0.0), axis=1, keepdims=True) * (1.0 / HEAD_DIM)
        scale = jnp.where(lane_g == g, lax.rsqrt(ms + EPS), scale)
    ck = (ck * scale * kn_ref[...]).astype(BF16)
    cv = comp[1].astype(BF16)

    sc = jnp.dot(ck, q, preferred_element_type=F32) + bc_ref[...]
    m = jnp.max(sc, axis=0, keepdims=True)
    m = jnp.where(m < 0.5 * NEG, 0.0, m)
    p = jnp.exp(sc - m)
    p = p * (1.0 / jnp.maximum(jnp.sum(p, axis=0, keepdims=True), 1e-30))
    ocT = _tn_dot(cv, p.astype(BF16))
    hi = p.astype(BF16)
    lo = (p - hi.astype(F32)).astype(BF16)
    m1 = (jnp.dot(map_ref[...], hi, preferred_element_type=F32)
          + jnp.dot(map_ref[...], lo, preferred_element_type=F32))
    hi = m1.astype(BF16)
    lo = (m1 - hi.astype(F32)).astype(BF16)
    imp = (jnp.dot(hi, gs_ref[...], preferred_element_type=F32)
           + jnp.dot(lo, gs_ref[...], preferred_element_type=F32))
    n_sel_pad = imp.shape[0]
    n_sel = -(-(past_len + dt) // SEL_BLOCK)
    j = lax.broadcasted_iota(jnp.int32, (n_sel_pad, lanes), 0)
    qpos = past_len + lax.broadcasted_iota(jnp.int32, (n_sel_pad, lanes), 1) % dt
    cur = qpos // SEL_BLOCK
    forced = (j == 0) | (j == cur) | (j == cur - 1)
    future = (j * SEL_BLOCK > qpos) | (j >= n_sel)
    v = jnp.where(forced, jnp.inf, jnp.where(future, -jnp.inf, imp))
    v_s[...] = v

    def count(a, cnt):
        va = v_s[pl.ds(a, 1), :]
        ahead = (va > v) | ((va == v) & (j > a))
        return cnt + jnp.where(ahead, 1.0, 0.0)

    rank = lax.fori_loop(0, n_sel, count, jnp.zeros((n_sel_pad, lanes), F32))
    sel = jnp.where((rank < n_top) & (j < n_sel), 0.0, NEG)

    blocks_per_page = page // SEL_BLOCK
    m_run = jnp.full((page, lanes), NEG, F32)
    for pg in range(n_pages):
        kT = pages[pg][0, 2 * KV_WIDTH:3 * KV_WIDTH, :].astype(BF16)
        mask = jnp.concatenate(
            [jnp.broadcast_to(sel[blocks_per_page * pg + bb:blocks_per_page * pg + bb + 1], (SEL_BLOCK, lanes))
             for bb in range(blocks_per_page)], axis=0)
        s = _tn_dot(kT, q) + bs_ref[pg * page:(pg + 1) * page, :] + mask
        s_s[pg * page:(pg + 1) * page, :] = s
        m_run = jnp.maximum(m_run, s)
    nb_past = past_len // SEL_BLOCK
    s_new = (jnp.dot(new_ref[0, 0], q, preferred_element_type=F32) + bs_ref[past_len:past_len + NEW_ROWS, :]
             + sel[nb_past:nb_past + 1])
    m = jnp.maximum(jnp.max(m_run, axis=0, keepdims=True), jnp.max(s_new, axis=0, keepdims=True))
    pn = jnp.exp(s_new - m)
    den = jnp.sum(pn, axis=0, keepdims=True)
    acc = _tn_dot(new_ref[0, 1], pn.astype(BF16))
    for pg in range(n_pages):
        pp = jnp.exp(s_s[pg * page:(pg + 1) * page, :] - m)
        den = den + jnp.sum(pp, axis=0, keepdims=True)
        acc = acc + jnp.dot(pages[pg][0, 3 * KV_WIDTH:4 * KV_WIDTH, :].astype(BF16), pp.astype(BF16),
                            preferred_element_type=F32)
    osT = acc * (1.0 / den)

    wb = win_ref.shape[2]
    s_w = _tn_dot(win_ref[0, 0:KV_WIDTH, :].astype(BF16), q) + bw_ref[0:wb, :]
    s_n = jnp.dot(new_ref[0, 2], q, preferred_element_type=F32) + bw_ref[wb:wb + NEW_ROWS, :]
    m = jnp.maximum(jnp.max(s_w, axis=0, keepdims=True), jnp.max(s_n, axis=0, keepdims=True))
    pw = jnp.exp(s_w - m)
    pn = jnp.exp(s_n - m)
    den = jnp.sum(pw, axis=0, keepdims=True) + jnp.sum(pn, axis=0, keepdims=True)
    acc = (jnp.dot(win_ref[0, KV_WIDTH:2 * KV_WIDTH, :].astype(BF16), pw.astype(BF16), preferred_element_type=F32)
           + _tn_dot(new_ref[0, 3], pn.astype(BF16)))
    owT = acc * (1.0 / den)

    lpg = lanes // KV_HEADS
    o_ref[0] = (gate_ref[0, 0:1, :] * _group_diagonal(ocT, lpg)
                + gate_ref[0, 1:2, :] * _group_diagonal(osT, lpg)
                + gate_ref[0, 2:3, :] * _group_diagonal(owT, lpg))


def _lane_table(rel_bias, dist, valid, dt):
    lanes = np.arange(N_HEADS * dt)
    h, t = lanes // dt, lanes % dt
    return bias_lookup(dist[:, t], valid[:, t], rel_bias[:, jnp.asarray(h)][None])[0]


def nsa_sample_tables(rel_bias, past_len, wb, dt):
    t = np.arange(dt)[None, :]
    n_ch = past_len // CMP_STRIDE
    c = np.arange(n_ch)[:, None]
    dist = past_len + t - (CMP_STRIDE * c + CMP_LEN - 1)
    bc = _lane_table(rel_bias, dist, (c < n_ch - 1) & (dist >= 0), dt)
    s = np.arange(past_len)[:, None]
    n = np.arange(NEW_ROWS)[:, None]
    dist = np.concatenate([past_len + t - s, t - n])
    valid = np.concatenate([np.ones((past_len, dt), bool), (n < dt) & (t - n >= 0)])
    bs = _lane_table(rel_bias, dist, valid, dt)
    l = np.arange(wb)[:, None]
    dist = np.concatenate([wb + t - l, t - n])
    valid = np.concatenate([wb + t - l <= NSA_WINDOW, (n < dt) & (t - n >= 0)])
    bw = _lane_table(rel_bias, dist, valid, dt)
    return bc, bs, bw


def nsa_sample_constants(past_len, dt):
    n_ch = past_len // CMP_STRIDE
    n_sel = -(-(past_len + dt) // SEL_BLOCK)
    n_sel_pad = -(-n_sel // 8) * 8
    c = np.arange(n_ch)[None, :]
    j0 = np.arange(n_sel_pad)[:, None] * SEL_BLOCK
    c_start, c_end = CMP_STRIDE * c, CMP_STRIDE * c + CMP_LEN - 1
    sel_map = (c_start <= j0 + SEL_BLOCK - 1) & (c_end >= j0) & (c < n_ch - 1) & (j0 < n_sel * SEL_BLOCK)
    lanes = np.arange(N_HEADS * dt)
    grp, t = lanes // (REP * dt), lanes % dt
    gsum = (grp[:, None] == grp[None, :]) & (t[:, None] == t[None, :])
    return (jnp.asarray(sel_map.astype(np.float32), dtype=BF16), jnp.asarray(gsum.astype(np.float32), dtype=BF16))


def nsa_sample_attention(cache_pages, page_idx, win_cache, win_base, new_rows, qbd, gates, cw, k_gain_cmp, tables):
    db, n_pages = page_idx.shape
    page = cache_pages.shape[2]
    past_len = n_pages * page
    lanes = qbd.shape[2]
    dt = lanes // N_HEADS
    wb = win_cache.shape[2]
    w1ab, pe, w2 = cw
    w1l = w1ab.reshape(2, CMP_STRIDE, HEAD_DIM, 2 * CMP_HIDDEN)
    w2bd = jnp.einsum('shd,gk->sghkd', w2, jnp.eye(KV_HEADS, dtype=w2.dtype)).reshape(
        2, KV_HEADS * CMP_HIDDEN, KV_WIDTH)
    kn = jnp.tile(k_gain_cmp.reshape(1, HEAD_DIM), (1, KV_HEADS))
    sel_map, gsum = nsa_sample_constants(past_len, dt)
    bc, bs, bw = tables
    n_sel = -(-(past_len + dt) // SEL_BLOCK)
    assert past_len % SEL_BLOCK == 0 and dt <= SEL_BLOCK and page % SEL_BLOCK == 0

    def const(a):
        nd = a.ndim
        return pl.BlockSpec(a.shape, lambda i, pt: (0,) * nd)

    page_specs = [pl.BlockSpec((1, cache_pages.shape[1], page), functools.partial(lambda p, i, pt: (pt[i, p], 0, 0), p))
                  for p in range(n_pages)]
    consts = [w1l, w1ab, pe, w2bd, kn, sel_map, gsum, bc, bs, bw]
    grid_spec = pltpu.PrefetchScalarGridSpec(
        num_scalar_prefetch=1,
        grid=(db,),
        in_specs=page_specs + [
            pl.BlockSpec((1, win_cache.shape[1], wb), lambda i, pt: (win_base + i, 0, 0)),
            pl.BlockSpec((1,) + new_rows.shape[1:], lambda i, pt: (i, 0, 0, 0)),
            pl.BlockSpec((1,) + qbd.shape[1:], lambda i, pt: (i, 0, 0)),
            pl.BlockSpec((1,) + gates.shape[1:], lambda i, pt: (i, 0, 0)),
        ] + [const(a) for a in consts],
        out_specs=pl.BlockSpec((1, HEAD_DIM, lanes), lambda i, pt: (i, 0, 0)),
        scratch_shapes=[pltpu.VMEM((2 * KV_WIDTH // 128, past_len, 128), F32),
                        pltpu.VMEM((past_len, lanes), F32),
                        pltpu.VMEM((sel_map.shape[0], lanes), F32)],
    )
    return pl.pallas_call(
        functools.partial(_nsa_sample_kernel, n_pages=n_pages, page=page, n_top=min(SEL_TOP_N, n_sel),
                          past_len=past_len, dt=dt),
        out_shape=jax.ShapeDtypeStruct((db, HEAD_DIM, lanes), F32),
        grid_spec=grid_spec,
        compiler_params=_params("parallel"),
        name="nsa_sample",
    )(page_idx, *([cache_pages] * n_pages), win_cache, new_rows, qbd, gates, *consts)


def _dil_sample_kernel(kv_ref, new_ref, q_ref, b0_ref, b1_ref, b2_ref, o_ref, s_s, *, los, buf_len):
    lanes = q_ref.shape[3]
    chunk = 128
    ms, dens, accs = [], [], []
    for gi, (bias_ref, lo) in enumerate(zip((b0_ref, b1_ref, b2_ref), los)):
        q = q_ref[0, gi]
        n_chunks = (buf_len - lo) // chunk
        m_run = jnp.full((chunk, lanes), NEG, F32)
        for c in range(n_chunks):
            r0 = lo + c * chunk
            s = (_tn_dot(kv_ref[0, 0:KV_WIDTH, r0:r0 + chunk].astype(BF16), q)
                 + bias_ref[c * chunk:(c + 1) * chunk, :])
            s_s[c * chunk:(c + 1) * chunk, :] = s
            m_run = jnp.maximum(m_run, s)
        s_new = (jnp.dot(new_ref[0, 0], q, preferred_element_type=F32)
                 + bias_ref[n_chunks * chunk:n_chunks * chunk + NEW_ROWS, :])
        m = jnp.maximum(jnp.max(m_run, axis=0, keepdims=True), jnp.max(s_new, axis=0, keepdims=True))
        pn = jnp.exp(s_new - m)
        den = jnp.sum(pn, axis=0, keepdims=True)
        acc = _tn_dot(new_ref[0, 1], pn.astype(BF16))
        for c in range(n_chunks):
            r0 = lo + c * chunk
            pp = jnp.exp(s_s[c * chunk:(c + 1) * chunk, :] - m)
            den = den + jnp.sum(pp, axis=0, keepdims=True)
            acc = acc + jnp.dot(kv_ref[0, KV_WIDTH:2 * KV_WIDTH, r0:r0 + chunk].astype(BF16), pp.astype(BF16),
                                preferred_element_type=F32)
        ms.append(m)
        dens.append(den)
        accs.append(acc)
    mx = functools.reduce(jnp.maximum, ms)
    w = [jnp.exp(m - mx) for m in ms]
    num = functools.reduce(jnp.add, [wg * a for wg, a in zip(w, accs)])
    den = functools.reduce(jnp.add, [wg * d for wg, d in zip(w, dens)])
    o_ref[0] = _group_diagonal(num * (1.0 / den), lanes // KV_HEADS)


def dil_sample_tables(rel_bias, buf_len, dt):
    t = np.arange(dt)[None, :]
    n = np.arange(NEW_ROWS)[:, None]
    tables, los = [], []
    for window, dil in DIL_GROUPS:
        lo = max(0, buf_len - window) // 128 * 128
        s = np.arange(lo, buf_len)[:, None]
        dist = np.concatenate([buf_len + t - s, t - n])
        valid = np.concatenate([np.ones((buf_len - lo, dt), bool), (n < dt) & (t - n >= 0)])
        valid = valid & (dist % dil == 0) & (dist <= window)
        tables.append(_lane_table(rel_bias, dist, valid, dt))
        los.append(lo)
    return tables, tuple(los)


def dil_sample_attention(kv_cache, new_rows, qbd3, tables, los):
    db, width, buf_len = kv_cache.shape
    lanes = qbd3.shape[3]
    assert buf_len % 128 == 0
    tspec = [pl.BlockSpec(tb.shape, lambda i: (0, 0)) for tb in tables]
    return pl.pallas_call(
        functools.partial(_dil_sample_kernel, los=los, buf_len=buf_len),
        out_shape=jax.ShapeDtypeStruct((db, HEAD_DIM, lanes), F32),
        grid=(db,),
        in_specs=[
            pl.BlockSpec((1, width, buf_len), lambda i: (i, 0, 0)),
            pl.BlockSpec((1,) + new_rows.shape[1:], lambda i: (i, 0, 0, 0)),
            pl.BlockSpec((1,) + qbd3.shape[1:], lambda i: (i, 0, 0, 0)),
        ] + tspec,
        out_specs=pl.BlockSpec((1, HEAD_DIM, lanes), lambda i: (i, 0, 0)),
        scratch_shapes=[pltpu.VMEM((buf_len, lanes), F32)],
        compiler_params=_params("parallel"),
        name="dil_sample",
    )(kv_cache, new_rows, qbd3, *tables)


def lane_queries(qT, db, dt):
    q = qT.reshape(KV_HEADS, REP, HEAD_DIM, db, dt).transpose(3, 0, 2, 1, 4).reshape(db, KV_HEADS, HEAD_DIM, REP * dt)
    eye = jnp.eye(KV_HEADS, dtype=q.dtype)
    qbd = q[:, :, :, None, :] * eye[None, :, None, :, None]
    return qbd.reshape(db, KV_WIDTH, KV_HEADS * REP * dt).astype(BF16)


def lanes_to_features(o, db, dt):
    o = o.reshape(db, HEAD_DIM, KV_HEADS, REP, dt).transpose(2, 3, 1, 0, 4)
    return o.reshape(N_HEADS * HEAD_DIM, db * dt)


def new_token_rows(zT_rows, db, dt):
    n = zT_rows.shape[0] // KV_WIDTH
    r = zT_rows.reshape(n, KV_WIDTH, db, dt).transpose(2, 0, 3, 1)
    return jnp.pad(r, ((0, 0), (0, 0), (0, NEW_ROWS - dt), (0, 0))).astype(BF16)


def kernel(x_prompt, x_sample, cache_nsa_kv, cache_win_kv, cache_dil_kv, page_table, rel_bias, a_attn_norm, a_w_in, a_q_norm, a_k_norm, a_cmp_pe, a_cmp_w1, a_cmp_w2, a_w_out, kv_norm, w_kv_shared, k_norm_shared, b_attn_norm, b_w_q, b_q_norm, b_w_out, mlp_norm, mlp_w1, mlp_w2):
    b, s, d = x_prompt.shape
    db, dt, _ = x_sample.shape
    n_a = a_w_in.shape[0]
    n_b = b_w_q.shape[0]
    n_pool, page = cache_nsa_kv.shape[1:3]
    wb = cache_win_kv.shape[2]
    buf_len = cache_dil_kv.shape[1]
    past_len = page_table.shape[1] * page
    assert N_HEADS * dt == 128
    tables = {}
    xpT = x_prompt.reshape(b * s, d).T
    xsT = x_sample.reshape(db * dt, d).T
    cache_pages = cache_nsa_kv.transpose(0, 1, 3, 4, 5, 2).reshape(n_a * n_pool, -1, page)
    win_cache = cache_win_kv.transpose(0, 1, 3, 4, 5, 2).reshape(n_a * db, -1, wb)
    nsa_tables = nsa_sample_tables(rel_bias, past_len, wb, dt)
    rows_p, rows_s, wins_p, wins_s = [], [], [], []
    for l in range(n_a + n_b):
        w1T = mlp_w1[l].T.astype(BF16)
        w2T = mlp_w2[l].T.astype(BF16)
        if l < n_a:
            w_inT = a_w_in[l].T.astype(BF16)
            norms = jnp.stack([a_q_norm[l], a_k_norm[l][1], a_k_norm[l][2]])[..., None]
            cw = compress_weights(a_cmp_pe[l], a_cmp_w1[l], a_cmp_w2[l])
            woT = a_w_out[l].T.astype(BF16)
            zT = norm_proj(xpT, a_attn_norm[l], w_inT, norms, NSA_SEGS)
            oT, rows, win = nsa_attention_prompt(zT, b, s, rel_bias, cw, a_k_norm[l][0], tables=tables)
            rows_p.append(rows)
            wins_p.append(win[:, -min(NSA_WINDOW, s):])
            xpT = outproj_mlp(xpT, oT, woT, mlp_norm[l], w1T, w2T)
            zs = norm_proj(xsT, a_attn_norm[l], w_inT, norms, NSA_SEGS)
            gates = zs[2560:2560 + N_BRANCHES * N_HEADS].reshape(N_BRANCHES, KV_HEADS, REP, db, dt)
            gates = gates.transpose(3, 0, 1, 2, 4).reshape(db, N_BRANCHES, N_HEADS * dt)
            o = nsa_sample_attention(
                cache_pages, page_table + l * n_pool, win_cache, l * db, new_token_rows(zs[1536:2560], db, dt),
                lane_queries(zs[:1024], db, dt), gates, cw, a_k_norm[l][0], nsa_tables)
            kv6 = zs[1024:2560].T.reshape(db, dt, 6, KV_HEADS, HEAD_DIM)
            rows_s.append(kv6[:, :, 0:4])
            wins_s.append(kv6[:, :, 4:6])
            xsT = outproj_mlp(xsT, lanes_to_features(o, db, dt).astype(BF16), woT, mlp_norm[l], w1T, w2T)
        else:
            i = l - n_a
            if i == 0:
                w_kvT = w_kv_shared.T.astype(BF16)
                kn = k_norm_shared.reshape(1, HEAD_DIM, 1)
                kvpT = norm_proj(xpT, kv_norm, w_kvT, kn, KV_SEGS)
                kvp = kvpT.T.reshape(b, s, 2, KV_HEADS, HEAD_DIM)
                kvsT = norm_proj(xsT, kv_norm, w_kvT, kn, KV_SEGS)
                kvs = kvsT.T.reshape(db, dt, 2, KV_HEADS, HEAD_DIM)
                dmax = max(w for w, _ in DIL_GROUPS)
                new_dil_p = kvp[:, -min(dmax, s):]
                new_dil_s = jnp.concatenate([cache_dil_kv, kvs], axis=1)[:, -min(dmax, buf_len + dt):]
                dil_cache = cache_dil_kv.transpose(0, 2, 3, 4, 1).reshape(db, -1, buf_len)
                dil_new = new_token_rows(kvsT, db, dt)
                dil_tables, dil_los = dil_sample_tables(rel_bias, buf_len, dt)
            w_qT = b_w_q[i].T.astype(BF16)
            qn = b_q_norm[i][..., None]
            woT = b_w_out[i].T.astype(BF16)
            qT = norm_proj(xpT, b_attn_norm[i], w_qT, qn, DIL_Q_SEGS)
            oT = dil_attention_prompt(qT, kvpT, b, s, rel_bias, tables=tables)
            xpT = outproj_mlp(xpT, oT, woT, mlp_norm[l], w1T, w2T)
            qs = norm_proj(xsT, b_attn_norm[i], w_qT, qn, DIL_Q_SEGS)
            qbd3 = jnp.stack([lane_queries(qs[1024 * gi:1024 * (gi + 1)], db, dt) for gi in range(len(DIL_GROUPS))], axis=1)
            o = dil_sample_attention(dil_cache, dil_new, qbd3, dil_tables, dil_los)
            xsT = outproj_mlp(xsT, lanes_to_features(o, db, dt).astype(BF16), woT, mlp_norm[l], w1T, w2T)
    return (xpT.T.reshape(b, s, d), xsT.T.reshape(db, dt, d),
            jnp.stack(rows_p), jnp.stack(rows_s), jnp.stack(wins_p),
            jnp.concatenate([cache_win_kv, jnp.stack(wins_s)], axis=2)[:, :, -min(NSA_WINDOW, wb + dt):],
            new_dil_p, new_dil_s)
```

```python
import functools
import math

import numpy as np
import jax
import jax.numpy as jnp
from jax import lax
from jax.experimental import pallas as pl
from jax.experimental.pallas import tpu as pltpu

F32 = jnp.float32
BF16 = jnp.bfloat16

D_MODEL = 1024
N_HEADS = 16
HEAD_DIM = 64
KV_HEADS = 4
REP = N_HEADS // KV_HEADS
GROUP_ROWS = REP * HEAD_DIM
KV_WIDTH = KV_HEADS * HEAD_DIM
D_FF = 4 * D_MODEL
EPS = 1e-6
ATTN_SCALE = HEAD_DIM ** -0.5
N_BUCKETS = 32
MAX_DISTANCE = 2048
CMP_LEN = 32
CMP_STRIDE = 16
CMP_HIDDEN = 2 * HEAD_DIM
SEL_BLOCK = 64
SEL_TOP_N = 16
NSA_WINDOW = 512
N_BRANCHES = 3
NSA_IN = N_HEADS * HEAD_DIM + 6 * KV_HEADS * HEAD_DIM + N_BRANCHES * N_HEADS
DIL_GROUPS = ((128, 1), (512, 4), (2048, 16))
PAGE_SIZE = 128

NEG = -1e30
LOG2E = math.log2(math.e)
VMEM_LIMIT = 56 * 1024 * 1024
BIAS_TABLE_LEN = 2048


def _bucket_of_distance(d):
    max_exact = N_BUCKETS // 2
    d = np.maximum(d, 0)
    ratio = np.log(np.maximum(d, 1).astype(np.float32) / np.float32(max_exact)) / np.float32(
        math.log(MAX_DISTANCE / max_exact))
    large = max_exact + (ratio * np.float32(N_BUCKETS - max_exact)).astype(np.int32)
    return np.where(d < max_exact, d, np.minimum(large, N_BUCKETS - 1)).astype(np.int32)


_BUCKETS = _bucket_of_distance(np.arange(BIAS_TABLE_LEN))
BIAS_CONST_FROM = int(np.argmax(_BUCKETS == N_BUCKETS - 1))
assert np.all(_BUCKETS[BIAS_CONST_FROM:] == N_BUCKETS - 1)


def _params(*sem):
    return pltpu.CompilerParams(dimension_semantics=sem, vmem_limit_bytes=VMEM_LIMIT)


def _token_tile(t, tm):
    tm = min(tm, t)
    assert t % tm == 0, (t, tm)
    return tm


def _tn_dot(a, b):
    return lax.dot_general(a, b, (((0,), (0,)), ((), ())), preferred_element_type=F32)


def _bias_lookup_kernel(bkt_ref, tab_ref, o_ref):
    bkt = bkt_ref[...]
    v = jnp.full(bkt.shape, NEG, F32)
    for bb in range(N_BUCKETS):
        v = jnp.where(bkt == bb, tab_ref[0, bb:bb + 1, :], v)
    o_ref[0] = v


def bias_lookup(dist, valid, tab):
    rows, lanes = dist.shape
    bkt = np.where(valid, _BUCKETS[np.clip(dist, 0, BIAS_TABLE_LEN - 1)], N_BUCKETS).astype(np.int32)
    rb = next(r for r in range(min(rows, 512) // 8 * 8, 0, -8) if rows % r == 0)
    nh = tab.shape[0]
    return pl.pallas_call(
        _bias_lookup_kernel,
        out_shape=jax.ShapeDtypeStruct((nh, rows, lanes), F32),
        grid=(rows // rb, nh),
        in_specs=[pl.BlockSpec((rb, lanes), lambda r, h: (r, 0)),
                  pl.BlockSpec((1, N_BUCKETS, lanes), lambda r, h: (h, 0, 0))],
        out_specs=pl.BlockSpec((1, rb, lanes), lambda r, h: (h, r, 0)),
        compiler_params=_params("parallel", "parallel"),
        name="bias_lookup",
    )(jnp.asarray(bkt), tab)


def _cached(tables, key, build):
    if tables is None:
        return build()
    if key not in tables:
        tables[key] = build()
    return tables[key]


def head_rows(rel_bias, lanes):
    return jnp.broadcast_to((rel_bias.T * LOG2E)[:, :, None], (N_HEADS, N_BUCKETS, lanes))


def _proj_kernel(x_ref, g_ref, w_ref, n_ref, o_ref, *, segs):
    x = x_ref[...]
    ms = jnp.mean(x * x, axis=0, keepdims=True)
    xn = (x * lax.rsqrt(ms + EPS) * g_ref[...]).astype(BF16)
    tm = x.shape[1]
    for r0, nr, kind, ni, scale in segs:
        z = jnp.dot(w_ref[r0:r0 + nr, :], xn, preferred_element_type=F32)
        if kind == "norm":
            z3 = z.reshape(nr // HEAD_DIM, HEAD_DIM, tm)
            hs = jnp.mean(z3 * z3, axis=1, keepdims=True)
            z = (z3 * lax.rsqrt(hs + EPS) * n_ref[ni][None] * scale).reshape(nr, tm)
        elif kind == "sigmoid":
            z = 1.0 / (1.0 + jnp.exp(-z))
        o_ref[r0:r0 + nr, :] = z


def norm_proj(xT, gain, wT, norms, segs, tm=512):
    d, t = xT.shape
    tm = _token_tile(t, tm)
    n = wT.shape[0]
    return pl.pallas_call(
        functools.partial(_proj_kernel, segs=tuple(segs)),
        out_shape=jax.ShapeDtypeStruct((n, t), F32),
        grid=(t // tm,),
        in_specs=[
            pl.BlockSpec((d, tm), lambda i: (0, i)),
            pl.BlockSpec((d, 1), lambda i: (0, 0)),
            pl.BlockSpec((n, d), lambda i: (0, 0)),
            pl.BlockSpec(norms.shape, lambda i: (0, 0, 0)),
        ],
        out_specs=pl.BlockSpec((n, tm), lambda i: (0, i)),
        compiler_params=_params("parallel"),
        name="norm_proj",
    )(xT, gain.reshape(d, 1), wT, norms)


def _outmlp_kernel(x_ref, o_ref, wo_ref, g_ref, w1_ref, w2_ref, y_ref, x1_s, xn_s, acc_s):
    f = pl.program_id(1)

    @pl.when(f == 0)
    def _():
        x1 = x_ref[...] + jnp.dot(wo_ref[...], o_ref[...], preferred_element_type=F32)
        x1_s[...] = x1
        ms = jnp.mean(x1 * x1, axis=0, keepdims=True)
        xn_s[...] = (x1 * lax.rsqrt(ms + EPS) * g_ref[...]).astype(BF16)
        acc_s[...] = jnp.zeros_like(acc_s)

    h = jnp.maximum(jnp.dot(w1_ref[...], xn_s[...], preferred_element_type=F32), 0.0)
    acc_s[...] += jnp.dot(w2_ref[...], (h * h).astype(BF16), preferred_element_type=F32)

    @pl.when(f == pl.num_programs(1) - 1)
    def _():
        y_ref[...] = x1_s[...] + acc_s[...]


def outproj_mlp(xT, oT, woT, gain, w1T, w2T, tm=1024, tf=1024):
    d, t = xT.shape
    tm = _token_tile(t, tm)
    dff = w1T.shape[0]
    return pl.pallas_call(
        _outmlp_kernel,
        out_shape=jax.ShapeDtypeStruct((d, t), F32),
        grid=(t // tm, dff // tf),
        in_specs=[
            pl.BlockSpec((d, tm), lambda i, f: (0, i)),
            pl.BlockSpec((d, tm), lambda i, f: (0, i)),
            pl.BlockSpec((d, d), lambda i, f: (0, 0)),
            pl.BlockSpec((d, 1), lambda i, f: (0, 0)),
            pl.BlockSpec((tf, d), lambda i, f: (f, 0)),
            pl.BlockSpec((d, tf), lambda i, f: (0, f)),
        ],
        out_specs=pl.BlockSpec((d, tm), lambda i, f: (0, i)),
        scratch_shapes=[pltpu.VMEM((d, tm), F32), pltpu.VMEM((d, tm), BF16), pltpu.VMEM((d, tm), F32)],
        compiler_params=_params("parallel", "arbitrary"),
        name="outproj_mlp",
    )(xT, oT, woT, gain.reshape(d, 1), w1T, w2T)


def _compress_kernel(ch_ref, w1_ref, pe_ref, w2_ref, kn_ref, o_ref):
    slot = pl.program_id(0)
    bn, c, _ = ch_ref.shape[1:]
    w1 = w1_ref[0]
    ab = jnp.dot(ch_ref[0].reshape(bn * c, CMP_STRIDE * HEAD_DIM), w1, preferred_element_type=F32)
    pe_a = jnp.dot(pe_ref[0, 0], w1[:, :CMP_HIDDEN], preferred_element_type=F32)[0:1]
    pe_b = jnp.dot(pe_ref[0, 1], w1[:, CMP_HIDDEN:], preferred_element_type=F32)[0:1]
    nxt = pltpu.roll(ab[:, CMP_HIDDEN:], bn * c - 1, 0)
    hid = ab[:, :CMP_HIDDEN] + nxt + (pe_a + pe_b)
    act = hid / (1.0 + jnp.exp(-hid))
    out = jnp.dot(act.astype(BF16), w2_ref[0], preferred_element_type=F32)
    ms = jnp.mean(out * out, axis=-1, keepdims=True)
    normed = out * lax.rsqrt(ms + EPS) * kn_ref[...]
    o_ref[0] = jnp.where(slot == 0, normed, out).reshape(bn, c, HEAD_DIM)


def compress(ch, w1ab, pe, w2, k_gain, bn):
    _, nb, c, w = ch.shape
    return pl.pallas_call(
        _compress_kernel,
        out_shape=jax.ShapeDtypeStruct((2, nb, c, HEAD_DIM), F32),
        grid=(2, nb // bn),
        in_specs=[
            pl.BlockSpec((1, bn, c, w), lambda s, i: (s, i, 0, 0)),
            pl.BlockSpec((1, w, 2 * CMP_HIDDEN), lambda s, i: (s, 0, 0)),
            pl.BlockSpec((1, 2, 16, w), lambda s, i: (s, 0, 0, 0)),
            pl.BlockSpec((1, CMP_HIDDEN, HEAD_DIM), lambda s, i: (s, 0, 0)),
            pl.BlockSpec((1, HEAD_DIM), lambda s, i: (0, 0)),
        ],
        out_specs=pl.BlockSpec((1, bn, c, HEAD_DIM), lambda s, i: (s, i, 0, 0)),
        compiler_params=_params("parallel", "parallel"),
        name="nsa_compress",
    )(ch, w1ab, pe, w2, k_gain.reshape(1, HEAD_DIM))


def compress_weights(cmp_pe, cmp_w1, cmp_w2):
    half = CMP_STRIDE * HEAD_DIM
    w1 = cmp_w1.reshape(2, 2, half, CMP_HIDDEN)
    w1ab = jnp.concatenate([w1[:, 0], w1[:, 1]], axis=-1).astype(BF16)
    pe = jnp.broadcast_to(cmp_pe.reshape(2, 2, 1, half), (2, 2, 16, half)).astype(BF16)
    return w1ab, pe, cmp_w2.astype(BF16)


CMP_CHUNK = 128


def _cmp_topk_kernel(q_ref, ck_ref, cv_ref, map_ref, bias_ref, oc_ref, sel_ref, v_s, s_s, cnt_s, *,
                     tq, n_top, k_const):
    i = pl.program_id(2)
    c = ck_ref.shape[2]
    n_sel = map_ref.shape[0]
    n_chunks = c // CMP_CHUNK
    ms = []
    for r in range(REP):
        q = q_ref[r * HEAD_DIM:(r + 1) * HEAD_DIM, :]
        cm = None
        for jc in range(n_chunks):
            k0 = i * (tq // CMP_STRIDE) + jc * CMP_CHUNK
            start = pl.multiple_of(jnp.minimum(k0, k_const), 8)
            s = (jnp.dot(ck_ref[0, 0, jc * CMP_CHUNK:(jc + 1) * CMP_CHUNK, :], q, preferred_element_type=F32)
                 + bias_ref[r, pl.ds(start, CMP_CHUNK), :])
            s_s[r, jc * CMP_CHUNK:(jc + 1) * CMP_CHUNK, :] = s
            mx = jnp.max(s, axis=0, keepdims=True)
            cm = mx if cm is None else jnp.maximum(cm, mx)
        ms.append(jnp.where(cm < 0.5 * NEG, 0.0, cm))
    psum = None
    for r in range(REP):
        p = jnp.exp2(s_s[r] - ms[r])
        p = p * (1.0 / jnp.maximum(jnp.sum(p, axis=0, keepdims=True), 1e-30))
        oc_ref[r * HEAD_DIM:(r + 1) * HEAD_DIM, :] = jnp.dot(
            cv_ref[0, 0], p.astype(BF16), preferred_element_type=F32)
        psum = p if psum is None else psum + p
    hi = psum.astype(BF16)
    lo = (psum - hi.astype(F32)).astype(BF16)
    imp = (jnp.dot(map_ref[...], hi, preferred_element_type=F32)
           + jnp.dot(map_ref[...], lo, preferred_element_type=F32))
    qpos = i * tq + lax.broadcasted_iota(jnp.int32, (n_sel, tq), 1)
    j = lax.broadcasted_iota(jnp.int32, (n_sel, tq), 0)
    cur = qpos // SEL_BLOCK
    forced = (j == 0) | (j == cur) | (j == cur - 1)
    future = j * SEL_BLOCK > qpos
    v = jnp.where(forced, jnp.inf, jnp.where(future, -jnp.inf, imp))
    v_s[...] = v
    groups = n_sel // 8
    row = lax.broadcasted_iota(jnp.int32, (8, 128), 0)
    cnt_s[...] = jnp.zeros(cnt_s.shape, F32)
    newest = (i * tq + tq - 1) // SEL_BLOCK
    for ga in range(groups):
        @pl.when(8 * ga <= newest)
        def _(ga=ga):
            for lt in range(tq // 128):
                lanes = slice(lt * 128, (lt + 1) * 128)
                vg = [v_s[8 * gi:8 * gi + 8, lanes] for gi in range(groups)]
                cnt = [cnt_s[8 * gi:8 * gi + 8, lanes] for gi in range(groups)]
                for a in range(8 * ga, 8 * ga + 8):
                    va = v_s[a:a + 1, lanes]
                    for gi in range(groups):
                        if 8 * gi > a:
                            ahead = va >= vg[gi]
                        elif 8 * gi + 7 < a:
                            ahead = va > vg[gi]
                        else:
                            ahead = (va > vg[gi]) | ((va == vg[gi]) & (row > a - 8 * gi))
                        cnt[gi] = cnt[gi] + jnp.where(ahead, 1.0, 0.0)
                for gi in range(groups):
                    cnt_s[8 * gi:8 * gi + 8, lanes] = cnt[gi]

    for lt in range(tq // 128):
        lanes = slice(lt * 128, (lt + 1) * 128)
        bpt = n_sel // sel_ref.shape[2]
        for gi in range(groups):
            mask = jnp.where(cnt_s[8 * gi:8 * gi + 8, lanes] < n_top, 0.0, NEG)
            for hh in range(8 // bpt):
                part = mask if hh == 0 else pltpu.roll(mask, 8 - hh * bpt, 0)
                part = jnp.where(row < bpt, part, 0.0)
                sel_ref[0, 0, gi * (8 // bpt) + hh, :, lanes] = jnp.concatenate(
                    [part, jnp.zeros((SEL_ROWS - 8, 128), F32)], axis=0).astype(BF16)


def cmp_strip(rel_bias, c, tq, rows):
    k = np.arange(rows)[:, None]
    tl = np.arange(tq)[None, :]
    d = CMP_STRIDE * (k - (c - 1)) + tl - (CMP_LEN - 1)
    return bias_lookup(d, d >= 0, head_rows(rel_bias, tq))


def cmp_topk(qT, ckr, cvTr, rel_bias, b, s, tq=256, key_tile=256, tables=None):
    c = s // CMP_STRIDE
    n_sel = s // SEL_BLOCK
    nkt = s // key_tile
    assert 8 % (key_tile // SEL_BLOCK) == 0 and n_sel % 8 == 0
    n_top = min(SEL_TOP_N, n_sel)
    nq = s // tq
    k_const = (c - 1) + -(-(BIAS_CONST_FROM + CMP_LEN - 1) // CMP_STRIDE)
    k_const = -(-k_const // 8) * 8
    strip = _cached(tables, ("cmp", c, tq), lambda: cmp_strip(rel_bias, c, tq, k_const + CMP_CHUNK))
    cc = np.arange(c)[::-1]
    c_end = cc * CMP_STRIDE + CMP_LEN - 1
    c_start = c_end - CMP_LEN + 1
    j0 = np.arange(n_sel)[:, None] * SEL_BLOCK
    sel_map = ((c_start[None] <= j0 + SEL_BLOCK - 1) & (c_end[None] >= j0) & (cc[None] < c - 1))
    sel_map = jnp.asarray(sel_map.astype(np.float32), dtype=BF16)
    return pl.pallas_call(
        functools.partial(_cmp_topk_kernel, tq=tq, n_top=n_top, k_const=k_const),
        out_shape=(jax.ShapeDtypeStruct((N_HEADS * HEAD_DIM, b * s), F32),
                   jax.ShapeDtypeStruct((b, KV_HEADS, nkt, SEL_ROWS, s), BF16)),
        grid=(KV_HEADS, b, nq),
        in_specs=[
            pl.BlockSpec((GROUP_ROWS, tq), lambda g, n, i: (g, n * nq + i)),
            pl.BlockSpec((1, 1, c, HEAD_DIM), lambda g, n, i: (n, g, 0, 0)),
            pl.BlockSpec((1, 1, HEAD_DIM, c), lambda g, n, i: (n, g, 0, 0)),
            pl.BlockSpec((n_sel, c), lambda g, n, i: (0, 0)),
            pl.BlockSpec((REP, strip.shape[1], tq), lambda g, n, i: (g, 0, 0)),
        ],
        out_specs=(pl.BlockSpec((GROUP_ROWS, tq), lambda g, n, i: (g, n * nq + i)),
                   pl.BlockSpec((1, 1, nkt, SEL_ROWS, tq), lambda g, n, i: (n, g, 0, 0, i))),
        scratch_shapes=[pltpu.VMEM((n_sel, tq), F32), pltpu.VMEM((REP, c, tq), F32), pltpu.VMEM((n_sel, tq), F32)],
        compiler_params=_params("parallel", "parallel", "parallel"),
        name="nsa_cmp_topk",
    )(qT, ckr, cvTr, sel_map, strip)


SEL_ROWS = 16


def _flash_kernel(*refs, tq, n_tiles_max, n_delta, use_sel):
    nh = REP
    if use_sel:
        q_ref, k_ref, v_ref, bias_ref, sel_ref = refs[:5]
        outs = refs[5:]
    else:
        q_ref, k_ref, v_ref, bias_ref = refs[:4]
        sel_ref = None
        outs = refs[4:]
    o_ref = outs[0]
    accs, s_bufs = outs[1:1 + nh], outs[1 + nh:]
    i = pl.program_id(2)
    n_tiles = jnp.minimum(i + 1, n_tiles_max)
    for acc in accs:
        acc[...] = jnp.zeros(acc.shape, F32)

    def scores(jt, s_buf):
        kt = jnp.maximum(i - jt, 0)
        kT = k_ref[0, 0, kt]
        tile = jnp.where(jt < n_tiles, jnp.minimum(jt, n_delta - 1), n_delta)
        cms = []
        for r in range(nh):
            q = q_ref[r * HEAD_DIM:(r + 1) * HEAD_DIM, :]
            if use_sel:
                q = jnp.concatenate([q, sel_ref[0, 0, kt]], axis=0)
            s = _tn_dot(kT, q) + bias_ref[r, tile]
            s_buf[r] = s
            cms.append(jnp.max(s, axis=0, keepdims=True))
        return tuple(cms)

    def accumulate(jt, s_buf, cms, ms):
        v = v_ref[0, 0, jnp.maximum(i - jt, 0)]
        new_m = []
        for r in range(nh):
            m_new = jnp.maximum(ms[r], cms[r])
            alpha = jnp.exp2(ms[r] - m_new)
            p = jnp.exp2((s_buf[r] - m_new).astype(BF16))
            accs[r][...] = alpha * accs[r][...] + jnp.dot(v, p, preferred_element_type=F32)
            new_m.append(m_new)
        return tuple(new_m)

    def pair(jj, carry):
        ms, cm0 = carry
        cm1 = scores(2 * jj + 1, s_bufs[1])
        ms = accumulate(2 * jj, s_bufs[0], cm0, ms)
        cm0 = scores(2 * jj + 2, s_bufs[0])
        ms = accumulate(2 * jj + 1, s_bufs[1], cm1, ms)
        return ms, cm0

    init = (tuple(jnp.full((1, tq), NEG, F32) for _ in range(nh)), scores(0, s_bufs[0]))
    lax.fori_loop(0, (n_tiles + 1) // 2, pair, init)
    for r in range(nh):
        o_ref[r * HEAD_DIM:(r + 1) * HEAD_DIM, :] = (
            accs[r][0:HEAD_DIM, :] * (1.0 / accs[r][HEAD_DIM:HEAD_DIM + 1, :]))


def flash_bias_tiles(rel_bias, t, n_delta, dil, window):
    j = np.arange(n_delta + 1)[:, None, None]
    sl = np.arange(t)[None, :, None]
    tl = np.arange(t)[None, None, :]
    d = j * t + tl - sl
    ok = (d >= 0) & (j < n_delta)
    if window is not None:
        ok = ok & (d <= window)
    tiles = bias_lookup((d * dil).reshape(-1, t), ok.reshape(-1, t), head_rows(rel_bias, t))
    return tiles.reshape(N_HEADS, n_delta + 1, t, t)


def flash_attention(qT, kT, vT, rel_bias, nseq, seqlen, sel=None, window=None, dil=1, tables=None):
    nkt, tk = kT.shape[2], kT.shape[4]
    tq = tk
    assert seqlen == nkt * tk
    nq = nkt
    if window is None:
        n_delta = -(-(BIAS_CONST_FROM + tk - 1) // tk) + 1
        n_tiles_max = nkt
    else:
        n_tiles_max = (window + tk - 1) // tk + 1
        n_delta = n_tiles_max
    tiles = _cached(tables, ("flash", tk, n_delta, dil, window),
                    lambda: flash_bias_tiles(rel_bias, tk, n_delta, dil, window))
    h_rows = qT.shape[0]
    use_sel = sel is not None
    in_specs = [
        pl.BlockSpec((GROUP_ROWS, tq), lambda g, n, i: (g, n * nq + i)),
        pl.BlockSpec((1, 1) + kT.shape[2:], lambda g, n, i: (n, g, 0, 0, 0)),
        pl.BlockSpec((1, 1) + vT.shape[2:], lambda g, n, i: (n, g, 0, 0, 0)),
        pl.BlockSpec((REP, n_delta + 1, tk, tq), lambda g, n, i: (g, 0, 0, 0)),
    ]
    args = [qT, kT, vT, tiles]
    if use_sel:
        in_specs.append(pl.BlockSpec((1, 1) + sel.shape[2:4] + (tq,), lambda g, n, i: (n, g, 0, 0, i)))
        args.append(sel)
    return pl.pallas_call(
        functools.partial(_flash_kernel, tq=tq, n_tiles_max=n_tiles_max, n_delta=n_delta, use_sel=use_sel),
        out_shape=jax.ShapeDtypeStruct((h_rows, nseq * seqlen), F32),
        grid=(KV_HEADS, nseq, nq),
        in_specs=in_specs,
        out_specs=pl.BlockSpec((GROUP_ROWS, tq), lambda g, n, i: (g, n * nq + i)),
        scratch_shapes=([pltpu.VMEM((vT.shape[3], tq), F32) for _ in range(REP)]
                        + [pltpu.VMEM((REP, tk, tq), F32) for _ in range(2)]),
        compiler_params=_params("parallel", "parallel", "parallel"),
        name="flash_sel" if use_sel else ("flash_band" if window is not None else "flash_causal"),
    )(*args)


def _band_kernel(q_ref, k_ref, v_ref, bias_ref, o_ref, mo_ref, lo_ref, *, t, n_tiles, sub):
    step = pl.program_id(2)
    for u in range(sub):
        i = step * sub + u
        cols = slice(u * t, (u + 1) * t)
        kts = [jnp.maximum(i - jt, 0) for jt in range(n_tiles)]
        tiles = [jnp.where(jt <= i, jt, n_tiles) for jt in range(n_tiles)]
        kTs = [k_ref[0, 0, kt] for kt in kts]
        vTs = [v_ref[0, 0, kt] for kt in kts]
        for r in range(REP):
            rows = slice(r * HEAD_DIM, (r + 1) * HEAD_DIM)
            q = q_ref[rows, cols]
            s = [_tn_dot(kTs[jt], q) + bias_ref[r, tiles[jt]] for jt in range(n_tiles)]
            m = functools.reduce(jnp.maximum, [jnp.max(x, axis=0, keepdims=True) for x in s])
            p = [jnp.exp2(x - m) for x in s]
            den = functools.reduce(jnp.add, [jnp.sum(x, axis=0, keepdims=True) for x in p])
            acc = functools.reduce(jnp.add, [jnp.dot(vTs[jt], p[jt].astype(BF16), preferred_element_type=F32)
                                             for jt in range(n_tiles)])
            o_ref[rows, cols] = acc
            mo_ref[0, r:r + 1, cols] = m
            lo_ref[0, r:r + 1, cols] = den


def band_attention(qT, kT, vT, rel_bias, nseq, seqlen, window, dil, sub=8, tables=None):
    nkt, t = kT.shape[2], kT.shape[4]
    n_tiles = (window + t - 1) // t + 1
    sub = min(sub, nkt)
    assert seqlen == nkt * t and nkt % sub == 0
    nq = nkt // sub
    tiles = _cached(tables, ("flash", t, n_tiles, dil, window),
                    lambda: flash_bias_tiles(rel_bias, t, n_tiles, dil, window))
    o_spec = pl.BlockSpec((GROUP_ROWS, sub * t), lambda g, n, i: (g, n * nq + i))
    st_spec = pl.BlockSpec((1, REP, sub * t), lambda g, n, i: (g, 0, n * nq + i))
    st_shape = jax.ShapeDtypeStruct((KV_HEADS, REP, nseq * seqlen), F32)
    return pl.pallas_call(
        functools.partial(_band_kernel, t=t, n_tiles=n_tiles, sub=sub),
        out_shape=(jax.ShapeDtypeStruct((qT.shape[0], nseq * seqlen), F32), st_shape, st_shape),
        grid=(KV_HEADS, nseq, nq),
        in_specs=[
            pl.BlockSpec((GROUP_ROWS, sub * t), lambda g, n, i: (g, n * nq + i)),
            pl.BlockSpec((1, 1) + kT.shape[2:], lambda g, n, i: (n, g, 0, 0, 0)),
            pl.BlockSpec((1, 1) + vT.shape[2:], lambda g, n, i: (n, g, 0, 0, 0)),
            pl.BlockSpec((REP, n_tiles + 1, t, t), lambda g, n, i: (g, 0, 0, 0)),
        ],
        out_specs=(o_spec, st_spec, st_spec),
        compiler_params=_params("parallel", "parallel", "parallel"),
        name="band_attention",
    )(qT, kT, vT, tiles)


def _nsa_combine_kernel(oc_ref, os_ref, ow_ref, g_ref, o_ref):
    for h in range(N_HEADS):
        rows = slice(h * HEAD_DIM, (h + 1) * HEAD_DIM)
        o = (g_ref[h:h + 1, :] * oc_ref[rows, :]
             + g_ref[N_HEADS + h:N_HEADS + h + 1, :] * os_ref[rows, :]
             + g_ref[2 * N_HEADS + h:2 * N_HEADS + h + 1, :] * ow_ref[rows, :])
        o_ref[rows, :] = o.astype(BF16)


def nsa_combine(ocT, osT, owT, gatesT, tm=512):
    d, t = ocT.shape
    tm = _token_tile(t, tm)
    big = pl.BlockSpec((d, tm), lambda i: (0, i))
    return pl.pallas_call(
        _nsa_combine_kernel,
        out_shape=jax.ShapeDtypeStruct((d, t), BF16),
        grid=(t // tm,),
        in_specs=[big, big, big, pl.BlockSpec((N_BRANCHES * N_HEADS, tm), lambda i: (0, i))],
        out_specs=big,
        compiler_params=_params("parallel"),
        name="nsa_combine",
    )(ocT, osT, owT, gatesT)


def _dil_merge_kernel(*refs):
    ng = (len(refs) - 3)
    acc_refs, (m_ref, l_ref, o_ref) = refs[:ng], refs[ng:]
    mx = functools.reduce(jnp.maximum, [m_ref[g] for g in range(ng)])
    w = [jnp.exp2(m_ref[g] - mx) for g in range(ng)]
    den = functools.reduce(jnp.add, [w[g] * l_ref[g] for g in range(ng)])
    inv = 1.0 / den
    for h in range(N_HEADS):
        rows = slice(h * HEAD_DIM, (h + 1) * HEAD_DIM)
        num = functools.reduce(jnp.add, [w[g][h:h + 1, :] * acc_refs[g][rows, :] for g in range(ng)])
        o_ref[rows, :] = (num * inv[h:h + 1, :]).astype(BF16)


def dil_merge(accs, m, l, tm=512):
    ng = len(accs)
    d, t = accs[0].shape
    tm = _token_tile(t, tm)
    st = pl.BlockSpec((ng, N_HEADS, tm), lambda i: (0, 0, i))
    big = pl.BlockSpec((d, tm), lambda i: (0, i))
    return pl.pallas_call(
        _dil_merge_kernel,
        out_shape=jax.ShapeDtypeStruct((d, t), BF16),
        grid=(t // tm,),
        in_specs=[big] * ng + [st, st],
        out_specs=big,
        compiler_params=_params("parallel"),
        name="dil_merge",
    )(*accs, m, l)


NSA_SEGS = (
    (0, 512, "norm", 0, ATTN_SCALE), (512, 512, "norm", 0, ATTN_SCALE),
    (1024, 512, "plain", 0, 1.0),
    (1536, 256, "norm", 1, 1.0),
    (1792, 256, "plain", 0, 1.0),
    (2048, 256, "norm", 2, 1.0),
    (2304, 256, "plain", 0, 1.0),
    (2560, N_BRANCHES * N_HEADS, "sigmoid", 0, 1.0),
)
KV_SEGS = ((0, 256, "norm", 0, 1.0), (256, 256, "plain", 0, 1.0))
DIL_Q_SEGS = tuple((512 * j, 512, "norm", j // 2, ATTN_SCALE) for j in range(6))


def kv_tiles(xT, nseq, seqlen, t, extra=None):
    nt = seqlen // t
    x = xT.reshape(KV_HEADS, HEAD_DIM, nseq, nt, t).transpose(2, 0, 3, 1, 4).astype(BF16)
    if extra is not None:
        col = np.arange(t)[None, :] // SEL_BLOCK if extra == "blocks" else np.zeros((1, t), np.int64)
        ind = (col == np.arange(SEL_ROWS)[:, None]).astype(np.float32)
        ind = jnp.broadcast_to(jnp.asarray(ind, dtype=BF16), x.shape[:3] + (SEL_ROWS, t))
        x = jnp.concatenate([x, ind], axis=3)
    return x


def nsa_attention_prompt(zT, b, s, rel_bias, cw, k_gain_cmp, t=256, tables=None):
    qT = (zT[:1024] * LOG2E).astype(BF16)
    tok = zT[1024:2560].T
    rows = tok[:, :1024].reshape(b, s, 4, KV_HEADS, HEAD_DIM)
    win = tok[:, 1024:].reshape(b, s, 2, KV_HEADS, HEAD_DIM)
    gatesT = zT[2560:2560 + N_BRANCHES * N_HEADS]
    c = s // CMP_STRIDE
    ch = rows[:, :, 0:2].transpose(2, 0, 3, 1, 4).reshape(2, b * KV_HEADS, c, CMP_STRIDE * HEAD_DIM).astype(BF16)
    cmp = compress(ch, *cw, k_gain_cmp, bn=4)
    cmp = jnp.flip(cmp, axis=2).reshape(2, b, KV_HEADS, c, HEAD_DIM)
    ckr = cmp[0].astype(BF16)
    cvTr = cmp[1].transpose(0, 1, 3, 2).astype(BF16)
    ocT, mask = cmp_topk(qT, ckr, cvTr, rel_bias, b, s, key_tile=t, tables=tables)
    osT = flash_attention(qT, kv_tiles(zT[1536:1792], b, s, t, "blocks"), kv_tiles(zT[1792:2048], b, s, t, "ones"),
                          rel_bias, b, s, sel=mask, tables=tables)
    owT = flash_attention(qT, kv_tiles(zT[2048:2304], b, s, t), kv_tiles(zT[2304:2560], b, s, t, "ones"),
                          rel_bias, b, s, window=NSA_WINDOW, tables=tables)
    oT = nsa_combine(ocT, osT, owT, gatesT)
    return oT, rows, win


def dil_attention_prompt(qallT, kvT, b, s, rel_bias, t=128, tables=None):
    accs, ms, ls = [], [], []
    for gi, (window, dil) in enumerate(DIL_GROUPS):
        length = s // dil

        def split(x):
            r = x.shape[0]
            return x.reshape(r, b, length, dil).transpose(0, 1, 3, 2).reshape(r, b * s)

        def merge(x):
            r = x.shape[0]
            return x.reshape(r, b, dil, length).transpose(0, 1, 3, 2).reshape(r, b * s)

        qg = split((qallT[1024 * gi:1024 * (gi + 1)] * LOG2E).astype(BF16))
        kvg = split(kvT.astype(BF16))
        acc, m, l = band_attention(qg, kv_tiles(kvg[:KV_WIDTH], b * dil, length, t),
                                   kv_tiles(kvg[KV_WIDTH:], b * dil, length, t), rel_bias, b * dil, length,
                                   window // dil, dil, tables=tables)

        accs.append(merge(acc))
        ms.append(merge(m.reshape(N_HEADS, b * s)))
        ls.append(merge(l.reshape(N_HEADS, b * s)))
    return dil_merge(accs, jnp.stack(ms), jnp.stack(ls))


NEW_ROWS = 16


def _group_diagonal(oT, lanes_per_group):
    grp = lax.broadcasted_iota(jnp.int32, (HEAD_DIM, oT.shape[1]), 1) // lanes_per_group
    out = jnp.zeros((HEAD_DIM, oT.shape[1]), F32)
    for g in range(KV_HEADS):
        out = jnp.where(grp == g, oT[g * HEAD_DIM:(g + 1) * HEAD_DIM, :], out)
    return out


def _nsa_sample_kernel(pt_ref, *refs, n_pages, page, n_top, past_len, dt):
    del pt_ref
    pages = refs[:n_pages]
    (win_ref, new_ref, q_ref, gate_ref, wp_ref, perm_ref, w1f_ref, pe_ref, w2_ref, kn_ref, map_ref, gs_ref,
     bc_ref, bs_ref, bw_ref, o_ref, x_s, s_s, v_s) = refs[n_pages:]
    q = q_ref[0]
    lanes = q.shape[1]
    n_ch = past_len // CMP_STRIDE

    pairs = KV_WIDTH // 128
    cpp = page // CMP_STRIDE
    for p in range(n_pages):
        for jj in range(2 * pairs):
            f = pages[p][0, 128 * jj:128 * (jj + 1), :].astype(BF16)
            t = jnp.dot(f, perm_ref[...], preferred_element_type=F32).T
            x_s[jj, :, cpp * p:cpp * (p + 1), :] = t.reshape(CMP_STRIDE, cpp, 128)
    comp = []
    for slot in range(2):
        ab = []
        for hp in range(pairs):
            x = jnp.concatenate([x_s[slot * pairs + hp, l] for l in range(CMP_STRIDE)], axis=1).astype(BF16)
            both = jnp.dot(x, wp_ref[slot], preferred_element_type=F32)
            ab += [both[:, :2 * CMP_HIDDEN], both[:, 2 * CMP_HIDDEN:]]
        w1f = w1f_ref[slot]
        pe = (jnp.dot(pe_ref[slot, 0], w1f[:, :CMP_HIDDEN], preferred_element_type=F32)
              + jnp.dot(pe_ref[slot, 1], w1f[:, CMP_HIDDEN:], preferred_element_type=F32))[0:1]
        hid = jnp.concatenate(
            [a[:, :CMP_HIDDEN] + pltpu.roll(a[:, CMP_HIDDEN:], n_ch - 1, 0) + pe for a in ab], axis=1)
        act = hid / (1.0 + jnp.exp(-hid))
        comp.append(jnp.dot(act.astype(BF16), w2_ref[slot], preferred_element_type=F32))
    ck = comp[0]
    lane_g = lax.broadcasted_iota(jnp.int32, ck.shape, 1) // HEAD_DIM
    sq = ck * ck
    scale = jnp.zeros_like(ck)
    for g in range(KV_HEADS):
        ms = jnp.sum(jnp.where(lane_g == g, sq, 0.0), axis=1, keepdims=True) * (1.0 / HEAD_DIM)
        scale = jnp.where(lane_g == g, lax.rsqrt(ms + EPS), scale)
    ck = (ck * scale * kn_ref[...]).astype(BF16)
    cv = comp[1].astype(BF16)

    sc = jnp.dot(ck, q, preferred_element_type=F32) + bc_ref[...]
    m = jnp.max(sc, axis=0, keepdims=True)
    m = jnp.where(m < 0.5 * NEG, 0.0, m)
    p = jnp.exp(sc - m)
    p = p * (1.0 / jnp.maximum(jnp.sum(p, axis=0, keepdims=True), 1e-30))
    ocT = _tn_dot(cv, p.astype(BF16))
    hi = p.astype(BF16)
    lo = (p - hi.astype(F32)).astype(BF16)
    m1 = (jnp.dot(map_ref[...], hi, preferred_element_type=F32)
          + jnp.dot(map_ref[...], lo, preferred_element_type=F32))
    hi = m1.astype(BF16)
    lo = (m1 - hi.astype(F32)).astype(BF16)
    imp = (jnp.dot(hi, gs_ref[...], preferred_element_type=F32)
           + jnp.dot(lo, gs_ref[...], preferred_element_type=F32))
    n_sel_pad = imp.shape[0]
    n_sel = -(-(past_len + dt) // SEL_BLOCK)
    j = lax.broadcasted_iota(jnp.int32, (n_sel_pad, lanes), 0)
    qpos = past_len + lax.broadcasted_iota(jnp.int32, (n_sel_pad, lanes), 1) % dt
    cur = qpos // SEL_BLOCK
    forced = (j == 0) | (j == cur) | (j == cur - 1)
    future = (j * SEL_BLOCK > qpos) | (j >= n_sel)
    v = jnp.where(forced, jnp.inf, jnp.where(future, -jnp.inf, imp))
    v_s[...] = v

    def count(a, cnt):
        va = v_s[pl.ds(a, 1), :]
        ahead = (va > v) | ((va == v) & (j > a))
        return cnt + jnp.where(ahead, 1.0, 0.0)

    rank = lax.fori_loop(0, n_sel, count, jnp.zeros((n_sel_pad, lanes), F32))
    sel = jnp.where((rank < n_top) & (j < n_sel), 0.0, NEG)

    blocks_per_page = page // SEL_BLOCK
    m_run = jnp.full((page, lanes), NEG, F32)
    for pg in range(n_pages):
        kT = pages[pg][0, 2 * KV_WIDTH:3 * KV_WIDTH, :].astype(BF16)
        mask = jnp.concatenate(
            [jnp.broadcast_to(sel[blocks_per_page * pg + bb:blocks_per_page * pg + bb + 1], (SEL_BLOCK, lanes))
             for bb in range(blocks_per_page)], axis=0)
        s = _tn_dot(kT, q) + bs_ref[pg * page:(pg + 1) * page, :] + mask
        s_s[pg * page:(pg + 1) * page, :] = s
        m_run = jnp.maximum(m_run, s)
    nb_past = past_len // SEL_BLOCK
    s_new = (jnp.dot(new_ref[0, 0], q, preferred_element_type=F32) + bs_ref[past_len:past_len + NEW_ROWS, :]
             + sel[nb_past:nb_past + 1])
    m = jnp.maximum(jnp.max(m_run, axis=0, keepdims=True), jnp.max(s_new, axis=0, keepdims=True))
    pn = jnp.exp(s_new - m)
    den = jnp.sum(pn, axis=0, keepdims=True)
    acc = _tn_dot(new_ref[0, 1], pn.astype(BF16))
    for pg in range(n_pages):
        pp = jnp.exp(s_s[pg * page:(pg + 1) * page, :] - m)
        den = den + jnp.sum(pp, axis=0, keepdims=True)
        acc = acc + jnp.dot(pages[pg][0, 3 * KV_WIDTH:4 * KV_WIDTH, :].astype(BF16), pp.astype(BF16),
                            preferred_element_type=F32)
    osT = acc * (1.0 / den)

    wb = win_ref.shape[2]
    s_w = _tn_dot(win_ref[0, 0:KV_WIDTH, :].astype(BF16), q) + bw_ref[0:wb, :]
    s_n = jnp.dot(new_ref[0, 2], q, preferred_element_type=F32) + bw_ref[wb:wb + NEW_ROWS, :]
    m = jnp.maximum(jnp.max(s_w, axis=0, keepdims=True), jnp.max(s_n, axis=0, keepdims=True))
    pw = jnp.exp(s_w - m)
    pn = jnp.exp(s_n - m)
    den = jnp.sum(pw, axis=0, keepdims=True) + jnp.sum(pn, axis=0, keepdims=True)
    acc = (jnp.dot(win_ref[0, KV_WIDTH:2 * KV_WIDTH, :].astype(BF16), pw.astype(BF16), preferred_element_type=F32)
           + _tn_dot(new_ref[0, 3], pn.astype(BF16)))
    owT = acc * (1.0 / den)

    lpg = lanes // KV_HEADS
    o_ref[0] = (gate_ref[0, 0:1, :] * _group_diagonal(ocT, lpg)
                + gate_ref[0, 1:2, :] * _group_diagonal(osT, lpg)
                + gate_ref[0, 2:3, :] * _group_diagonal(owT, lpg))


def _lane_table(rel_bias, dist, valid, dt):
    lanes = np.arange(N_HEADS * dt)
    h, t = lanes // dt, lanes % dt
    return bias_lookup(dist[:, t], valid[:, t], rel_bias[:, jnp.asarray(h)][None])[0]


def nsa_sample_tables(rel_bias, past_len, wb, dt):
    t = np.arange(dt)[None, :]
    n_ch = past_len // CMP_STRIDE
    c = np.arange(n_ch)[:, None]
    dist = past_len + t - (CMP_STRIDE * c + CMP_LEN - 1)
    bc = _lane_table(rel_bias, dist, (c < n_ch - 1) & (dist >= 0), dt)
    s = np.arange(past_len)[:, None]
    n = np.arange(NEW_ROWS)[:, None]
    dist = np.concatenate([past_len + t - s, t - n])
    valid = np.concatenate([np.ones((past_len, dt), bool), (n < dt) & (t - n >= 0)])
    bs = _lane_table(rel_bias, dist, valid, dt)
    l = np.arange(wb)[:, None]
    dist = np.concatenate([wb + t - l, t - n])
    valid = np.concatenate([wb + t - l <= NSA_WINDOW, (n < dt) & (t - n >= 0)])
    bw = _lane_table(rel_bias, dist, valid, dt)
    return bc, bs, bw


def nsa_sample_constants(past_len, dt):
    n_ch = past_len // CMP_STRIDE
    n_sel = -(-(past_len + dt) // SEL_BLOCK)
    n_sel_pad = -(-n_sel // 8) * 8
    c = np.arange(n_ch)[None, :]
    j0 = np.arange(n_sel_pad)[:, None] * SEL_BLOCK
    c_start, c_end = CMP_STRIDE * c, CMP_STRIDE * c + CMP_LEN - 1
    sel_map = (c_start <= j0 + SEL_BLOCK - 1) & (c_end >= j0) & (c < n_ch - 1) & (j0 < n_sel * SEL_BLOCK)
    lanes = np.arange(N_HEADS * dt)
    grp, t = lanes // (REP * dt), lanes % dt
    gsum = (grp[:, None] == grp[None, :]) & (t[:, None] == t[None, :])
    return (jnp.asarray(sel_map.astype(np.float32), dtype=BF16), jnp.asarray(gsum.astype(np.float32), dtype=BF16))


def nsa_sample_attention(cache_pages, page_idx, win_cache, win_base, new_rows, qbd, gates, cw, k_gain_cmp, tables):
    db, n_pages = page_idx.shape
    page = cache_pages.shape[2]
    past_len = n_pages * page
    lanes = qbd.shape[2]
    dt = lanes // N_HEADS
    wb = win_cache.shape[2]
    w1ab, pe, w2 = cw
    w1l = w1ab.reshape(2, CMP_STRIDE, HEAD_DIM, 2 * CMP_HIDDEN)
    wp = jnp.einsum('sldk,gh->slgdhk', w1l, jnp.eye(2, dtype=w1l.dtype)).reshape(
        2, CMP_STRIDE * 128, 4 * CMP_HIDDEN)
    tok = np.arange(page)
    perm = np.zeros((page, page), np.float32)
    perm[tok, (tok % CMP_STRIDE) * (page // CMP_STRIDE) + tok // CMP_STRIDE] = 1.0
    perm = jnp.asarray(perm, dtype=BF16)
    w2bd = jnp.einsum('shd,gk->sghkd', w2, jnp.eye(KV_HEADS, dtype=w2.dtype)).reshape(
        2, KV_HEADS * CMP_HIDDEN, KV_WIDTH)
    kn = jnp.tile(k_gain_cmp.reshape(1, HEAD_DIM), (1, KV_HEADS))
    sel_map, gsum = nsa_sample_constants(past_len, dt)
    bc, bs, bw = tables
    n_sel = -(-(past_len + dt) // SEL_BLOCK)
    assert past_len % SEL_BLOCK == 0 and dt <= SEL_BLOCK and page % SEL_BLOCK == 0

    def const(a):
        nd = a.ndim
        return pl.BlockSpec(a.shape, lambda i, pt: (0,) * nd)

    page_specs = [pl.BlockSpec((1, cache_pages.shape[1], page), functools.partial(lambda p, i, pt: (pt[i, p], 0, 0), p))
                  for p in range(n_pages)]
    consts = [wp, perm, w1ab, pe, w2bd, kn, sel_map, gsum, bc, bs, bw]
    grid_spec = pltpu.PrefetchScalarGridSpec(
        num_scalar_prefetch=1,
        grid=(db,),
        in_specs=page_specs + [
            pl.BlockSpec((1, win_cache.shape[1], wb), lambda i, pt: (win_base + i, 0, 0)),
            pl.BlockSpec((1,) + new_rows.shape[1:], lambda i, pt: (i, 0, 0, 0)),
            pl.BlockSpec((1,) + qbd.shape[1:], lambda i, pt: (i, 0, 0)),
            pl.BlockSpec((1,) + gates.shape[1:], lambda i, pt: (i, 0, 0)),
        ] + [const(a) for a in consts],
        out_specs=pl.BlockSpec((1, HEAD_DIM, lanes), lambda i, pt: (i, 0, 0)),
        scratch_shapes=[pltpu.VMEM((2 * KV_WIDTH // 128, CMP_STRIDE, past_len // CMP_STRIDE, 128), F32),
                        pltpu.VMEM((past_len, lanes), F32),
                        pltpu.VMEM((sel_map.shape[0], lanes), F32)],
    )
    return pl.pallas_call(
        functools.partial(_nsa_sample_kernel, n_pages=n_pages, page=page, n_top=min(SEL_TOP_N, n_sel),
                          past_len=past_len, dt=dt),
        out_shape=jax.ShapeDtypeStruct((db, HEAD_DIM, lanes), F32),
        grid_spec=grid_spec,
        compiler_params=_params("parallel"),
        name="nsa_sample",
    )(page_idx, *([cache_pages] * n_pages), win_cache, new_rows, qbd, gates, *consts)


def _dil_sample_kernel(kv_ref, new_ref, q_ref, b0_ref, b1_ref, b2_ref, o_ref, s_s, *, los, buf_len):
    lanes = q_ref.shape[3]
    chunk = 128
    ms, dens, accs = [], [], []
    for gi, (bias_ref, lo) in enumerate(zip((b0_ref, b1_ref, b2_ref), los)):
        q = q_ref[0, gi]
        n_chunks = (buf_len - lo) // chunk
        m_run = jnp.full((chunk, lanes), NEG, F32)
        for c in range(n_chunks):
            r0 = lo + c * chunk
            s = (_tn_dot(kv_ref[0, 0:KV_WIDTH, r0:r0 + chunk].astype(BF16), q)
                 + bias_ref[c * chunk:(c + 1) * chunk, :])
            s_s[c * chunk:(c + 1) * chunk, :] = s
            m_run = jnp.maximum(m_run, s)
        s_new = (jnp.dot(new_ref[0, 0], q, preferred_element_type=F32)
                 + bias_ref[n_chunks * chunk:n_chunks * chunk + NEW_ROWS, :])
        m = jnp.maximum(jnp.max(m_run, axis=0, keepdims=True), jnp.max(s_new, axis=0, keepdims=True))
        pn = jnp.exp(s_new - m)
        den = jnp.sum(pn, axis=0, keepdims=True)
        acc = _tn_dot(new_ref[0, 1], pn.astype(BF16))
        for c in range(n_chunks):
            r0 = lo + c * chunk
            pp = jnp.exp(s_s[c * chunk:(c + 1) * chunk, :] - m)
            den = den + jnp.sum(pp, axis=0, keepdims=True)
            acc = acc + jnp.dot(kv_ref[0, KV_WIDTH:2 * KV_WIDTH, r0:r0 + chunk].astype(BF16), pp.astype(BF16),
                                preferred_element_type=F32)
        ms.append(m)
        dens.append(den)
        accs.append(acc)
    mx = functools.reduce(jnp.maximum, ms)
    w = [jnp.exp(m - mx) for m in ms]
    num = functools.reduce(jnp.add, [wg * a for wg, a in zip(w, accs)])
    den = functools.reduce(jnp.add, [wg * d for wg, d in zip(w, dens)])
    o_ref[0] = _group_diagonal(num * (1.0 / den), lanes // KV_HEADS)


def dil_sample_tables(rel_bias, buf_len, dt):
    t = np.arange(dt)[None, :]
    n = np.arange(NEW_ROWS)[:, None]
    tables, los = [], []
    for window, dil in DIL_GROUPS:
        lo = max(0, buf_len - window) // 128 * 128
        s = np.arange(lo, buf_len)[:, None]
        dist = np.concatenate([buf_len + t - s, t - n])
        valid = np.concatenate([np.ones((buf_len - lo, dt), bool), (n < dt) & (t - n >= 0)])
        valid = valid & (dist % dil == 0) & (dist <= window)
        tables.append(_lane_table(rel_bias, dist, valid, dt))
        los.append(lo)
    return tables, tuple(los)


def dil_sample_attention(kv_cache, new_rows, qbd3, tables, los):
    db, width, buf_len = kv_cache.shape
    lanes = qbd3.shape[3]
    assert buf_len % 128 == 0
    tspec = [pl.BlockSpec(tb.shape, lambda i: (0, 0)) for tb in tables]
    return pl.pallas_call(
        functools.partial(_dil_sample_kernel, los=los, buf_len=buf_len),
        out_shape=jax.ShapeDtypeStruct((db, HEAD_DIM, lanes), F32),
        grid=(db,),
        in_specs=[
            pl.BlockSpec((1, width, buf_len), lambda i: (i, 0, 0)),
            pl.BlockSpec((1,) + new_rows.shape[1:], lambda i: (i, 0, 0, 0)),
            pl.BlockSpec((1,) + qbd3.shape[1:], lambda i: (i, 0, 0, 0)),
        ] + tspec,
        out_specs=pl.BlockSpec((1, HEAD_DIM, lanes), lambda i: (i, 0, 0)),
        scratch_shapes=[pltpu.VMEM((buf_len, lanes), F32)],
        compiler_params=_params("parallel"),
        name="dil_sample",
    )(kv_cache, new_rows, qbd3, *tables)


def lane_queries(qT, db, dt):
    q = qT.reshape(KV_HEADS, REP, HEAD_DIM, db, dt).transpose(3, 0, 2, 1, 4).reshape(db, KV_HEADS, HEAD_DIM, REP * dt)
    eye = jnp.eye(KV_HEADS, dtype=q.dtype)
    qbd = q[:, :, :, None, :] * eye[None, :, None, :, None]
    return qbd.reshape(db, KV_WIDTH, KV_HEADS * REP * dt).astype(BF16)


def lanes_to_features(o, db, dt):
    o = o.reshape(db, HEAD_DIM, KV_HEADS, REP, dt).transpose(2, 3, 1, 0, 4)
    return o.reshape(N_HEADS * HEAD_DIM, db * dt)


def new_token_rows(zT_rows, db, dt):
    n = zT_rows.shape[0] // KV_WIDTH
    r = zT_rows.reshape(n, KV_WIDTH, db, dt).transpose(2, 0, 3, 1)
    return jnp.pad(r, ((0, 0), (0, 0), (0, NEW_ROWS - dt), (0, 0))).astype(BF16)


def kernel(x_prompt, x_sample, cache_nsa_kv, cache_win_kv, cache_dil_kv, page_table, rel_bias, a_attn_norm, a_w_in, a_q_norm, a_k_norm, a_cmp_pe, a_cmp_w1, a_cmp_w2, a_w_out, kv_norm, w_kv_shared, k_norm_shared, b_attn_norm, b_w_q, b_q_norm, b_w_out, mlp_norm, mlp_w1, mlp_w2):
    b, s, d = x_prompt.shape
    db, dt, _ = x_sample.shape
    n_a = a_w_in.shape[0]
    n_b = b_w_q.shape[0]
    n_pool, page = cache_nsa_kv.shape[1:3]
    wb = cache_win_kv.shape[2]
    buf_len = cache_dil_kv.shape[1]
    past_len = page_table.shape[1] * page
    assert N_HEADS * dt == 128
    tables = {}
    xpT = x_prompt.reshape(b * s, d).T
    xsT = x_sample.reshape(db * dt, d).T
    cache_pages = cache_nsa_kv.transpose(0, 1, 3, 4, 5, 2).reshape(n_a * n_pool, -1, page)
    win_cache = cache_win_kv.transpose(0, 1, 3, 4, 5, 2).reshape(n_a * db, -1, wb)
    nsa_tables = nsa_sample_tables(rel_bias, past_len, wb, dt)
    rows_p, rows_s, wins_p, wins_s = [], [], [], []
    for l in range(n_a + n_b):
        w1T = mlp_w1[l].T.astype(BF16)
        w2T = mlp_w2[l].T.astype(BF16)
        if l < n_a:
            w_inT = a_w_in[l].T.astype(BF16)
            norms = jnp.stack([a_q_norm[l], a_k_norm[l][1], a_k_norm[l][2]])[..., None]
            cw = compress_weights(a_cmp_pe[l], a_cmp_w1[l], a_cmp_w2[l])
            woT = a_w_out[l].T.astype(BF16)
            zT = norm_proj(xpT, a_attn_norm[l], w_inT, norms, NSA_SEGS)
            oT, rows, win = nsa_attention_prompt(zT, b, s, rel_bias, cw, a_k_norm[l][0], tables=tables)
            rows_p.append(rows)
            wins_p.append(win[:, -min(NSA_WINDOW, s):])
            xpT = outproj_mlp(xpT, oT, woT, mlp_norm[l], w1T, w2T)
            zs = norm_proj(xsT, a_attn_norm[l], w_inT, norms, NSA_SEGS)
            gates = zs[2560:2560 + N_BRANCHES * N_HEADS].reshape(N_BRANCHES, KV_HEADS, REP, db, dt)
            gates = gates.transpose(3, 0, 1, 2, 4).reshape(db, N_BRANCHES, N_HEADS * dt)
            o = nsa_sample_attention(
                cache_pages, page_table + l * n_pool, win_cache, l * db, new_token_rows(zs[1536:2560], db, dt),
                lane_queries(zs[:1024], db, dt), gates, cw, a_k_norm[l][0], nsa_tables)
            kv6 = zs[1024:2560].T.reshape(db, dt, 6, KV_HEADS, HEAD_DIM)
            rows_s.append(kv6[:, :, 0:4])
            wins_s.append(kv6[:, :, 4:6])
            xsT = outproj_mlp(xsT, lanes_to_features(o, db, dt).astype(BF16), woT, mlp_norm[l], w1T, w2T)
        else:
            i = l - n_a
            if i == 0:
                w_kvT = w_kv_shared.T.astype(BF16)
                kn = k_norm_shared.reshape(1, HEAD_DIM, 1)
                kvpT = norm_proj(xpT, kv_norm, w_kvT, kn, KV_SEGS)
                kvp = kvpT.T.reshape(b, s, 2, KV_HEADS, HEAD_DIM)
                kvsT = norm_proj(xsT, kv_norm, w_kvT, kn, KV_SEGS)
                kvs = kvsT.T.reshape(db, dt, 2, KV_HEADS, HEAD_DIM)
                dmax = max(w for w, _ in DIL_GROUPS)
                new_dil_p = kvp[:, -min(dmax, s):]
                new_dil_s = jnp.concatenate([cache_dil_kv, kvs], axis=1)[:, -min(dmax, buf_len + dt):]
                dil_cache = cache_dil_kv.transpose(0, 2, 3, 4, 1).reshape(db, -1, buf_len)
                dil_new = new_token_rows(kvsT, db, dt)
                dil_tables, dil_los = dil_sample_tables(rel_bias, buf_len, dt)
            w_qT = b_w_q[i].T.astype(BF16)
            qn = b_q_norm[i][..., None]
            woT = b_w_out[i].T.astype(BF16)
            qT = norm_proj(xpT, b_attn_norm[i], w_qT, qn, DIL_Q_SEGS)
            oT = dil_attention_prompt(qT, kvpT, b, s, rel_bias, tables=tables)
            xpT = outproj_mlp(xpT, oT, woT, mlp_norm[l], w1T, w2T)
            qs = norm_proj(xsT, b_attn_norm[i], w_qT, qn, DIL_Q_SEGS)
            qbd3 = jnp.stack([lane_queries(qs[1024 * gi:1024 * (gi + 1)], db, dt) for gi in range(len(DIL_GROUPS))], axis=1)
            o = dil_sample_attention(dil_cache, dil_new, qbd3, dil_tables, dil_los)
            xsT = outproj_mlp(xsT, lanes_to_features(o, db, dt).astype(BF16), woT, mlp_norm[l], w1T, w2T)
    return (xpT.T.reshape(b, s, d), xsT.T.reshape(db, dt, d),
            jnp.stack(rows_p), jnp.stack(rows_s), jnp.stack(wins_p),
            jnp.concatenate([cache_win_kv, jnp.stack(wins_s)], axis=2)[:, :, -min(NSA_WINDOW, wb + dt):],
            new_dil_p, new_dil_s)
```

```python
import functools
import math

import numpy as np
import jax
import jax.numpy as jnp
from jax import lax
from jax.experimental import pallas as pl
from jax.experimental.pallas import tpu as pltpu

F32 = jnp.float32
BF16 = jnp.bfloat16

D_MODEL = 1024
N_HEADS = 16
HEAD_DIM = 64
KV_HEADS = 4
REP = N_HEADS // KV_HEADS
GROUP_ROWS = REP * HEAD_DIM
KV_WIDTH = KV_HEADS * HEAD_DIM
D_FF = 4 * D_MODEL
EPS = 1e-6
ATTN_SCALE = HEAD_DIM ** -0.5
N_BUCKETS = 32
MAX_DISTANCE = 2048
CMP_LEN = 32
CMP_STRIDE = 16
CMP_HIDDEN = 2 * HEAD_DIM
SEL_BLOCK = 64
SEL_TOP_N = 16
NSA_WINDOW = 512
N_BRANCHES = 3
NSA_IN = N_HEADS * HEAD_DIM + 6 * KV_HEADS * HEAD_DIM + N_BRANCHES * N_HEADS
DIL_GROUPS = ((128, 1), (512, 4), (2048, 16))
PAGE_SIZE = 128

NEG = -1e30
LOG2E = math.log2(math.e)
VMEM_LIMIT = 56 * 1024 * 1024
BIAS_TABLE_LEN = 2048


def _bucket_of_distance(d):
    max_exact = N_BUCKETS // 2
    d = np.maximum(d, 0)
    ratio = np.log(np.maximum(d, 1).astype(np.float32) / np.float32(max_exact)) / np.float32(
        math.log(MAX_DISTANCE / max_exact))
    large = max_exact + (ratio * np.float32(N_BUCKETS - max_exact)).astype(np.int32)
    return np.where(d < max_exact, d, np.minimum(large, N_BUCKETS - 1)).astype(np.int32)


_BUCKETS = _bucket_of_distance(np.arange(BIAS_TABLE_LEN))
BIAS_CONST_FROM = int(np.argmax(_BUCKETS == N_BUCKETS - 1))
assert np.all(_BUCKETS[BIAS_CONST_FROM:] == N_BUCKETS - 1)


def _params(*sem):
    return pltpu.CompilerParams(dimension_semantics=sem, vmem_limit_bytes=VMEM_LIMIT)


def _token_tile(t, tm):
    tm = min(tm, t)
    assert t % tm == 0, (t, tm)
    return tm


def _tn_dot(a, b):
    return lax.dot_general(a, b, (((0,), (0,)), ((), ())), preferred_element_type=F32)


def _bias_lookup_kernel(bkt_ref, tab_ref, o_ref):
    bkt = bkt_ref[...]
    v = jnp.full(bkt.shape, NEG, F32)
    for bb in range(N_BUCKETS):
        v = jnp.where(bkt == bb, tab_ref[0, bb:bb + 1, :], v)
    o_ref[0] = v


def bias_lookup(dist, valid, tab):
    rows, lanes = dist.shape
    bkt = np.where(valid, _BUCKETS[np.clip(dist, 0, BIAS_TABLE_LEN - 1)], N_BUCKETS).astype(np.int32)
    rb = next(r for r in range(min(rows, 512) // 8 * 8, 0, -8) if rows % r == 0)
    nh = tab.shape[0]
    return pl.pallas_call(
        _bias_lookup_kernel,
        out_shape=jax.ShapeDtypeStruct((nh, rows, lanes), F32),
        grid=(rows // rb, nh),
        in_specs=[pl.BlockSpec((rb, lanes), lambda r, h: (r, 0)),
                  pl.BlockSpec((1, N_BUCKETS, lanes), lambda r, h: (h, 0, 0))],
        out_specs=pl.BlockSpec((1, rb, lanes), lambda r, h: (h, r, 0)),
        compiler_params=_params("parallel", "parallel"),
        name="bias_lookup",
    )(jnp.asarray(bkt), tab)


def _cached(tables, key, build):
    if tables is None:
        return build()
    if key not in tables:
        tables[key] = build()
    return tables[key]


def head_rows(rel_bias, lanes):
    return jnp.broadcast_to((rel_bias.T * LOG2E)[:, :, None], (N_HEADS, N_BUCKETS, lanes))


def _proj_kernel(x_ref, g_ref, w_ref, n_ref, o_ref, *, segs):
    x = x_ref[...]
    ms = jnp.mean(x * x, axis=0, keepdims=True)
    xn = (x * lax.rsqrt(ms + EPS) * g_ref[...]).astype(BF16)
    tm = x.shape[1]
    for r0, nr, kind, ni, scale in segs:
        z = jnp.dot(w_ref[r0:r0 + nr, :], xn, preferred_element_type=F32)
        if kind == "norm":
            z3 = z.reshape(nr // HEAD_DIM, HEAD_DIM, tm)
            hs = jnp.mean(z3 * z3, axis=1, keepdims=True)
            z = (z3 * lax.rsqrt(hs + EPS) * n_ref[ni][None] * scale).reshape(nr, tm)
        elif kind == "sigmoid":
            z = 1.0 / (1.0 + jnp.exp(-z))
        o_ref[r0:r0 + nr, :] = z


def norm_proj(xT, gain, wT, norms, segs, tm=512):
    d, t = xT.shape
    tm = _token_tile(t, tm)
    n = wT.shape[0]
    return pl.pallas_call(
        functools.partial(_proj_kernel, segs=tuple(segs)),
        out_shape=jax.ShapeDtypeStruct((n, t), F32),
        grid=(t // tm,),
        in_specs=[
            pl.BlockSpec((d, tm), lambda i: (0, i)),
            pl.BlockSpec((d, 1), lambda i: (0, 0)),
            pl.BlockSpec((n, d), lambda i: (0, 0)),
            pl.BlockSpec(norms.shape, lambda i: (0, 0, 0)),
        ],
        out_specs=pl.BlockSpec((n, tm), lambda i: (0, i)),
        compiler_params=_params("parallel"),
        name="norm_proj",
    )(xT, gain.reshape(d, 1), wT, norms)


def _nsa_proj_kernel(x_ref, g_ref, w_ref, n_ref, q_ref, rows_ref, gate_ref, ks_ref, vs_ref, kw_ref, vw_ref, *, t):
    x = x_ref[...]
    ms = jnp.mean(x * x, axis=0, keepdims=True)
    xn = (x * lax.rsqrt(ms + EPS) * g_ref[...]).astype(BF16)
    tm = x.shape[1]

    def proj(r0, nr):
        return jnp.dot(w_ref[r0:r0 + nr, :], xn, preferred_element_type=F32)

    def head_norm(z, ni, scale):
        z3 = z.reshape(z.shape[0] // HEAD_DIM, HEAD_DIM, tm)
        hs = jnp.mean(z3 * z3, axis=1, keepdims=True)
        return (z3 * lax.rsqrt(hs + EPS) * n_ref[ni][None] * scale).reshape(z.shape)

    half = N_HEADS * HEAD_DIM // 2
    for c in range(2):
        q_ref[c * half:(c + 1) * half, :] = head_norm(proj(c * half, half), 0, ATTN_SCALE * LOG2E).astype(BF16)
    base = N_HEADS * HEAD_DIM
    rows_ref[0:2 * KV_WIDTH, :] = proj(base, 2 * KV_WIDTH)
    kv = [head_norm(proj(base + 2 * KV_WIDTH, KV_WIDTH), 1, 1.0), proj(base + 3 * KV_WIDTH, KV_WIDTH),
          head_norm(proj(base + 4 * KV_WIDTH, KV_WIDTH), 2, 1.0), proj(base + 5 * KV_WIDTH, KV_WIDTH)]
    for idx, z in enumerate(kv):
        rows_ref[(2 + idx) * KV_WIDTH:(3 + idx) * KV_WIDTH, :] = z
    gate_ref[...] = 1.0 / (1.0 + jnp.exp(-proj(base + 6 * KV_WIDTH, N_BRANCHES * N_HEADS)))
    r = lax.broadcasted_iota(jnp.int32, (SEL_ROWS, t), 0)
    col = lax.broadcasted_iota(jnp.int32, (SEL_ROWS, t), 1)
    block_rows = jnp.where(col // SEL_BLOCK == r, 1.0, 0.0).astype(BF16)
    ones_rows = jnp.where(r == 0, 1.0, 0.0).astype(BF16)
    for (ref, extra), z in zip(((ks_ref, block_rows), (vs_ref, ones_rows), (kw_ref, None), (vw_ref, ones_rows)), kv):
        for j in range(tm // t):
            for g in range(KV_HEADS):
                ref[0, g, j, 0:HEAD_DIM, :] = z[g * HEAD_DIM:(g + 1) * HEAD_DIM, j * t:(j + 1) * t].astype(BF16)
                if extra is not None:
                    ref[0, g, j, HEAD_DIM:HEAD_DIM + SEL_ROWS, :] = extra


def nsa_proj_prompt(xT, gain, wT, norms, b, s, t=256, tm=512):
    d, tt = xT.shape
    nps = s // tm
    assert tt == b * s and s % tm == 0 and tm % t == 0
    n = wT.shape[0]
    col = lambda rows, dt: (jax.ShapeDtypeStruct((rows, tt), dt), pl.BlockSpec((rows, tm), lambda i: (0, i)))
    tile = lambda rows: (jax.ShapeDtypeStruct((b, KV_HEADS, s // t, rows, t), BF16),
                         pl.BlockSpec((1, KV_HEADS, tm // t, rows, t), lambda i: (i // nps, 0, i % nps, 0, 0)))
    outs = [col(N_HEADS * HEAD_DIM, BF16), col(6 * KV_WIDTH, F32), col(N_BRANCHES * N_HEADS, F32),
            tile(HEAD_DIM + SEL_ROWS), tile(HEAD_DIM + SEL_ROWS), tile(HEAD_DIM), tile(HEAD_DIM + SEL_ROWS)]
    return pl.pallas_call(
        functools.partial(_nsa_proj_kernel, t=t),
        out_shape=tuple(o[0] for o in outs),
        grid=(tt // tm,),
        in_specs=[
            pl.BlockSpec((d, tm), lambda i: (0, i)),
            pl.BlockSpec((d, 1), lambda i: (0, 0)),
            pl.BlockSpec((n, d), lambda i: (0, 0)),
            pl.BlockSpec(norms.shape, lambda i: (0, 0, 0)),
        ],
        out_specs=tuple(o[1] for o in outs),
        compiler_params=_params("parallel"),
        name="nsa_proj",
    )(xT, gain.reshape(d, 1), wT, norms)


def _outmlp_kernel(x_ref, o_ref, wo_ref, g_ref, w1_ref, w2_ref, y_ref, x1_s, xn_s, acc_s):
    f = pl.program_id(1)

    @pl.when(f == 0)
    def _():
        x1 = x_ref[...] + jnp.dot(wo_ref[...], o_ref[...], preferred_element_type=F32)
        x1_s[...] = x1
        ms = jnp.mean(x1 * x1, axis=0, keepdims=True)
        xn_s[...] = (x1 * lax.rsqrt(ms + EPS) * g_ref[...]).astype(BF16)
        acc_s[...] = jnp.zeros_like(acc_s)

    h = jnp.maximum(jnp.dot(w1_ref[...], xn_s[...], preferred_element_type=F32), 0.0)
    acc_s[...] += jnp.dot(w2_ref[...], (h * h).astype(BF16), preferred_element_type=F32)

    @pl.when(f == pl.num_programs(1) - 1)
    def _():
        y_ref[...] = x1_s[...] + acc_s[...]


def outproj_mlp(xT, oT, woT, gain, w1T, w2T, tm=1024, tf=1024):
    d, t = xT.shape
    tm = _token_tile(t, tm)
    dff = w1T.shape[0]
    return pl.pallas_call(
        _outmlp_kernel,
        out_shape=jax.ShapeDtypeStruct((d, t), F32),
        grid=(t // tm, dff // tf),
        in_specs=[
            pl.BlockSpec((d, tm), lambda i, f: (0, i)),
            pl.BlockSpec((d, tm), lambda i, f: (0, i)),
            pl.BlockSpec((d, d), lambda i, f: (0, 0)),
            pl.BlockSpec((d, 1), lambda i, f: (0, 0)),
            pl.BlockSpec((tf, d), lambda i, f: (f, 0)),
            pl.BlockSpec((d, tf), lambda i, f: (0, f)),
        ],
        out_specs=pl.BlockSpec((d, tm), lambda i, f: (0, i)),
        scratch_shapes=[pltpu.VMEM((d, tm), F32), pltpu.VMEM((d, tm), BF16), pltpu.VMEM((d, tm), F32)],
        compiler_params=_params("parallel", "arbitrary"),
        name="outproj_mlp",
    )(xT, oT, woT, gain.reshape(d, 1), w1T, w2T)


def _compress_kernel(ch_ref, w1_ref, pe_ref, w2_ref, kn_ref, o_ref):
    slot = pl.program_id(0)
    bn, c, _ = ch_ref.shape[1:]
    w1 = w1_ref[0]
    ab = jnp.dot(ch_ref[0].reshape(bn * c, CMP_STRIDE * HEAD_DIM), w1, preferred_element_type=F32)
    pe_a = jnp.dot(pe_ref[0, 0], w1[:, :CMP_HIDDEN], preferred_element_type=F32)[0:1]
    pe_b = jnp.dot(pe_ref[0, 1], w1[:, CMP_HIDDEN:], preferred_element_type=F32)[0:1]
    nxt = pltpu.roll(ab[:, CMP_HIDDEN:], bn * c - 1, 0)
    hid = ab[:, :CMP_HIDDEN] + nxt + (pe_a + pe_b)
    act = hid / (1.0 + jnp.exp(-hid))
    out = jnp.dot(act.astype(BF16), w2_ref[0], preferred_element_type=F32)
    ms = jnp.mean(out * out, axis=-1, keepdims=True)
    normed = out * lax.rsqrt(ms + EPS) * kn_ref[...]
    o_ref[0] = jnp.where(slot == 0, normed, out).reshape(bn, c, HEAD_DIM)


def compress(ch, w1ab, pe, w2, k_gain, bn):
    _, nb, c, w = ch.shape
    return pl.pallas_call(
        _compress_kernel,
        out_shape=jax.ShapeDtypeStruct((2, nb, c, HEAD_DIM), F32),
        grid=(2, nb // bn),
        in_specs=[
            pl.BlockSpec((1, bn, c, w), lambda s, i: (s, i, 0, 0)),
            pl.BlockSpec((1, w, 2 * CMP_HIDDEN), lambda s, i: (s, 0, 0)),
            pl.BlockSpec((1, 2, 16, w), lambda s, i: (s, 0, 0, 0)),
            pl.BlockSpec((1, CMP_HIDDEN, HEAD_DIM), lambda s, i: (s, 0, 0)),
            pl.BlockSpec((1, HEAD_DIM), lambda s, i: (0, 0)),
        ],
        out_specs=pl.BlockSpec((1, bn, c, HEAD_DIM), lambda s, i: (s, i, 0, 0)),
        compiler_params=_params("parallel", "parallel"),
        name="nsa_compress",
    )(ch, w1ab, pe, w2, k_gain.reshape(1, HEAD_DIM))


def compress_weights(cmp_pe, cmp_w1, cmp_w2):
    half = CMP_STRIDE * HEAD_DIM
    w1 = cmp_w1.reshape(2, 2, half, CMP_HIDDEN)
    w1ab = jnp.concatenate([w1[:, 0], w1[:, 1]], axis=-1).astype(BF16)
    pe = jnp.broadcast_to(cmp_pe.reshape(2, 2, 1, half), (2, 2, 16, half)).astype(BF16)
    return w1ab, pe, cmp_w2.astype(BF16)


CMP_CHUNK = 128


def _cmp_topk_kernel(q_ref, ck_ref, cv_ref, map_ref, bias_ref, oc_ref, sel_ref, v_s, s_s, cnt_s, *,
                     tq, n_top, k_const):
    i = pl.program_id(2)
    c = ck_ref.shape[2]
    n_sel = map_ref.shape[0]
    n_chunks = c // CMP_CHUNK
    ms = []
    for r in range(REP):
        q = q_ref[r * HEAD_DIM:(r + 1) * HEAD_DIM, :]
        cm = None
        for jc in range(n_chunks):
            k0 = i * (tq // CMP_STRIDE) + jc * CMP_CHUNK
            start = pl.multiple_of(jnp.minimum(k0, k_const), 8)
            s = (jnp.dot(ck_ref[0, 0, jc * CMP_CHUNK:(jc + 1) * CMP_CHUNK, :], q, preferred_element_type=F32)
                 + bias_ref[r, pl.ds(start, CMP_CHUNK), :])
            s_s[r, jc * CMP_CHUNK:(jc + 1) * CMP_CHUNK, :] = s
            mx = jnp.max(s, axis=0, keepdims=True)
            cm = mx if cm is None else jnp.maximum(cm, mx)
        ms.append(jnp.where(cm < 0.5 * NEG, 0.0, cm))
    psum = None
    for r in range(REP):
        p = jnp.exp2(s_s[r] - ms[r])
        p = p * (1.0 / jnp.maximum(jnp.sum(p, axis=0, keepdims=True), 1e-30))
        oc_ref[r * HEAD_DIM:(r + 1) * HEAD_DIM, :] = jnp.dot(
            cv_ref[0, 0], p.astype(BF16), preferred_element_type=F32)
        psum = p if psum is None else psum + p
    hi = psum.astype(BF16)
    lo = (psum - hi.astype(F32)).astype(BF16)
    imp = (jnp.dot(map_ref[...], hi, preferred_element_type=F32)
           + jnp.dot(map_ref[...], lo, preferred_element_type=F32))
    qpos = i * tq + lax.broadcasted_iota(jnp.int32, (n_sel, tq), 1)
    j = lax.broadcasted_iota(jnp.int32, (n_sel, tq), 0)
    cur = qpos // SEL_BLOCK
    forced = (j == 0) | (j == cur) | (j == cur - 1)
    future = j * SEL_BLOCK > qpos
    v = jnp.where(forced, jnp.inf, jnp.where(future, -jnp.inf, imp))
    v_s[...] = v
    groups = n_sel // 8
    row = lax.broadcasted_iota(jnp.int32, (8, 128), 0)
    cnt_s[...] = jnp.zeros(cnt_s.shape, F32)
    newest = (i * tq + tq - 1) // SEL_BLOCK
    for ga in range(groups):
        @pl.when(8 * ga <= newest)
        def _(ga=ga):
            for lt in range(tq // 128):
                lanes = slice(lt * 128, (lt + 1) * 128)
                vg = [v_s[8 * gi:8 * gi + 8, lanes] for gi in range(groups)]
                cnt = [cnt_s[8 * gi:8 * gi + 8, lanes] for gi in range(groups)]
                for a in range(8 * ga, 8 * ga + 8):
                    va = v_s[a:a + 1, lanes]
                    for gi in range(groups):
                        if 8 * gi > a:
                            ahead = va >= vg[gi]
                        elif 8 * gi + 7 < a:
                            ahead = va > vg[gi]
                        else:
                            ahead = (va > vg[gi]) | ((va == vg[gi]) & (row > a - 8 * gi))
                        cnt[gi] = cnt[gi] + jnp.where(ahead, 1.0, 0.0)
                for gi in range(groups):
                    cnt_s[8 * gi:8 * gi + 8, lanes] = cnt[gi]

    for lt in range(tq // 128):
        lanes = slice(lt * 128, (lt + 1) * 128)
        bpt = n_sel // sel_ref.shape[2]
        for gi in range(groups):
            mask = jnp.where(cnt_s[8 * gi:8 * gi + 8, lanes] < n_top, 0.0, NEG)
            for hh in range(8 // bpt):
                part = mask if hh == 0 else pltpu.roll(mask, 8 - hh * bpt, 0)
                part = jnp.where(row < bpt, part, 0.0)
                sel_ref[0, 0, gi * (8 // bpt) + hh, :, lanes] = jnp.concatenate(
                    [part, jnp.zeros((SEL_ROWS - 8, 128), F32)], axis=0).astype(BF16)


def cmp_strip(rel_bias, c, tq, rows):
    k = np.arange(rows)[:, None]
    tl = np.arange(tq)[None, :]
    d = CMP_STRIDE * (k - (c - 1)) + tl - (CMP_LEN - 1)
    return bias_lookup(d, d >= 0, head_rows(rel_bias, tq))


def cmp_topk(qT, ckr, cvTr, rel_bias, b, s, tq=256, key_tile=256, tables=None):
    c = s // CMP_STRIDE
    n_sel = s // SEL_BLOCK
    nkt = s // key_tile
    assert 8 % (key_tile // SEL_BLOCK) == 0 and n_sel % 8 == 0
    n_top = min(SEL_TOP_N, n_sel)
    nq = s // tq
    k_const = (c - 1) + -(-(BIAS_CONST_FROM + CMP_LEN - 1) // CMP_STRIDE)
    k_const = -(-k_const // 8) * 8
    strip = _cached(tables, ("cmp", c, tq), lambda: cmp_strip(rel_bias, c, tq, k_const + CMP_CHUNK))
    cc = np.arange(c)[::-1]
    c_end = cc * CMP_STRIDE + CMP_LEN - 1
    c_start = c_end - CMP_LEN + 1
    j0 = np.arange(n_sel)[:, None] * SEL_BLOCK
    sel_map = ((c_start[None] <= j0 + SEL_BLOCK - 1) & (c_end[None] >= j0) & (cc[None] < c - 1))
    sel_map = jnp.asarray(sel_map.astype(np.float32), dtype=BF16)
    return pl.pallas_call(
        functools.partial(_cmp_topk_kernel, tq=tq, n_top=n_top, k_const=k_const),
        out_shape=(jax.ShapeDtypeStruct((N_HEADS * HEAD_DIM, b * s), F32),
                   jax.ShapeDtypeStruct((b, KV_HEADS, nkt, SEL_ROWS, s), BF16)),
        grid=(KV_HEADS, b, nq),
        in_specs=[
            pl.BlockSpec((GROUP_ROWS, tq), lambda g, n, i: (g, n * nq + i)),
            pl.BlockSpec((1, 1, c, HEAD_DIM), lambda g, n, i: (n, g, 0, 0)),
            pl.BlockSpec((1, 1, HEAD_DIM, c), lambda g, n, i: (n, g, 0, 0)),
            pl.BlockSpec((n_sel, c), lambda g, n, i: (0, 0)),
            pl.BlockSpec((REP, strip.shape[1], tq), lambda g, n, i: (g, 0, 0)),
        ],
        out_specs=(pl.BlockSpec((GROUP_ROWS, tq), lambda g, n, i: (g, n * nq + i)),
                   pl.BlockSpec((1, 1, nkt, SEL_ROWS, tq), lambda g, n, i: (n, g, 0, 0, i))),
        scratch_shapes=[pltpu.VMEM((n_sel, tq), F32), pltpu.VMEM((REP, c, tq), F32), pltpu.VMEM((n_sel, tq), F32)],
        compiler_params=_params("parallel", "parallel", "parallel"),
        name="nsa_cmp_topk",
    )(qT, ckr, cvTr, sel_map, strip)


SEL_ROWS = 16


def _flash_kernel(*refs, tq, n_tiles_max, n_delta, use_sel):
    nh = REP
    if use_sel:
        q_ref, k_ref, v_ref, bias_ref, sel_ref = refs[:5]
        outs = refs[5:]
    else:
        q_ref, k_ref, v_ref, bias_ref = refs[:4]
        sel_ref = None
        outs = refs[4:]
    o_ref = outs[0]
    accs, s_bufs = outs[1:1 + nh], outs[1 + nh:]
    i = pl.program_id(2)
    n_tiles = jnp.minimum(i + 1, n_tiles_max)
    for acc in accs:
        acc[...] = jnp.zeros(acc.shape, F32)

    def scores(jt, s_buf):
        kt = jnp.maximum(i - jt, 0)
        kT = k_ref[0, 0, kt]
        tile = jnp.where(jt < n_tiles, jnp.minimum(jt, n_delta - 1), n_delta)
        cms = []
        for r in range(nh):
            q = q_ref[r * HEAD_DIM:(r + 1) * HEAD_DIM, :]
            if use_sel:
                q = jnp.concatenate([q, sel_ref[0, 0, kt]], axis=0)
            s = _tn_dot(kT, q) + bias_ref[r, tile]
            s_buf[r] = s
            cms.append(jnp.max(s, axis=0, keepdims=True))
        return tuple(cms)

    def accumulate(jt, s_buf, cms, ms):
        v = v_ref[0, 0, jnp.maximum(i - jt, 0)]
        new_m = []
        for r in range(nh):
            m_new = jnp.maximum(ms[r], cms[r])
            alpha = jnp.exp2(ms[r] - m_new)
            p = jnp.exp2((s_buf[r] - m_new).astype(BF16))
            accs[r][...] = alpha * accs[r][...] + jnp.dot(v, p, preferred_element_type=F32)
            new_m.append(m_new)
        return tuple(new_m)

    def pair(jj, carry):
        ms, cm0 = carry
        cm1 = scores(2 * jj + 1, s_bufs[1])
        ms = accumulate(2 * jj, s_bufs[0], cm0, ms)
        cm0 = scores(2 * jj + 2, s_bufs[0])
        ms = accumulate(2 * jj + 1, s_bufs[1], cm1, ms)
        return ms, cm0

    init = (tuple(jnp.full((1, tq), NEG, F32) for _ in range(nh)), scores(0, s_bufs[0]))
    lax.fori_loop(0, (n_tiles + 1) // 2, pair, init)
    for r in range(nh):
        o_ref[r * HEAD_DIM:(r + 1) * HEAD_DIM, :] = (
            accs[r][0:HEAD_DIM, :] * (1.0 / accs[r][HEAD_DIM:HEAD_DIM + 1, :]))


def flash_bias_tiles(rel_bias, t, n_delta, dil, window):
    j = np.arange(n_delta + 1)[:, None, None]
    sl = np.arange(t)[None, :, None]
    tl = np.arange(t)[None, None, :]
    d = j * t + tl - sl
    ok = (d >= 0) & (j < n_delta)
    if window is not None:
        ok = ok & (d <= window)
    tiles = bias_lookup((d * dil).reshape(-1, t), ok.reshape(-1, t), head_rows(rel_bias, t))
    return tiles.reshape(N_HEADS, n_delta + 1, t, t)


def flash_attention(qT, kT, vT, rel_bias, nseq, seqlen, sel=None, window=None, dil=1, tables=None):
    nkt, tk = kT.shape[2], kT.shape[4]
    tq = tk
    assert seqlen == nkt * tk
    nq = nkt
    if window is None:
        n_delta = -(-(BIAS_CONST_FROM + tk - 1) // tk) + 1
        n_tiles_max = nkt
    else:
        n_tiles_max = (window + tk - 1) // tk + 1
        n_delta = n_tiles_max
    tiles = _cached(tables, ("flash", tk, n_delta, dil, window),
                    lambda: flash_bias_tiles(rel_bias, tk, n_delta, dil, window))
    h_rows = qT.shape[0]
    use_sel = sel is not None
    in_specs = [
        pl.BlockSpec((GROUP_ROWS, tq), lambda g, n, i: (g, n * nq + i)),
        pl.BlockSpec((1, 1) + kT.shape[2:], lambda g, n, i: (n, g, 0, 0, 0)),
        pl.BlockSpec((1, 1) + vT.shape[2:], lambda g, n, i: (n, g, 0, 0, 0)),
        pl.BlockSpec((REP, n_delta + 1, tk, tq), lambda g, n, i: (g, 0, 0, 0)),
    ]
    args = [qT, kT, vT, tiles]
    if use_sel:
        in_specs.append(pl.BlockSpec((1, 1) + sel.shape[2:4] + (tq,), lambda g, n, i: (n, g, 0, 0, i)))
        args.append(sel)
    return pl.pallas_call(
        functools.partial(_flash_kernel, tq=tq, n_tiles_max=n_tiles_max, n_delta=n_delta, use_sel=use_sel),
        out_shape=jax.ShapeDtypeStruct((h_rows, nseq * seqlen), F32),
        grid=(KV_HEADS, nseq, nq),
        in_specs=in_specs,
        out_specs=pl.BlockSpec((GROUP_ROWS, tq), lambda g, n, i: (g, n * nq + i)),
        scratch_shapes=([pltpu.VMEM((vT.shape[3], tq), F32) for _ in range(REP)]
                        + [pltpu.VMEM((REP, tk, tq), F32) for _ in range(2)]),
        compiler_params=_params("parallel", "parallel", "parallel"),
        name="flash_sel" if use_sel else ("flash_band" if window is not None else "flash_causal"),
    )(*args)


def _band_kernel(q_ref, k_ref, v_ref, bias_ref, o_ref, mo_ref, lo_ref, *, t, n_tiles, sub):
    step = pl.program_id(2)
    for u in range(sub):
        i = step * sub + u
        cols = slice(u * t, (u + 1) * t)
        kts = [jnp.maximum(i - jt, 0) for jt in range(n_tiles)]
        tiles = [jnp.where(jt <= i, jt, n_tiles) for jt in range(n_tiles)]
        kTs = [k_ref[0, 0, kt] for kt in kts]
        vTs = [v_ref[0, 0, kt] for kt in kts]
        for r in range(REP):
            rows = slice(r * HEAD_DIM, (r + 1) * HEAD_DIM)
            q = q_ref[rows, cols]
            s = [_tn_dot(kTs[jt], q) + bias_ref[r, tiles[jt]] for jt in range(n_tiles)]
            m = functools.reduce(jnp.maximum, [jnp.max(x, axis=0, keepdims=True) for x in s])
            p = [jnp.exp2(x - m) for x in s]
            den = functools.reduce(jnp.add, [jnp.sum(x, axis=0, keepdims=True) for x in p])
            acc = functools.reduce(jnp.add, [jnp.dot(vTs[jt], p[jt].astype(BF16), preferred_element_type=F32)
                                             for jt in range(n_tiles)])
            o_ref[rows, cols] = acc
            mo_ref[0, r:r + 1, cols] = m
            lo_ref[0, r:r + 1, cols] = den


def band_attention(qT, kT, vT, rel_bias, nseq, seqlen, window, dil, sub=8, tables=None):
    nkt, t = kT.shape[2], kT.shape[4]
    n_tiles = (window + t - 1) // t + 1
    sub = min(sub, nkt)
    assert seqlen == nkt * t and nkt % sub == 0
    nq = nkt // sub
    tiles = _cached(tables, ("flash", t, n_tiles, dil, window),
                    lambda: flash_bias_tiles(rel_bias, t, n_tiles, dil, window))
    o_spec = pl.BlockSpec((GROUP_ROWS, sub * t), lambda g, n, i: (g, n * nq + i))
    st_spec = pl.BlockSpec((1, REP, sub * t), lambda g, n, i: (g, 0, n * nq + i))
    st_shape = jax.ShapeDtypeStruct((KV_HEADS, REP, nseq * seqlen), F32)
    return pl.pallas_call(
        functools.partial(_band_kernel, t=t, n_tiles=n_tiles, sub=sub),
        out_shape=(jax.ShapeDtypeStruct((qT.shape[0], nseq * seqlen), F32), st_shape, st_shape),
        grid=(KV_HEADS, nseq, nq),
        in_specs=[
            pl.BlockSpec((GROUP_ROWS, sub * t), lambda g, n, i: (g, n * nq + i)),
            pl.BlockSpec((1, 1) + kT.shape[2:], lambda g, n, i: (n, g, 0, 0, 0)),
            pl.BlockSpec((1, 1) + vT.shape[2:], lambda g, n, i: (n, g, 0, 0, 0)),
            pl.BlockSpec((REP, n_tiles + 1, t, t), lambda g, n, i: (g, 0, 0, 0)),
        ],
        out_specs=(o_spec, st_spec, st_spec),
        compiler_params=_params("parallel", "parallel", "parallel"),
        name="band_attention",
    )(qT, kT, vT, tiles)


def _nsa_combine_kernel(oc_ref, os_ref, ow_ref, g_ref, o_ref):
    for h in range(N_HEADS):
        rows = slice(h * HEAD_DIM, (h + 1) * HEAD_DIM)
        o = (g_ref[h:h + 1, :] * oc_ref[rows, :]
             + g_ref[N_HEADS + h:N_HEADS + h + 1, :] * os_ref[rows, :]
             + g_ref[2 * N_HEADS + h:2 * N_HEADS + h + 1, :] * ow_ref[rows, :])
        o_ref[rows, :] = o.astype(BF16)


def nsa_combine(ocT, osT, owT, gatesT, tm=512):
    d, t = ocT.shape
    tm = _token_tile(t, tm)
    big = pl.BlockSpec((d, tm), lambda i: (0, i))
    return pl.pallas_call(
        _nsa_combine_kernel,
        out_shape=jax.ShapeDtypeStruct((d, t), BF16),
        grid=(t // tm,),
        in_specs=[big, big, big, pl.BlockSpec((N_BRANCHES * N_HEADS, tm), lambda i: (0, i))],
        out_specs=big,
        compiler_params=_params("parallel"),
        name="nsa_combine",
    )(ocT, osT, owT, gatesT)


def _dil_merge_kernel(*refs):
    ng = (len(refs) - 3)
    acc_refs, (m_ref, l_ref, o_ref) = refs[:ng], refs[ng:]
    mx = functools.reduce(jnp.maximum, [m_ref[g] for g in range(ng)])
    w = [jnp.exp2(m_ref[g] - mx) for g in range(ng)]
    den = functools.reduce(jnp.add, [w[g] * l_ref[g] for g in range(ng)])
    inv = 1.0 / den
    for h in range(N_HEADS):
        rows = slice(h * HEAD_DIM, (h + 1) * HEAD_DIM)
        num = functools.reduce(jnp.add, [w[g][h:h + 1, :] * acc_refs[g][rows, :] for g in range(ng)])
        o_ref[rows, :] = (num * inv[h:h + 1, :]).astype(BF16)


def dil_merge(accs, m, l, tm=512):
    ng = len(accs)
    d, t = accs[0].shape
    tm = _token_tile(t, tm)
    st = pl.BlockSpec((ng, N_HEADS, tm), lambda i: (0, 0, i))
    big = pl.BlockSpec((d, tm), lambda i: (0, i))
    return pl.pallas_call(
        _dil_merge_kernel,
        out_shape=jax.ShapeDtypeStruct((d, t), BF16),
        grid=(t // tm,),
        in_specs=[big] * ng + [st, st],
        out_specs=big,
        compiler_params=_params("parallel"),
        name="dil_merge",
    )(*accs, m, l)


NSA_SEGS = (
    (0, 512, "norm", 0, ATTN_SCALE), (512, 512, "norm", 0, ATTN_SCALE),
    (1024, 512, "plain", 0, 1.0),
    (1536, 256, "norm", 1, 1.0),
    (1792, 256, "plain", 0, 1.0),
    (2048, 256, "norm", 2, 1.0),
    (2304, 256, "plain", 0, 1.0),
    (2560, N_BRANCHES * N_HEADS, "sigmoid", 0, 1.0),
)
KV_SEGS = ((0, 256, "norm", 0, 1.0), (256, 256, "plain", 0, 1.0))
DIL_Q_SEGS = tuple((512 * j, 512, "norm", j // 2, ATTN_SCALE) for j in range(6))


def kv_tiles(xT, nseq, seqlen, t):
    return xT.reshape(KV_HEADS, HEAD_DIM, nseq, seqlen // t, t).transpose(2, 0, 3, 1, 4).astype(BF16)


def nsa_attention_prompt(proj, b, s, rel_bias, cw, k_gain_cmp, tables=None):
    qT, rowsT, gatesT, ksT, vsT, kwT, vwT = proj
    t = ksT.shape[4]
    tok = rowsT.T
    rows = tok[:, :4 * KV_WIDTH].reshape(b, s, 4, KV_HEADS, HEAD_DIM)
    win = tok[:, 4 * KV_WIDTH:].reshape(b, s, 2, KV_HEADS, HEAD_DIM)
    c = s // CMP_STRIDE
    ch = rows[:, :, 0:2].transpose(2, 0, 3, 1, 4).reshape(2, b * KV_HEADS, c, CMP_STRIDE * HEAD_DIM).astype(BF16)
    cmp = compress(ch, *cw, k_gain_cmp, bn=4)
    cmp = jnp.flip(cmp, axis=2).reshape(2, b, KV_HEADS, c, HEAD_DIM)
    ckr = cmp[0].astype(BF16)
    cvTr = cmp[1].transpose(0, 1, 3, 2).astype(BF16)
    ocT, mask = cmp_topk(qT, ckr, cvTr, rel_bias, b, s, key_tile=t, tables=tables)
    osT = flash_attention(qT, ksT, vsT, rel_bias, b, s, sel=mask, tables=tables)
    owT = flash_attention(qT, kwT, vwT, rel_bias, b, s, window=NSA_WINDOW, tables=tables)
    oT = nsa_combine(ocT, osT, owT, gatesT)
    return oT, rows, win


def dil_attention_prompt(qallT, kvT, b, s, rel_bias, t=128, tables=None):
    accs, ms, ls = [], [], []
    for gi, (window, dil) in enumerate(DIL_GROUPS):
        length = s // dil

        def split(x):
            r = x.shape[0]
            return x.reshape(r, b, length, dil).transpose(0, 1, 3, 2).reshape(r, b * s)

        def merge(x):
            r = x.shape[0]
            return x.reshape(r, b, dil, length).transpose(0, 1, 3, 2).reshape(r, b * s)

        qg = split((qallT[1024 * gi:1024 * (gi + 1)] * LOG2E).astype(BF16))
        kvg = split(kvT.astype(BF16))
        acc, m, l = band_attention(qg, kv_tiles(kvg[:KV_WIDTH], b * dil, length, t),
                                   kv_tiles(kvg[KV_WIDTH:], b * dil, length, t), rel_bias, b * dil, length,
                                   window // dil, dil, tables=tables)

        accs.append(merge(acc))
        ms.append(merge(m.reshape(N_HEADS, b * s)))
        ls.append(merge(l.reshape(N_HEADS, b * s)))
    return dil_merge(accs, jnp.stack(ms), jnp.stack(ls))


NEW_ROWS = 16


def _group_diagonal(oT, lanes_per_group):
    grp = lax.broadcasted_iota(jnp.int32, (HEAD_DIM, oT.shape[1]), 1) // lanes_per_group
    out = jnp.zeros((HEAD_DIM, oT.shape[1]), F32)
    for g in range(KV_HEADS):
        out = jnp.where(grp == g, oT[g * HEAD_DIM:(g + 1) * HEAD_DIM, :], out)
    return out


def _nsa_sample_kernel(pt_ref, *refs, n_pages, page, n_top, past_len, dt):
    del pt_ref
    pages = refs[:n_pages]
    (win_ref, new_ref, q_ref, gate_ref, wp_ref, perm_ref, w1f_ref, pe_ref, w2_ref, kn_ref, map_ref, gs_ref,
     bc_ref, bs_ref, bw_ref, o_ref, x_s, s_s, v_s) = refs[n_pages:]
    q = q_ref[0]
    lanes = q.shape[1]
    n_ch = past_len // CMP_STRIDE

    pairs = KV_WIDTH // 128
    cpp = page // CMP_STRIDE
    for p in range(n_pages):
        for jj in range(2 * pairs):
            f = pages[p][0, 128 * jj:128 * (jj + 1), :].astype(BF16)
            t = jnp.dot(f, perm_ref[...], preferred_element_type=F32).T
            x_s[jj, :, cpp * p:cpp * (p + 1), :] = t.reshape(CMP_STRIDE, cpp, 128)
    comp = []
    for slot in range(2):
        ab = []
        for hp in range(pairs):
            x = jnp.concatenate([x_s[slot * pairs + hp, l] for l in range(CMP_STRIDE)], axis=1).astype(BF16)
            both = jnp.dot(x, wp_ref[slot], preferred_element_type=F32)
            ab += [both[:, :2 * CMP_HIDDEN], both[:, 2 * CMP_HIDDEN:]]
        w1f = w1f_ref[slot]
        pe = (jnp.dot(pe_ref[slot, 0], w1f[:, :CMP_HIDDEN], preferred_element_type=F32)
              + jnp.dot(pe_ref[slot, 1], w1f[:, CMP_HIDDEN:], preferred_element_type=F32))[0:1]
        hid = jnp.concatenate(
            [a[:, :CMP_HIDDEN] + pltpu.roll(a[:, CMP_HIDDEN:], n_ch - 1, 0) + pe for a in ab], axis=1)
        act = hid / (1.0 + jnp.exp(-hid))
        comp.append(jnp.dot(act.astype(BF16), w2_ref[slot], preferred_element_type=F32))
    ck = comp[0]
    lane_g = lax.broadcasted_iota(jnp.int32, ck.shape, 1) // HEAD_DIM
    sq = ck * ck
    scale = jnp.zeros_like(ck)
    for g in range(KV_HEADS):
        ms = jnp.sum(jnp.where(lane_g == g, sq, 0.0), axis=1, keepdims=True) * (1.0 / HEAD_DIM)
        scale = jnp.where(lane_g == g, lax.rsqrt(ms + EPS), scale)
    ck = (ck * scale * kn_ref[...]).astype(BF16)
    cv = comp[1].astype(BF16)

    sc = jnp.dot(ck, q, preferred_element_type=F32) + bc_ref[...]
    m = jnp.max(sc, axis=0, keepdims=True)
    m = jnp.where(m < 0.5 * NEG, 0.0, m)
    p = jnp.exp(sc - m)
    p = p * (1.0 / jnp.maximum(jnp.sum(p, axis=0, keepdims=True), 1e-30))
    ocT = _tn_dot(cv, p.astype(BF16))
    hi = p.astype(BF16)
    lo = (p - hi.astype(F32)).astype(BF16)
    m1 = (jnp.dot(map_ref[...], hi, preferred_element_type=F32)
          + jnp.dot(map_ref[...], lo, preferred_element_type=F32))
    hi = m1.astype(BF16)
    lo = (m1 - hi.astype(F32)).astype(BF16)
    imp = (jnp.dot(hi, gs_ref[...], preferred_element_type=F32)
           + jnp.dot(lo, gs_ref[...], preferred_element_type=F32))
    n_sel_pad = imp.shape[0]
    n_sel = -(-(past_len + dt) // SEL_BLOCK)
    j = lax.broadcasted_iota(jnp.int32, (n_sel_pad, lanes), 0)
    qpos = past_len + lax.broadcasted_iota(jnp.int32, (n_sel_pad, lanes), 1) % dt
    cur = qpos // SEL_BLOCK
    forced = (j == 0) | (j == cur) | (j == cur - 1)
    future = (j * SEL_BLOCK > qpos) | (j >= n_sel)
    v = jnp.where(forced, jnp.inf, jnp.where(future, -jnp.inf, imp))
    v_s[...] = v

    def count(a, cnt):
        va = v_s[pl.ds(a, 1), :]
        ahead = (va > v) | ((va == v) & (j > a))
        return cnt + jnp.where(ahead, 1.0, 0.0)

    rank = lax.fori_loop(0, n_sel, count, jnp.zeros((n_sel_pad, lanes), F32))
    sel = jnp.where((rank < n_top) & (j < n_sel), 0.0, NEG)

    blocks_per_page = page // SEL_BLOCK
    m_run = jnp.full((page, lanes), NEG, F32)
    for pg in range(n_pages):
        kT = pages[pg][0, 2 * KV_WIDTH:3 * KV_WIDTH, :].astype(BF16)
        mask = jnp.concatenate(
            [jnp.broadcast_to(sel[blocks_per_page * pg + bb:blocks_per_page * pg + bb + 1], (SEL_BLOCK, lanes))
             for bb in range(blocks_per_page)], axis=0)
        s = _tn_dot(kT, q) + bs_ref[pg * page:(pg + 1) * page, :] + mask
        s_s[pg * page:(pg + 1) * page, :] = s
        m_run = jnp.maximum(m_run, s)
    nb_past = past_len // SEL_BLOCK
    s_new = (jnp.dot(new_ref[0, 0], q, preferred_element_type=F32) + bs_ref[past_len:past_len + NEW_ROWS, :]
             + sel[nb_past:nb_past + 1])
    m = jnp.maximum(jnp.max(m_run, axis=0, keepdims=True), jnp.max(s_new, axis=0, keepdims=True))
    pn = jnp.exp(s_new - m)
    den = jnp.sum(pn, axis=0, keepdims=True)
    acc = _tn_dot(new_ref[0, 1], pn.astype(BF16))
    for pg in range(n_pages):
        pp = jnp.exp(s_s[pg * page:(pg + 1) * page, :] - m)
        den = den + jnp.sum(pp, axis=0, keepdims=True)
        acc = acc + jnp.dot(pages[pg][0, 3 * KV_WIDTH:4 * KV_WIDTH, :].astype(BF16), pp.astype(BF16),
                            preferred_element_type=F32)
    osT = acc * (1.0 / den)

    wb = win_ref.shape[2]
    s_w = _tn_dot(win_ref[0, 0:KV_WIDTH, :].astype(BF16), q) + bw_ref[0:wb, :]
    s_n = jnp.dot(new_ref[0, 2], q, preferred_element_type=F32) + bw_ref[wb:wb + NEW_ROWS, :]
    m = jnp.maximum(jnp.max(s_w, axis=0, keepdims=True), jnp.max(s_n, axis=0, keepdims=True))
    pw = jnp.exp(s_w - m)
    pn = jnp.exp(s_n - m)
    den = jnp.sum(pw, axis=0, keepdims=True) + jnp.sum(pn, axis=0, keepdims=True)
    acc = (jnp.dot(win_ref[0, KV_WIDTH:2 * KV_WIDTH, :].astype(BF16), pw.astype(BF16), preferred_element_type=F32)
           + _tn_dot(new_ref[0, 3], pn.astype(BF16)))
    owT = acc * (1.0 / den)

    lpg = lanes // KV_HEADS
    o_ref[0] = (gate_ref[0, 0:1, :] * _group_diagonal(ocT, lpg)
                + gate_ref[0, 1:2, :] * _group_diagonal(osT, lpg)
                + gate_ref[0, 2:3, :] * _group_diagonal(owT, lpg))


def _lane_table(rel_bias, dist, valid, dt):
    lanes = np.arange(N_HEADS * dt)
    h, t = lanes // dt, lanes % dt
    return bias_lookup(dist[:, t], valid[:, t], rel_bias[:, jnp.asarray(h)][None])[0]


def nsa_sample_tables(rel_bias, past_len, wb, dt):
    t = np.arange(dt)[None, :]
    n_ch = past_len // CMP_STRIDE
    c = np.arange(n_ch)[:, None]
    dist = past_len + t - (CMP_STRIDE * c + CMP_LEN - 1)
    bc = _lane_table(rel_bias, dist, (c < n_ch - 1) & (dist >= 0), dt)
    s = np.arange(past_len)[:, None]
    n = np.arange(NEW_ROWS)[:, None]
    dist = np.concatenate([past_len + t - s, t - n])
    valid = np.concatenate([np.ones((past_len, dt), bool), (n < dt) & (t - n >= 0)])
    bs = _lane_table(rel_bias, dist, valid, dt)
    l = np.arange(wb)[:, None]
    dist = np.concatenate([wb + t - l, t - n])
    valid = np.concatenate([wb + t - l <= NSA_WINDOW, (n < dt) & (t - n >= 0)])
    bw = _lane_table(rel_bias, dist, valid, dt)
    return bc, bs, bw


def nsa_sample_constants(past_len, dt):
    n_ch = past_len // CMP_STRIDE
    n_sel = -(-(past_len + dt) // SEL_BLOCK)
    n_sel_pad = -(-n_sel // 8) * 8
    c = np.arange(n_ch)[None, :]
    j0 = np.arange(n_sel_pad)[:, None] * SEL_BLOCK
    c_start, c_end = CMP_STRIDE * c, CMP_STRIDE * c + CMP_LEN - 1
    sel_map = (c_start <= j0 + SEL_BLOCK - 1) & (c_end >= j0) & (c < n_ch - 1) & (j0 < n_sel * SEL_BLOCK)
    lanes = np.arange(N_HEADS * dt)
    grp, t = lanes // (REP * dt), lanes % dt
    gsum = (grp[:, None] == grp[None, :]) & (t[:, None] == t[None, :])
    return (jnp.asarray(sel_map.astype(np.float32), dtype=BF16), jnp.asarray(gsum.astype(np.float32), dtype=BF16))


def nsa_sample_attention(cache_pages, page_idx, win_cache, win_base, new_rows, qbd, gates, cw, k_gain_cmp, tables):
    db, n_pages = page_idx.shape
    page = cache_pages.shape[2]
    past_len = n_pages * page
    lanes = qbd.shape[2]
    dt = lanes // N_HEADS
    wb = win_cache.shape[2]
    w1ab, pe, w2 = cw
    w1l = w1ab.reshape(2, CMP_STRIDE, HEAD_DIM, 2 * CMP_HIDDEN)
    wp = jnp.einsum('sldk,gh->slgdhk', w1l, jnp.eye(2, dtype=w1l.dtype)).reshape(
        2, CMP_STRIDE * 128, 4 * CMP_HIDDEN)
    tok = np.arange(page)
    perm = np.zeros((page, page), np.float32)
    perm[tok, (tok % CMP_STRIDE) * (page // CMP_STRIDE) + tok // CMP_STRIDE] = 1.0
    perm = jnp.asarray(perm, dtype=BF16)
    w2bd = jnp.einsum('shd,gk->sghkd', w2, jnp.eye(KV_HEADS, dtype=w2.dtype)).reshape(
        2, KV_HEADS * CMP_HIDDEN, KV_WIDTH)
    kn = jnp.tile(k_gain_cmp.reshape(1, HEAD_DIM), (1, KV_HEADS))
    sel_map, gsum = nsa_sample_constants(past_len, dt)
    bc, bs, bw = tables
    n_sel = -(-(past_len + dt) // SEL_BLOCK)
    assert past_len % SEL_BLOCK == 0 and dt <= SEL_BLOCK and page % SEL_BLOCK == 0

    def const(a):
        nd = a.ndim
        return pl.BlockSpec(a.shape, lambda i, pt: (0,) * nd)

    page_specs = [pl.BlockSpec((1, cache_pages.shape[1], page), functools.partial(lambda p, i, pt: (pt[i, p], 0, 0), p))
                  for p in range(n_pages)]
    consts = [wp, perm, w1ab, pe, w2bd, kn, sel_map, gsum, bc, bs, bw]
    grid_spec = pltpu.PrefetchScalarGridSpec(
        num_scalar_prefetch=1,
        grid=(db,),
        in_specs=page_specs + [
            pl.BlockSpec((1, win_cache.shape[1], wb), lambda i, pt: (win_base + i, 0, 0)),
            pl.BlockSpec((1,) + new_rows.shape[1:], lambda i, pt: (i, 0, 0, 0)),
            pl.BlockSpec((1,) + qbd.shape[1:], lambda i, pt: (i, 0, 0)),
            pl.BlockSpec((1,) + gates.shape[1:], lambda i, pt: (i, 0, 0)),
        ] + [const(a) for a in consts],
        out_specs=pl.BlockSpec((1, HEAD_DIM, lanes), lambda i, pt: (i, 0, 0)),
        scratch_shapes=[pltpu.VMEM((2 * KV_WIDTH // 128, CMP_STRIDE, past_len // CMP_STRIDE, 128), F32),
                        pltpu.VMEM((past_len, lanes), F32),
                        pltpu.VMEM((sel_map.shape[0], lanes), F32)],
    )
    return pl.pallas_call(
        functools.partial(_nsa_sample_kernel, n_pages=n_pages, page=page, n_top=min(SEL_TOP_N, n_sel),
                          past_len=past_len, dt=dt),
        out_shape=jax.ShapeDtypeStruct((db, HEAD_DIM, lanes), F32),
        grid_spec=grid_spec,
        compiler_params=_params("parallel"),
        name="nsa_sample",
    )(page_idx, *([cache_pages] * n_pages), win_cache, new_rows, qbd, gates, *consts)


def _dil_sample_kernel(kv_ref, new_ref, q_ref, b0_ref, b1_ref, b2_ref, o_ref, s_s, *, los, buf_len):
    lanes = q_ref.shape[3]
    chunk = 128
    ms, dens, accs = [], [], []
    for gi, (bias_ref, lo) in enumerate(zip((b0_ref, b1_ref, b2_ref), los)):
        q = q_ref[0, gi]
        n_chunks = (buf_len - lo) // chunk
        m_run = jnp.full((chunk, lanes), NEG, F32)
        for c in range(n_chunks):
            r0 = lo + c * chunk
            s = (_tn_dot(kv_ref[0, 0:KV_WIDTH, r0:r0 + chunk].astype(BF16), q)
                 + bias_ref[c * chunk:(c + 1) * chunk, :])
            s_s[c * chunk:(c + 1) * chunk, :] = s
            m_run = jnp.maximum(m_run, s)
        s_new = (jnp.dot(new_ref[0, 0], q, preferred_element_type=F32)
                 + bias_ref[n_chunks * chunk:n_chunks * chunk + NEW_ROWS, :])
        m = jnp.maximum(jnp.max(m_run, axis=0, keepdims=True), jnp.max(s_new, axis=0, keepdims=True))
        pn = jnp.exp(s_new - m)
        den = jnp.sum(pn, axis=0, keepdims=True)
        acc = _tn_dot(new_ref[0, 1], pn.astype(BF16))
        for c in range(n_chunks):
            r0 = lo + c * chunk
            pp = jnp.exp(s_s[c * chunk:(c + 1) * chunk, :] - m)
            den = den + jnp.sum(pp, axis=0, keepdims=True)
            acc = acc + jnp.dot(kv_ref[0, KV_WIDTH:2 * KV_WIDTH, r0:r0 + chunk].astype(BF16), pp.astype(BF16),
                                preferred_element_type=F32)
        ms.append(m)
        dens.append(den)
        accs.append(acc)
    mx = functools.reduce(jnp.maximum, ms)
    w = [jnp.exp(m - mx) for m in ms]
    num = functools.reduce(jnp.add, [wg * a for wg, a in zip(w, accs)])
    den = functools.reduce(jnp.add, [wg * d for wg, d in zip(w, dens)])
    o_ref[0] = _group_diagonal(num * (1.0 / den), lanes // KV_HEADS)


def dil_sample_tables(rel_bias, buf_len, dt):
    t = np.arange(dt)[None, :]
    n = np.arange(NEW_ROWS)[:, None]
    tables, los = [], []
    for window, dil in DIL_GROUPS:
        lo = max(0, buf_len - window) // 128 * 128
        s = np.arange(lo, buf_len)[:, None]
        dist = np.concatenate([buf_len + t - s, t - n])
        valid = np.concatenate([np.ones((buf_len - lo, dt), bool), (n < dt) & (t - n >= 0)])
        valid = valid & (dist % dil == 0) & (dist <= window)
        tables.append(_lane_table(rel_bias, dist, valid, dt))
        los.append(lo)
    return tables, tuple(los)


def dil_sample_attention(kv_cache, new_rows, qbd3, tables, los):
    db, width, buf_len = kv_cache.shape
    lanes = qbd3.shape[3]
    assert buf_len % 128 == 0
    tspec = [pl.BlockSpec(tb.shape, lambda i: (0, 0)) for tb in tables]
    return pl.pallas_call(
        functools.partial(_dil_sample_kernel, los=los, buf_len=buf_len),
        out_shape=jax.ShapeDtypeStruct((db, HEAD_DIM, lanes), F32),
        grid=(db,),
        in_specs=[
            pl.BlockSpec((1, width, buf_len), lambda i: (i, 0, 0)),
            pl.BlockSpec((1,) + new_rows.shape[1:], lambda i: (i, 0, 0, 0)),
            pl.BlockSpec((1,) + qbd3.shape[1:], lambda i: (i, 0, 0, 0)),
        ] + tspec,
        out_specs=pl.BlockSpec((1, HEAD_DIM, lanes), lambda i: (i, 0, 0)),
        scratch_shapes=[pltpu.VMEM((buf_len, lanes), F32)],
        compiler_params=_params("parallel"),
        name="dil_sample",
    )(kv_cache, new_rows, qbd3, *tables)


def lane_queries(qT, db, dt):
    q = qT.reshape(KV_HEADS, REP, HEAD_DIM, db, dt).transpose(3, 0, 2, 1, 4).reshape(db, KV_HEADS, HEAD_DIM, REP * dt)
    eye = jnp.eye(KV_HEADS, dtype=q.dtype)
    qbd = q[:, :, :, None, :] * eye[None, :, None, :, None]
    return qbd.reshape(db, KV_WIDTH, KV_HEADS * REP * dt).astype(BF16)


def lanes_to_features(o, db, dt):
    o = o.reshape(db, HEAD_DIM, KV_HEADS, REP, dt).transpose(2, 3, 1, 0, 4)
    return o.reshape(N_HEADS * HEAD_DIM, db * dt)


def new_token_rows(zT_rows, db, dt):
    n = zT_rows.shape[0] // KV_WIDTH
    r = zT_rows.reshape(n, KV_WIDTH, db, dt).transpose(2, 0, 3, 1)
    return jnp.pad(r, ((0, 0), (0, 0), (0, NEW_ROWS - dt), (0, 0))).astype(BF16)


def kernel(x_prompt, x_sample, cache_nsa_kv, cache_win_kv, cache_dil_kv, page_table, rel_bias, a_attn_norm, a_w_in, a_q_norm, a_k_norm, a_cmp_pe, a_cmp_w1, a_cmp_w2, a_w_out, kv_norm, w_kv_shared, k_norm_shared, b_attn_norm, b_w_q, b_q_norm, b_w_out, mlp_norm, mlp_w1, mlp_w2):
    b, s, d = x_prompt.shape
    db, dt, _ = x_sample.shape
    n_a = a_w_in.shape[0]
    n_b = b_w_q.shape[0]
    n_pool, page = cache_nsa_kv.shape[1:3]
    wb = cache_win_kv.shape[2]
    buf_len = cache_dil_kv.shape[1]
    past_len = page_table.shape[1] * page
    assert N_HEADS * dt == 128
    tables = {}
    xpT = x_prompt.reshape(b * s, d).T
    xsT = x_sample.reshape(db * dt, d).T
    cache_pages = cache_nsa_kv.transpose(0, 1, 3, 4, 5, 2).reshape(n_a * n_pool, -1, page)
    win_cache = cache_win_kv.transpose(0, 1, 3, 4, 5, 2).reshape(n_a * db, -1, wb)
    nsa_tables = nsa_sample_tables(rel_bias, past_len, wb, dt)
    rows_p, rows_s, wins_p, wins_s = [], [], [], []
    for l in range(n_a + n_b):
        w1T = mlp_w1[l].T.astype(BF16)
        w2T = mlp_w2[l].T.astype(BF16)
        if l < n_a:
            w_inT = a_w_in[l].T.astype(BF16)
            norms = jnp.stack([a_q_norm[l], a_k_norm[l][1], a_k_norm[l][2]])[..., None]
            cw = compress_weights(a_cmp_pe[l], a_cmp_w1[l], a_cmp_w2[l])
            woT = a_w_out[l].T.astype(BF16)
            proj = nsa_proj_prompt(xpT, a_attn_norm[l], w_inT, norms, b, s)
            oT, rows, win = nsa_attention_prompt(proj, b, s, rel_bias, cw, a_k_norm[l][0], tables=tables)
            rows_p.append(rows)
            wins_p.append(win[:, -min(NSA_WINDOW, s):])
            xpT = outproj_mlp(xpT, oT, woT, mlp_norm[l], w1T, w2T)
            zs = norm_proj(xsT, a_attn_norm[l], w_inT, norms, NSA_SEGS)
            gates = zs[2560:2560 + N_BRANCHES * N_HEADS].reshape(N_BRANCHES, KV_HEADS, REP, db, dt)
            gates = gates.transpose(3, 0, 1, 2, 4).reshape(db, N_BRANCHES, N_HEADS * dt)
            o = nsa_sample_attention(
                cache_pages, page_table + l * n_pool, win_cache, l * db, new_token_rows(zs[1536:2560], db, dt),
                lane_queries(zs[:1024], db, dt), gates, cw, a_k_norm[l][0], nsa_tables)
            kv6 = zs[1024:2560].T.reshape(db, dt, 6, KV_HEADS, HEAD_DIM)
            rows_s.append(kv6[:, :, 0:4])
            wins_s.append(kv6[:, :, 4:6])
            xsT = outproj_mlp(xsT, lanes_to_features(o, db, dt).astype(BF16), woT, mlp_norm[l], w1T, w2T)
        else:
            i = l - n_a
            if i == 0:
                w_kvT = w_kv_shared.T.astype(BF16)
                kn = k_norm_shared.reshape(1, HEAD_DIM, 1)
                kvpT = norm_proj(xpT, kv_norm, w_kvT, kn, KV_SEGS)
                kvp = kvpT.T.reshape(b, s, 2, KV_HEADS, HEAD_DIM)
                kvsT = norm_proj(xsT, kv_norm, w_kvT, kn, KV_SEGS)
                kvs = kvsT.T.reshape(db, dt, 2, KV_HEADS, HEAD_DIM)
                dmax = max(w for w, _ in DIL_GROUPS)
                new_dil_p = kvp[:, -min(dmax, s):]
                new_dil_s = jnp.concatenate([cache_dil_kv, kvs], axis=1)[:, -min(dmax, buf_len + dt):]
                dil_cache = cache_dil_kv.transpose(0, 2, 3, 4, 1).reshape(db, -1, buf_len)
                dil_new = new_token_rows(kvsT, db, dt)
                dil_tables, dil_los = dil_sample_tables(rel_bias, buf_len, dt)
            w_qT = b_w_q[i].T.astype(BF16)
            qn = b_q_norm[i][..., None]
            woT = b_w_out[i].T.astype(BF16)
            qT = norm_proj(xpT, b_attn_norm[i], w_qT, qn, DIL_Q_SEGS)
            oT = dil_attention_prompt(qT, kvpT, b, s, rel_bias, tables=tables)
            xpT = outproj_mlp(xpT, oT, woT, mlp_norm[l], w1T, w2T)
            qs = norm_proj(xsT, b_attn_norm[i], w_qT, qn, DIL_Q_SEGS)
            qbd3 = jnp.stack([lane_queries(qs[1024 * gi:1024 * (gi + 1)], db, dt) for gi in range(len(DIL_GROUPS))], axis=1)
            o = dil_sample_attention(dil_cache, dil_new, qbd3, dil_tables, dil_los)
            xsT = outproj_mlp(xsT, lanes_to_features(o, db, dt).astype(BF16), woT, mlp_norm[l], w1T, w2T)
    return (xpT.T.reshape(b, s, d), xsT.T.reshape(db, dt, d),
            jnp.stack(rows_p), jnp.stack(rows_s), jnp.stack(wins_p),
            jnp.concatenate([cache_win_kv, jnp.stack(wins_s)], axis=2)[:, :, -min(NSA_WINDOW, wb + dt):],
            new_dil_p, new_dil_s)
```

```python
import functools
import math

import numpy as np
import jax
import jax.numpy as jnp
from jax import lax
from jax.experimental import pallas as pl
from jax.experimental.pallas import tpu as pltpu

F32 = jnp.float32
BF16 = jnp.bfloat16

D_MODEL = 1024
N_HEADS = 16
HEAD_DIM = 64
KV_HEADS = 4
REP = N_HEADS // KV_HEADS
GROUP_ROWS = REP * HEAD_DIM
KV_WIDTH = KV_HEADS * HEAD_DIM
D_FF = 4 * D_MODEL
EPS = 1e-6
ATTN_SCALE = HEAD_DIM ** -0.5
N_BUCKETS = 32
MAX_DISTANCE = 2048
CMP_LEN = 32
CMP_STRIDE = 16
CMP_HIDDEN = 2 * HEAD_DIM
SEL_BLOCK = 64
SEL_TOP_N = 16
NSA_WINDOW = 512
N_BRANCHES = 3
NSA_IN = N_HEADS * HEAD_DIM + 6 * KV_HEADS * HEAD_DIM + N_BRANCHES * N_HEADS
DIL_GROUPS = ((128, 1), (512, 4), (2048, 16))
PAGE_SIZE = 128

NEG = -1e30
LOG2E = math.log2(math.e)
VMEM_LIMIT = 56 * 1024 * 1024
BIAS_TABLE_LEN = 2048


def _bucket_of_distance(d):
    max_exact = N_BUCKETS // 2
    d = np.maximum(d, 0)
    ratio = np.log(np.maximum(d, 1).astype(np.float32) / np.float32(max_exact)) / np.float32(
        math.log(MAX_DISTANCE / max_exact))
    large = max_exact + (ratio * np.float32(N_BUCKETS - max_exact)).astype(np.int32)
    return np.where(d < max_exact, d, np.minimum(large, N_BUCKETS - 1)).astype(np.int32)


_BUCKETS = _bucket_of_distance(np.arange(BIAS_TABLE_LEN))
BIAS_CONST_FROM = int(np.argmax(_BUCKETS == N_BUCKETS - 1))
assert np.all(_BUCKETS[BIAS_CONST_FROM:] == N_BUCKETS - 1)


def _params(*sem):
    return pltpu.CompilerParams(dimension_semantics=sem, vmem_limit_bytes=VMEM_LIMIT)


def _token_tile(t, tm):
    tm = min(tm, t)
    assert t % tm == 0, (t, tm)
    return tm


def _tn_dot(a, b):
    return lax.dot_general(a, b, (((0,), (0,)), ((), ())), preferred_element_type=F32)


def _bias_lookup_kernel(bkt_ref, tab_ref, o_ref):
    bkt = bkt_ref[...]
    v = jnp.full(bkt.shape, NEG, F32)
    for bb in range(N_BUCKETS):
        v = jnp.where(bkt == bb, tab_ref[0, bb:bb + 1, :], v)
    o_ref[0] = v


def bias_lookup(dist, valid, tab):
    rows, lanes = dist.shape
    bkt = np.where(valid, _BUCKETS[np.clip(dist, 0, BIAS_TABLE_LEN - 1)], N_BUCKETS).astype(np.int32)
    rb = next(r for r in range(min(rows, 512) // 8 * 8, 0, -8) if rows % r == 0)
    nh = tab.shape[0]
    return pl.pallas_call(
        _bias_lookup_kernel,
        out_shape=jax.ShapeDtypeStruct((nh, rows, lanes), F32),
        grid=(rows // rb, nh),
        in_specs=[pl.BlockSpec((rb, lanes), lambda r, h: (r, 0)),
                  pl.BlockSpec((1, N_BUCKETS, lanes), lambda r, h: (h, 0, 0))],
        out_specs=pl.BlockSpec((1, rb, lanes), lambda r, h: (h, r, 0)),
        compiler_params=_params("parallel", "parallel"),
        name="bias_lookup",
    )(jnp.asarray(bkt), tab)


def _cached(tables, key, build):
    if tables is None:
        return build()
    if key not in tables:
        tables[key] = build()
    return tables[key]


def head_rows(rel_bias, lanes):
    return jnp.broadcast_to((rel_bias.T * LOG2E)[:, :, None], (N_HEADS, N_BUCKETS, lanes))


def _proj_kernel(x_ref, g_ref, w_ref, n_ref, o_ref, *, segs):
    x = x_ref[...]
    ms = jnp.mean(x * x, axis=0, keepdims=True)
    xn = (x * lax.rsqrt(ms + EPS) * g_ref[...]).astype(BF16)
    tm = x.shape[1]
    for r0, nr, kind, ni, scale in segs:
        z = jnp.dot(w_ref[r0:r0 + nr, :], xn, preferred_element_type=F32)
        if kind == "norm":
            z3 = z.reshape(nr // HEAD_DIM, HEAD_DIM, tm)
            hs = jnp.mean(z3 * z3, axis=1, keepdims=True)
            z = (z3 * lax.rsqrt(hs + EPS) * n_ref[ni][None] * scale).reshape(nr, tm)
        elif kind == "sigmoid":
            z = 1.0 / (1.0 + jnp.exp(-z))
        o_ref[r0:r0 + nr, :] = z


def norm_proj(xT, gain, wT, norms, segs, tm=512):
    d, t = xT.shape
    tm = _token_tile(t, tm)
    n = wT.shape[0]
    return pl.pallas_call(
        functools.partial(_proj_kernel, segs=tuple(segs)),
        out_shape=jax.ShapeDtypeStruct((n, t), F32),
        grid=(t // tm,),
        in_specs=[
            pl.BlockSpec((d, tm), lambda i: (0, i)),
            pl.BlockSpec((d, 1), lambda i: (0, 0)),
            pl.BlockSpec((n, d), lambda i: (0, 0)),
            pl.BlockSpec(norms.shape, lambda i: (0, 0, 0)),
        ],
        out_specs=pl.BlockSpec((n, tm), lambda i: (0, i)),
        compiler_params=_params("parallel"),
        name="norm_proj",
    )(xT, gain.reshape(d, 1), wT, norms)


def _nsa_proj_kernel(x_ref, g_ref, w_ref, n_ref, q_ref, rows_ref, gate_ref, ks_ref, vs_ref, kw_ref, vw_ref, *, t):
    x = x_ref[...]
    ms = jnp.mean(x * x, axis=0, keepdims=True)
    xn = (x * lax.rsqrt(ms + EPS) * g_ref[...]).astype(BF16)
    tm = x.shape[1]

    def proj(r0, nr):
        return jnp.dot(w_ref[r0:r0 + nr, :], xn, preferred_element_type=F32)

    def head_norm(z, ni, scale):
        z3 = z.reshape(z.shape[0] // HEAD_DIM, HEAD_DIM, tm)
        hs = jnp.mean(z3 * z3, axis=1, keepdims=True)
        return (z3 * lax.rsqrt(hs + EPS) * n_ref[ni][None] * scale).reshape(z.shape)

    half = N_HEADS * HEAD_DIM // 2
    for c in range(2):
        q_ref[c * half:(c + 1) * half, :] = head_norm(proj(c * half, half), 0, ATTN_SCALE * LOG2E).astype(BF16)
    base = N_HEADS * HEAD_DIM
    rows_ref[0:2 * KV_WIDTH, :] = proj(base, 2 * KV_WIDTH)
    kv = [head_norm(proj(base + 2 * KV_WIDTH, KV_WIDTH), 1, 1.0), proj(base + 3 * KV_WIDTH, KV_WIDTH),
          head_norm(proj(base + 4 * KV_WIDTH, KV_WIDTH), 2, 1.0), proj(base + 5 * KV_WIDTH, KV_WIDTH)]
    for idx, z in enumerate(kv):
        rows_ref[(2 + idx) * KV_WIDTH:(3 + idx) * KV_WIDTH, :] = z
    gate_ref[...] = 1.0 / (1.0 + jnp.exp(-proj(base + 6 * KV_WIDTH, N_BRANCHES * N_HEADS)))
    r = lax.broadcasted_iota(jnp.int32, (SEL_ROWS, t), 0)
    col = lax.broadcasted_iota(jnp.int32, (SEL_ROWS, t), 1)
    block_rows = jnp.where(col // SEL_BLOCK == r, 1.0, 0.0).astype(BF16)
    ones_rows = jnp.where(r == 0, 1.0, 0.0).astype(BF16)
    for (ref, extra), z in zip(((ks_ref, block_rows), (vs_ref, ones_rows), (kw_ref, None), (vw_ref, ones_rows)), kv):
        for j in range(tm // t):
            for g in range(KV_HEADS):
                ref[0, g, j, 0:HEAD_DIM, :] = z[g * HEAD_DIM:(g + 1) * HEAD_DIM, j * t:(j + 1) * t].astype(BF16)
                if extra is not None:
                    ref[0, g, j, HEAD_DIM:HEAD_DIM + SEL_ROWS, :] = extra


def nsa_proj_prompt(xT, gain, wT, norms, b, s, t=256, tm=512):
    d, tt = xT.shape
    nps = s // tm
    assert tt == b * s and s % tm == 0 and tm % t == 0
    n = wT.shape[0]
    col = lambda rows, dt: (jax.ShapeDtypeStruct((rows, tt), dt), pl.BlockSpec((rows, tm), lambda i: (0, i)))
    tile = lambda rows: (jax.ShapeDtypeStruct((b, KV_HEADS, s // t, rows, t), BF16),
                         pl.BlockSpec((1, KV_HEADS, tm // t, rows, t), lambda i: (i // nps, 0, i % nps, 0, 0)))
    outs = [col(N_HEADS * HEAD_DIM, BF16), col(6 * KV_WIDTH, F32), col(N_BRANCHES * N_HEADS, F32),
            tile(HEAD_DIM + SEL_ROWS), tile(HEAD_DIM + SEL_ROWS), tile(HEAD_DIM), tile(HEAD_DIM + SEL_ROWS)]
    return pl.pallas_call(
        functools.partial(_nsa_proj_kernel, t=t),
        out_shape=tuple(o[0] for o in outs),
        grid=(tt // tm,),
        in_specs=[
            pl.BlockSpec((d, tm), lambda i: (0, i)),
            pl.BlockSpec((d, 1), lambda i: (0, 0)),
            pl.BlockSpec((n, d), lambda i: (0, 0)),
            pl.BlockSpec(norms.shape, lambda i: (0, 0, 0)),
        ],
        out_specs=tuple(o[1] for o in outs),
        compiler_params=_params("parallel"),
        name="nsa_proj",
    )(xT, gain.reshape(d, 1), wT, norms)


def _outmlp_kernel(x_ref, o_ref, wo_ref, g_ref, w1_ref, w2_ref, y_ref, x1_s, xn_s, acc_s):
    f = pl.program_id(1)

    @pl.when(f == 0)
    def _():
        x1 = x_ref[...] + jnp.dot(wo_ref[...], o_ref[...], preferred_element_type=F32)
        x1_s[...] = x1
        ms = jnp.mean(x1 * x1, axis=0, keepdims=True)
        xn_s[...] = (x1 * lax.rsqrt(ms + EPS) * g_ref[...]).astype(BF16)
        acc_s[...] = jnp.zeros_like(acc_s)

    h = jnp.maximum(jnp.dot(w1_ref[...], xn_s[...], preferred_element_type=F32), 0.0)
    acc_s[...] += jnp.dot(w2_ref[...], (h * h).astype(BF16), preferred_element_type=F32)

    @pl.when(f == pl.num_programs(1) - 1)
    def _():
        y_ref[...] = x1_s[...] + acc_s[...]


def outproj_mlp(xT, oT, woT, gain, w1T, w2T, tm=1024, tf=1024):
    d, t = xT.shape
    tm = _token_tile(t, tm)
    dff = w1T.shape[0]
    return pl.pallas_call(
        _outmlp_kernel,
        out_shape=jax.ShapeDtypeStruct((d, t), F32),
        grid=(t // tm, dff // tf),
        in_specs=[
            pl.BlockSpec((d, tm), lambda i, f: (0, i)),
            pl.BlockSpec((d, tm), lambda i, f: (0, i)),
            pl.BlockSpec((d, d), lambda i, f: (0, 0)),
            pl.BlockSpec((d, 1), lambda i, f: (0, 0)),
            pl.BlockSpec((tf, d), lambda i, f: (f, 0)),
            pl.BlockSpec((d, tf), lambda i, f: (0, f)),
        ],
        out_specs=pl.BlockSpec((d, tm), lambda i, f: (0, i)),
        scratch_shapes=[pltpu.VMEM((d, tm), F32), pltpu.VMEM((d, tm), BF16), pltpu.VMEM((d, tm), F32)],
        compiler_params=_params("parallel", "arbitrary"),
        name="outproj_mlp",
    )(xT, oT, woT, gain.reshape(d, 1), w1T, w2T)


def _compress_kernel(ch_ref, w1_ref, pe_ref, w2_ref, kn_ref, o_ref):
    slot = pl.program_id(0)
    bn, c, _ = ch_ref.shape[1:]
    w1 = w1_ref[0]
    ab = jnp.dot(ch_ref[0].reshape(bn * c, CMP_STRIDE * HEAD_DIM), w1, preferred_element_type=F32)
    pe_a = jnp.dot(pe_ref[0, 0], w1[:, :CMP_HIDDEN], preferred_element_type=F32)[0:1]
    pe_b = jnp.dot(pe_ref[0, 1], w1[:, CMP_HIDDEN:], preferred_element_type=F32)[0:1]
    nxt = pltpu.roll(ab[:, CMP_HIDDEN:], bn * c - 1, 0)
    hid = ab[:, :CMP_HIDDEN] + nxt + (pe_a + pe_b)
    act = hid / (1.0 + jnp.exp(-hid))
    out = jnp.dot(act.astype(BF16), w2_ref[0], preferred_element_type=F32)
    ms = jnp.mean(out * out, axis=-1, keepdims=True)
    normed = out * lax.rsqrt(ms + EPS) * kn_ref[...]
    o_ref[0] = jnp.where(slot == 0, normed, out).reshape(bn, c, HEAD_DIM)


def compress(ch, w1ab, pe, w2, k_gain, bn):
    _, nb, c, w = ch.shape
    return pl.pallas_call(
        _compress_kernel,
        out_shape=jax.ShapeDtypeStruct((2, nb, c, HEAD_DIM), F32),
        grid=(2, nb // bn),
        in_specs=[
            pl.BlockSpec((1, bn, c, w), lambda s, i: (s, i, 0, 0)),
            pl.BlockSpec((1, w, 2 * CMP_HIDDEN), lambda s, i: (s, 0, 0)),
            pl.BlockSpec((1, 2, 16, w), lambda s, i: (s, 0, 0, 0)),
            pl.BlockSpec((1, CMP_HIDDEN, HEAD_DIM), lambda s, i: (s, 0, 0)),
            pl.BlockSpec((1, HEAD_DIM), lambda s, i: (0, 0)),
        ],
        out_specs=pl.BlockSpec((1, bn, c, HEAD_DIM), lambda s, i: (s, i, 0, 0)),
        compiler_params=_params("parallel", "parallel"),
        name="nsa_compress",
    )(ch, w1ab, pe, w2, k_gain.reshape(1, HEAD_DIM))


def compress_weights(cmp_pe, cmp_w1, cmp_w2):
    half = CMP_STRIDE * HEAD_DIM
    w1 = cmp_w1.reshape(2, 2, half, CMP_HIDDEN)
    w1ab = jnp.concatenate([w1[:, 0], w1[:, 1]], axis=-1).astype(BF16)
    pe = jnp.broadcast_to(cmp_pe.reshape(2, 2, 1, half), (2, 2, 16, half)).astype(BF16)
    return w1ab, pe, cmp_w2.astype(BF16)


CMP_CHUNK = 128


def _cmp_topk_kernel(q_ref, ck_ref, cv_ref, map_ref, bias_ref, oc_ref, sel_ref, v_s, s_s, cnt_s, *,
                     tq, n_top, k_const):
    i = pl.program_id(2)
    c = ck_ref.shape[2]
    n_sel = map_ref.shape[0]
    n_chunks = c // CMP_CHUNK
    ms = []
    for r in range(REP):
        q = q_ref[r * HEAD_DIM:(r + 1) * HEAD_DIM, :]
        cm = None
        for jc in range(n_chunks):
            k0 = i * (tq // CMP_STRIDE) + jc * CMP_CHUNK
            start = pl.multiple_of(jnp.minimum(k0, k_const), 8)
            s = (jnp.dot(ck_ref[0, 0, jc * CMP_CHUNK:(jc + 1) * CMP_CHUNK, :], q, preferred_element_type=F32)
                 + bias_ref[r, pl.ds(start, CMP_CHUNK), :])
            s_s[r, jc * CMP_CHUNK:(jc + 1) * CMP_CHUNK, :] = s
            mx = jnp.max(s, axis=0, keepdims=True)
            cm = mx if cm is None else jnp.maximum(cm, mx)
        ms.append(jnp.where(cm < 0.5 * NEG, 0.0, cm))
    psum = None
    for r in range(REP):
        p = jnp.exp2(s_s[r] - ms[r])
        p = p * (1.0 / jnp.maximum(jnp.sum(p, axis=0, keepdims=True), 1e-30))
        oc_ref[r * HEAD_DIM:(r + 1) * HEAD_DIM, :] = jnp.dot(
            cv_ref[0, 0], p.astype(BF16), preferred_element_type=F32)
        psum = p if psum is None else psum + p
    hi = psum.astype(BF16)
    lo = (psum - hi.astype(F32)).astype(BF16)
    imp = (jnp.dot(map_ref[...], hi, preferred_element_type=F32)
           + jnp.dot(map_ref[...], lo, preferred_element_type=F32))
    qpos = i * tq + lax.broadcasted_iota(jnp.int32, (n_sel, tq), 1)
    j = lax.broadcasted_iota(jnp.int32, (n_sel, tq), 0)
    cur = qpos // SEL_BLOCK
    forced = (j == 0) | (j == cur) | (j == cur - 1)
    future = j * SEL_BLOCK > qpos
    v = jnp.where(forced, jnp.inf, jnp.where(future, -jnp.inf, imp))
    v_s[...] = v
    groups = n_sel // 8
    row = lax.broadcasted_iota(jnp.int32, (8, 128), 0)
    cnt_s[...] = jnp.zeros(cnt_s.shape, F32)
    newest = (i * tq + tq - 1) // SEL_BLOCK
    for ga in range(groups):
        @pl.when(8 * ga <= newest)
        def _(ga=ga):
            for lt in range(tq // 128):
                lanes = slice(lt * 128, (lt + 1) * 128)
                vg = [v_s[8 * gi:8 * gi + 8, lanes] for gi in range(groups)]
                cnt = [cnt_s[8 * gi:8 * gi + 8, lanes] for gi in range(groups)]
                for a in range(8 * ga, 8 * ga + 8):
                    va = v_s[a:a + 1, lanes]
                    for gi in range(groups):
                        if 8 * gi > a:
                            ahead = va >= vg[gi]
                        elif 8 * gi + 7 < a:
                            ahead = va > vg[gi]
                        else:
                            ahead = (va > vg[gi]) | ((va == vg[gi]) & (row > a - 8 * gi))
                        cnt[gi] = cnt[gi] + jnp.where(ahead, 1.0, 0.0)
                for gi in range(groups):
                    cnt_s[8 * gi:8 * gi + 8, lanes] = cnt[gi]

    for lt in range(tq // 128):
        lanes = slice(lt * 128, (lt + 1) * 128)
        bpt = n_sel // sel_ref.shape[2]
        for gi in range(groups):
            mask = jnp.where(cnt_s[8 * gi:8 * gi + 8, lanes] < n_top, 0.0, NEG)
            for hh in range(8 // bpt):
                part = mask if hh == 0 else pltpu.roll(mask, 8 - hh * bpt, 0)
                part = jnp.where(row < bpt, part, 0.0)
                sel_ref[0, 0, gi * (8 // bpt) + hh, :, lanes] = jnp.concatenate(
                    [part, jnp.zeros((SEL_ROWS - 8, 128), F32)], axis=0).astype(BF16)


def cmp_strip(rel_bias, c, tq, rows):
    k = np.arange(rows)[:, None]
    tl = np.arange(tq)[None, :]
    d = CMP_STRIDE * (k - (c - 1)) + tl - (CMP_LEN - 1)
    return bias_lookup(d, d >= 0, head_rows(rel_bias, tq))


def cmp_topk(qT, ckr, cvTr, rel_bias, b, s, tq=256, key_tile=256, tables=None):
    c = s // CMP_STRIDE
    n_sel = s // SEL_BLOCK
    nkt = s // key_tile
    assert 8 % (key_tile // SEL_BLOCK) == 0 and n_sel % 8 == 0
    n_top = min(SEL_TOP_N, n_sel)
    nq = s // tq
    k_const = (c - 1) + -(-(BIAS_CONST_FROM + CMP_LEN - 1) // CMP_STRIDE)
    k_const = -(-k_const // 8) * 8
    strip = _cached(tables, ("cmp", c, tq), lambda: cmp_strip(rel_bias, c, tq, k_const + CMP_CHUNK))
    cc = np.arange(c)[::-1]
    c_end = cc * CMP_STRIDE + CMP_LEN - 1
    c_start = c_end - CMP_LEN + 1
    j0 = np.arange(n_sel)[:, None] * SEL_BLOCK
    sel_map = ((c_start[None] <= j0 + SEL_BLOCK - 1) & (c_end[None] >= j0) & (cc[None] < c - 1))
    sel_map = jnp.asarray(sel_map.astype(np.float32), dtype=BF16)
    return pl.pallas_call(
        functools.partial(_cmp_topk_kernel, tq=tq, n_top=n_top, k_const=k_const),
        out_shape=(jax.ShapeDtypeStruct((N_HEADS * HEAD_DIM, b * s), F32),
                   jax.ShapeDtypeStruct((b, KV_HEADS, nkt, SEL_ROWS, s), BF16)),
        grid=(KV_HEADS, b, nq),
        in_specs=[
            pl.BlockSpec((GROUP_ROWS, tq), lambda g, n, i: (g, n * nq + i)),
            pl.BlockSpec((1, 1, c, HEAD_DIM), lambda g, n, i: (n, g, 0, 0)),
            pl.BlockSpec((1, 1, HEAD_DIM, c), lambda g, n, i: (n, g, 0, 0)),
            pl.BlockSpec((n_sel, c), lambda g, n, i: (0, 0)),
            pl.BlockSpec((REP, strip.shape[1], tq), lambda g, n, i: (g, 0, 0)),
        ],
        out_specs=(pl.BlockSpec((GROUP_ROWS, tq), lambda g, n, i: (g, n * nq + i)),
                   pl.BlockSpec((1, 1, nkt, SEL_ROWS, tq), lambda g, n, i: (n, g, 0, 0, i))),
        scratch_shapes=[pltpu.VMEM((n_sel, tq), F32), pltpu.VMEM((REP, c, tq), F32), pltpu.VMEM((n_sel, tq), F32)],
        compiler_params=_params("parallel", "parallel", "parallel"),
        name="nsa_cmp_topk",
    )(qT, ckr, cvTr, sel_map, strip)


SEL_ROWS = 16


def _flash_kernel(*refs, tq, n_tiles_max, n_delta, use_sel):
    nh = REP
    if use_sel:
        q_ref, k_ref, v_ref, bias_ref, sel_ref = refs[:5]
        outs = refs[5:]
    else:
        q_ref, k_ref, v_ref, bias_ref = refs[:4]
        sel_ref = None
        outs = refs[4:]
    o_ref = outs[0]
    accs, s_bufs = outs[1:1 + nh], outs[1 + nh:]
    i = pl.program_id(2)
    n_tiles = jnp.minimum(i + 1, n_tiles_max)
    for acc in accs:
        acc[...] = jnp.zeros(acc.shape, F32)

    def scores(jt, s_buf):
        kt = jnp.maximum(i - jt, 0)
        kT = k_ref[0, 0, kt]
        tile = jnp.where(jt < n_tiles, jnp.minimum(jt, n_delta - 1), n_delta)
        cms = []
        for r in range(nh):
            q = q_ref[r * HEAD_DIM:(r + 1) * HEAD_DIM, :]
            if use_sel:
                q = jnp.concatenate([q, sel_ref[0, 0, kt]], axis=0)
            s = _tn_dot(kT, q) + bias_ref[r, tile]
            s_buf[r] = s
            cms.append(jnp.max(s, axis=0, keepdims=True))
        return tuple(cms)

    def accumulate(jt, s_buf, cms, ms):
        v = v_ref[0, 0, jnp.maximum(i - jt, 0)]
        new_m = []
        for r in range(nh):
            m_new = jnp.maximum(ms[r], cms[r])
            alpha = jnp.exp2(ms[r] - m_new)
            p = jnp.exp2((s_buf[r] - m_new).astype(BF16))
            accs[r][...] = alpha * accs[r][...] + jnp.dot(v, p, preferred_element_type=F32)
            new_m.append(m_new)
        return tuple(new_m)

    def pair(jj, carry):
        ms, cm0 = carry
        cm1 = scores(2 * jj + 1, s_bufs[1])
        ms = accumulate(2 * jj, s_bufs[0], cm0, ms)
        cm0 = scores(2 * jj + 2, s_bufs[0])
        ms = accumulate(2 * jj + 1, s_bufs[1], cm1, ms)
        return ms, cm0

    init = (tuple(jnp.full((1, tq), NEG, F32) for _ in range(nh)), scores(0, s_bufs[0]))
    lax.fori_loop(0, (n_tiles + 1) // 2, pair, init)
    for r in range(nh):
        o_ref[r * HEAD_DIM:(r + 1) * HEAD_DIM, :] = (
            accs[r][0:HEAD_DIM, :] * (1.0 / accs[r][HEAD_DIM:HEAD_DIM + 1, :]))


def flash_bias_tiles(rel_bias, t, n_delta, dil, window):
    j = np.arange(n_delta + 1)[:, None, None]
    sl = np.arange(t)[None, :, None]
    tl = np.arange(t)[None, None, :]
    d = j * t + tl - sl
    ok = (d >= 0) & (j < n_delta)
    if window is not None:
        ok = ok & (d <= window)
    tiles = bias_lookup((d * dil).reshape(-1, t), ok.reshape(-1, t), head_rows(rel_bias, t))
    return tiles.reshape(N_HEADS, n_delta + 1, t, t)


def flash_attention(qT, kT, vT, rel_bias, nseq, seqlen, sel=None, window=None, dil=1, tables=None):
    nkt, tk = kT.shape[2], kT.shape[4]
    tq = tk
    assert seqlen == nkt * tk
    nq = nkt
    if window is None:
        n_delta = -(-(BIAS_CONST_FROM + tk - 1) // tk) + 1
        n_tiles_max = nkt
    else:
        n_tiles_max = (window + tk - 1) // tk + 1
        n_delta = n_tiles_max
    tiles = _cached(tables, ("flash", tk, n_delta, dil, window),
                    lambda: flash_bias_tiles(rel_bias, tk, n_delta, dil, window))
    h_rows = qT.shape[0]
    use_sel = sel is not None
    in_specs = [
        pl.BlockSpec((GROUP_ROWS, tq), lambda g, n, i: (g, n * nq + i)),
        pl.BlockSpec((1, 1) + kT.shape[2:], lambda g, n, i: (n, g, 0, 0, 0)),
        pl.BlockSpec((1, 1) + vT.shape[2:], lambda g, n, i: (n, g, 0, 0, 0)),
        pl.BlockSpec((REP, n_delta + 1, tk, tq), lambda g, n, i: (g, 0, 0, 0)),
    ]
    args = [qT, kT, vT, tiles]
    if use_sel:
        in_specs.append(pl.BlockSpec((1, 1) + sel.shape[2:4] + (tq,), lambda g, n, i: (n, g, 0, 0, i)))
        args.append(sel)
    return pl.pallas_call(
        functools.partial(_flash_kernel, tq=tq, n_tiles_max=n_tiles_max, n_delta=n_delta, use_sel=use_sel),
        out_shape=jax.ShapeDtypeStruct((h_rows, nseq * seqlen), F32),
        grid=(KV_HEADS, nseq, nq),
        in_specs=in_specs,
        out_specs=pl.BlockSpec((GROUP_ROWS, tq), lambda g, n, i: (g, n * nq + i)),
        scratch_shapes=([pltpu.VMEM((vT.shape[3], tq), F32) for _ in range(REP)]
                        + [pltpu.VMEM((REP, tk, tq), F32) for _ in range(2)]),
        compiler_params=_params("parallel", "parallel", "parallel"),
        name="flash_sel" if use_sel else ("flash_band" if window is not None else "flash_causal"),
    )(*args)


def _band_kernel(q_ref, k_ref, v_ref, bias_ref, o_ref, mo_ref, lo_ref, *, t, n_tiles, sub):
    step = pl.program_id(2)
    for u in range(sub):
        i = step * sub + u
        cols = slice(u * t, (u + 1) * t)
        kts = [jnp.maximum(i - jt, 0) for jt in range(n_tiles)]
        tiles = [jnp.where(jt <= i, jt, n_tiles) for jt in range(n_tiles)]
        kTs = [k_ref[0, 0, kt] for kt in kts]
        vTs = [v_ref[0, 0, kt] for kt in kts]
        for r in range(REP):
            rows = slice(r * HEAD_DIM, (r + 1) * HEAD_DIM)
            q = q_ref[rows, cols]
            s = [_tn_dot(kTs[jt], q) + bias_ref[r, tiles[jt]] for jt in range(n_tiles)]
            m = functools.reduce(jnp.maximum, [jnp.max(x, axis=0, keepdims=True) for x in s])
            p = [jnp.exp2(x - m) for x in s]
            den = functools.reduce(jnp.add, [jnp.sum(x, axis=0, keepdims=True) for x in p])
            acc = functools.reduce(jnp.add, [jnp.dot(vTs[jt], p[jt].astype(BF16), preferred_element_type=F32)
                                             for jt in range(n_tiles)])
            o_ref[rows, cols] = acc
            mo_ref[0, r:r + 1, cols] = m
            lo_ref[0, r:r + 1, cols] = den


def band_attention(qT, kT, vT, rel_bias, nseq, seqlen, window, dil, sub=8, tables=None):
    nkt, t = kT.shape[2], kT.shape[4]
    n_tiles = (window + t - 1) // t + 1
    sub = min(sub, nkt)
    assert seqlen == nkt * t and nkt % sub == 0
    nq = nkt // sub
    tiles = _cached(tables, ("flash", t, n_tiles, dil, window),
                    lambda: flash_bias_tiles(rel_bias, t, n_tiles, dil, window))
    o_spec = pl.BlockSpec((GROUP_ROWS, sub * t), lambda g, n, i: (g, n * nq + i))
    st_spec = pl.BlockSpec((1, REP, sub * t), lambda g, n, i: (g, 0, n * nq + i))
    st_shape = jax.ShapeDtypeStruct((KV_HEADS, REP, nseq * seqlen), F32)
    return pl.pallas_call(
        functools.partial(_band_kernel, t=t, n_tiles=n_tiles, sub=sub),
        out_shape=(jax.ShapeDtypeStruct((qT.shape[0], nseq * seqlen), F32), st_shape, st_shape),
        grid=(KV_HEADS, nseq, nq),
        in_specs=[
            pl.BlockSpec((GROUP_ROWS, sub * t), lambda g, n, i: (g, n * nq + i)),
            pl.BlockSpec((1, 1) + kT.shape[2:], lambda g, n, i: (n, g, 0, 0, 0)),
            pl.BlockSpec((1, 1) + vT.shape[2:], lambda g, n, i: (n, g, 0, 0, 0)),
            pl.BlockSpec((REP, n_tiles + 1, t, t), lambda g, n, i: (g, 0, 0, 0)),
        ],
        out_specs=(o_spec, st_spec, st_spec),
        compiler_params=_params("parallel", "parallel", "parallel"),
        name="band_attention",
    )(qT, kT, vT, tiles)


def _nsa_combine_kernel(oc_ref, os_ref, ow_ref, g_ref, o_ref):
    for h in range(N_HEADS):
        rows = slice(h * HEAD_DIM, (h + 1) * HEAD_DIM)
        o = (g_ref[h:h + 1, :] * oc_ref[rows, :]
             + g_ref[N_HEADS + h:N_HEADS + h + 1, :] * os_ref[rows, :]
             + g_ref[2 * N_HEADS + h:2 * N_HEADS + h + 1, :] * ow_ref[rows, :])
        o_ref[rows, :] = o.astype(BF16)


def nsa_combine(ocT, osT, owT, gatesT, tm=512):
    d, t = ocT.shape
    tm = _token_tile(t, tm)
    big = pl.BlockSpec((d, tm), lambda i: (0, i))
    return pl.pallas_call(
        _nsa_combine_kernel,
        out_shape=jax.ShapeDtypeStruct((d, t), BF16),
        grid=(t // tm,),
        in_specs=[big, big, big, pl.BlockSpec((N_BRANCHES * N_HEADS, tm), lambda i: (0, i))],
        out_specs=big,
        compiler_params=_params("parallel"),
        name="nsa_combine",
    )(ocT, osT, owT, gatesT)


def _dil_merge_kernel(*refs):
    ng = (len(refs) - 3)
    acc_refs, (m_ref, l_ref, o_ref) = refs[:ng], refs[ng:]
    mx = functools.reduce(jnp.maximum, [m_ref[g] for g in range(ng)])
    w = [jnp.exp2(m_ref[g] - mx) for g in range(ng)]
    den = functools.reduce(jnp.add, [w[g] * l_ref[g] for g in range(ng)])
    inv = 1.0 / den
    for h in range(N_HEADS):
        rows = slice(h * HEAD_DIM, (h + 1) * HEAD_DIM)
        num = functools.reduce(jnp.add, [w[g][h:h + 1, :] * acc_refs[g][rows, :] for g in range(ng)])
        o_ref[rows, :] = (num * inv[h:h + 1, :]).astype(BF16)


def dil_merge(accs, m, l, tm=512):
    ng = len(accs)
    d, t = accs[0].shape
    tm = _token_tile(t, tm)
    st = pl.BlockSpec((ng, N_HEADS, tm), lambda i: (0, 0, i))
    big = pl.BlockSpec((d, tm), lambda i: (0, i))
    return pl.pallas_call(
        _dil_merge_kernel,
        out_shape=jax.ShapeDtypeStruct((d, t), BF16),
        grid=(t // tm,),
        in_specs=[big] * ng + [st, st],
        out_specs=big,
        compiler_params=_params("parallel"),
        name="dil_merge",
    )(*accs, m, l)


NSA_SEGS = (
    (0, 512, "norm", 0, ATTN_SCALE), (512, 512, "norm", 0, ATTN_SCALE),
    (1024, 512, "plain", 0, 1.0),
    (1536, 256, "norm", 1, 1.0),
    (1792, 256, "plain", 0, 1.0),
    (2048, 256, "norm", 2, 1.0),
    (2304, 256, "plain", 0, 1.0),
    (2560, N_BRANCHES * N_HEADS, "sigmoid", 0, 1.0),
)
KV_SEGS = ((0, 256, "norm", 0, 1.0), (256, 256, "plain", 0, 1.0))
DIL_Q_SEGS = tuple((512 * j, 512, "norm", j // 2, ATTN_SCALE) for j in range(6))


def kv_tiles(xT, nseq, seqlen, t):
    return xT.reshape(KV_HEADS, HEAD_DIM, nseq, seqlen // t, t).transpose(2, 0, 3, 1, 4).astype(BF16)


def nsa_attention_prompt(proj, b, s, rel_bias, cw, k_gain_cmp, tables=None):
    qT, rowsT, gatesT, ksT, vsT, kwT, vwT = proj
    t = ksT.shape[4]
    rows, win = rowsT[:4 * KV_WIDTH], rowsT[4 * KV_WIDTH:]
    c = s // CMP_STRIDE
    ch = rowsT[:2 * KV_WIDTH].astype(BF16).reshape(2, KV_HEADS, HEAD_DIM, b, c, CMP_STRIDE)
    ch = ch.transpose(0, 3, 1, 4, 5, 2).reshape(2, b * KV_HEADS, c, CMP_STRIDE * HEAD_DIM)
    cmp = compress(ch, *cw, k_gain_cmp, bn=4)
    cmp = jnp.flip(cmp, axis=2).reshape(2, b, KV_HEADS, c, HEAD_DIM)
    ckr = cmp[0].astype(BF16)
    cvTr = cmp[1].transpose(0, 1, 3, 2).astype(BF16)
    ocT, mask = cmp_topk(qT, ckr, cvTr, rel_bias, b, s, key_tile=t, tables=tables)
    osT = flash_attention(qT, ksT, vsT, rel_bias, b, s, sel=mask, tables=tables)
    owT = flash_attention(qT, kwT, vwT, rel_bias, b, s, window=NSA_WINDOW, tables=tables)
    oT = nsa_combine(ocT, osT, owT, gatesT)
    return oT, rows, win


def dil_attention_prompt(qallT, kvT, b, s, rel_bias, t=128, tables=None):
    accs, ms, ls = [], [], []
    for gi, (window, dil) in enumerate(DIL_GROUPS):
        length = s // dil

        def split(x):
            r = x.shape[0]
            return x.reshape(r, b, length, dil).transpose(0, 1, 3, 2).reshape(r, b * s)

        def merge(x):
            r = x.shape[0]
            return x.reshape(r, b, dil, length).transpose(0, 1, 3, 2).reshape(r, b * s)

        qg = split((qallT[1024 * gi:1024 * (gi + 1)] * LOG2E).astype(BF16))
        kvg = split(kvT.astype(BF16))
        acc, m, l = band_attention(qg, kv_tiles(kvg[:KV_WIDTH], b * dil, length, t),
                                   kv_tiles(kvg[KV_WIDTH:], b * dil, length, t), rel_bias, b * dil, length,
                                   window // dil, dil, tables=tables)

        accs.append(merge(acc))
        ms.append(merge(m.reshape(N_HEADS, b * s)))
        ls.append(merge(l.reshape(N_HEADS, b * s)))
    return dil_merge(accs, jnp.stack(ms), jnp.stack(ls))


NEW_ROWS = 16


def _group_diagonal(oT, lanes_per_group):
    grp = lax.broadcasted_iota(jnp.int32, (HEAD_DIM, oT.shape[1]), 1) // lanes_per_group
    out = jnp.zeros((HEAD_DIM, oT.shape[1]), F32)
    for g in range(KV_HEADS):
        out = jnp.where(grp == g, oT[g * HEAD_DIM:(g + 1) * HEAD_DIM, :], out)
    return out


def _nsa_sample_kernel(pt_ref, *refs, n_pages, page, n_top, past_len, dt):
    del pt_ref
    pages = refs[:n_pages]
    (win_ref, new_ref, q_ref, gate_ref, wp_ref, perm_ref, w1f_ref, pe_ref, w2_ref, kn_ref, map_ref, gs_ref,
     bc_ref, bs_ref, bw_ref, o_ref, x_s, s_s, v_s) = refs[n_pages:]
    q = q_ref[0]
    lanes = q.shape[1]
    n_ch = past_len // CMP_STRIDE

    pairs = KV_WIDTH // 128
    cpp = page // CMP_STRIDE
    for p in range(n_pages):
        for jj in range(2 * pairs):
            f = pages[p][0, 128 * jj:128 * (jj + 1), :].astype(BF16)
            t = jnp.dot(f, perm_ref[...], preferred_element_type=F32).T
            x_s[jj, :, cpp * p:cpp * (p + 1), :] = t.reshape(CMP_STRIDE, cpp, 128)
    comp = []
    for slot in range(2):
        ab = []
        for hp in range(pairs):
            x = jnp.concatenate([x_s[slot * pairs + hp, l] for l in range(CMP_STRIDE)], axis=1).astype(BF16)
            both = jnp.dot(x, wp_ref[slot], preferred_element_type=F32)
            ab += [both[:, :2 * CMP_HIDDEN], both[:, 2 * CMP_HIDDEN:]]
        w1f = w1f_ref[slot]
        pe = (jnp.dot(pe_ref[slot, 0], w1f[:, :CMP_HIDDEN], preferred_element_type=F32)
              + jnp.dot(pe_ref[slot, 1], w1f[:, CMP_HIDDEN:], preferred_element_type=F32))[0:1]
        hid = jnp.concatenate(
            [a[:, :CMP_HIDDEN] + pltpu.roll(a[:, CMP_HIDDEN:], n_ch - 1, 0) + pe for a in ab], axis=1)
        act = hid / (1.0 + jnp.exp(-hid))
        comp.append(jnp.dot(act.astype(BF16), w2_ref[slot], preferred_element_type=F32))
    ck = comp[0]
    lane_g = lax.broadcasted_iota(jnp.int32, ck.shape, 1) // HEAD_DIM
    sq = ck * ck
    scale = jnp.zeros_like(ck)
    for g in range(KV_HEADS):
        ms = jnp.sum(jnp.where(lane_g == g, sq, 0.0), axis=1, keepdims=True) * (1.0 / HEAD_DIM)
        scale = jnp.where(lane_g == g, lax.rsqrt(ms + EPS), scale)
    ck = (ck * scale * kn_ref[...]).astype(BF16)
    cv = comp[1].astype(BF16)

    sc = jnp.dot(ck, q, preferred_element_type=F32) + bc_ref[...]
    m = jnp.max(sc, axis=0, keepdims=True)
    m = jnp.where(m < 0.5 * NEG, 0.0, m)
    p = jnp.exp(sc - m)
    p = p * (1.0 / jnp.maximum(jnp.sum(p, axis=0, keepdims=True), 1e-30))
    ocT = _tn_dot(cv, p.astype(BF16))
    hi = p.astype(BF16)
    lo = (p - hi.astype(F32)).astype(BF16)
    m1 = (jnp.dot(map_ref[...], hi, preferred_element_type=F32)
          + jnp.dot(map_ref[...], lo, preferred_element_type=F32))
    hi = m1.astype(BF16)
    lo = (m1 - hi.astype(F32)).astype(BF16)
    imp = (jnp.dot(hi, gs_ref[...], preferred_element_type=F32)
           + jnp.dot(lo, gs_ref[...], preferred_element_type=F32))
    n_sel_pad = imp.shape[0]
    n_sel = -(-(past_len + dt) // SEL_BLOCK)
    j = lax.broadcasted_iota(jnp.int32, (n_sel_pad, lanes), 0)
    qpos = past_len + lax.broadcasted_iota(jnp.int32, (n_sel_pad, lanes), 1) % dt
    cur = qpos // SEL_BLOCK
    forced = (j == 0) | (j == cur) | (j == cur - 1)
    future = (j * SEL_BLOCK > qpos) | (j >= n_sel)
    v = jnp.where(forced, jnp.inf, jnp.where(future, -jnp.inf, imp))
    v_s[...] = v

    def count(a, cnt):
        va = v_s[pl.ds(a, 1), :]
        ahead = (va > v) | ((va == v) & (j > a))
        return cnt + jnp.where(ahead, 1.0, 0.0)

    rank = lax.fori_loop(0, n_sel, count, jnp.zeros((n_sel_pad, lanes), F32))
    sel = jnp.where((rank < n_top) & (j < n_sel), 0.0, NEG)

    blocks_per_page = page // SEL_BLOCK
    m_run = jnp.full((page, lanes), NEG, F32)
    for pg in range(n_pages):
        kT = pages[pg][0, 2 * KV_WIDTH:3 * KV_WIDTH, :].astype(BF16)
        mask = jnp.concatenate(
            [jnp.broadcast_to(sel[blocks_per_page * pg + bb:blocks_per_page * pg + bb + 1], (SEL_BLOCK, lanes))
             for bb in range(blocks_per_page)], axis=0)
        s = _tn_dot(kT, q) + bs_ref[pg * page:(pg + 1) * page, :] + mask
        s_s[pg * page:(pg + 1) * page, :] = s
        m_run = jnp.maximum(m_run, s)
    nb_past = past_len // SEL_BLOCK
    s_new = (jnp.dot(new_ref[0, 0], q, preferred_element_type=F32) + bs_ref[past_len:past_len + NEW_ROWS, :]
             + sel[nb_past:nb_past + 1])
    m = jnp.maximum(jnp.max(m_run, axis=0, keepdims=True), jnp.max(s_new, axis=0, keepdims=True))
    pn = jnp.exp(s_new - m)
    den = jnp.sum(pn, axis=0, keepdims=True)
    acc = _tn_dot(new_ref[0, 1], pn.astype(BF16))
    for pg in range(n_pages):
        pp = jnp.exp(s_s[pg * page:(pg + 1) * page, :] - m)
        den = den + jnp.sum(pp, axis=0, keepdims=True)
        acc = acc + jnp.dot(pages[pg][0, 3 * KV_WIDTH:4 * KV_WIDTH, :].astype(BF16), pp.astype(BF16),
                            preferred_element_type=F32)
    osT = acc * (1.0 / den)

    wb = win_ref.shape[2]
    s_w = _tn_dot(win_ref[0, 0:KV_WIDTH, :].astype(BF16), q) + bw_ref[0:wb, :]
    s_n = jnp.dot(new_ref[0, 2], q, preferred_element_type=F32) + bw_ref[wb:wb + NEW_ROWS, :]
    m = jnp.maximum(jnp.max(s_w, axis=0, keepdims=True), jnp.max(s_n, axis=0, keepdims=True))
    pw = jnp.exp(s_w - m)
    pn = jnp.exp(s_n - m)
    den = jnp.sum(pw, axis=0, keepdims=True) + jnp.sum(pn, axis=0, keepdims=True)
    acc = (jnp.dot(win_ref[0, KV_WIDTH:2 * KV_WIDTH, :].astype(BF16), pw.astype(BF16), preferred_element_type=F32)
           + _tn_dot(new_ref[0, 3], pn.astype(BF16)))
    owT = acc * (1.0 / den)

    lpg = lanes // KV_HEADS
    o_ref[0] = (gate_ref[0, 0:1, :] * _group_diagonal(ocT, lpg)
                + gate_ref[0, 1:2, :] * _group_diagonal(osT, lpg)
                + gate_ref[0, 2:3, :] * _group_diagonal(owT, lpg))


def _lane_table(rel_bias, dist, valid, dt):
    lanes = np.arange(N_HEADS * dt)
    h, t = lanes // dt, lanes % dt
    return bias_lookup(dist[:, t], valid[:, t], rel_bias[:, jnp.asarray(h)][None])[0]


def nsa_sample_tables(rel_bias, past_len, wb, dt):
    t = np.arange(dt)[None, :]
    n_ch = past_len // CMP_STRIDE
    c = np.arange(n_ch)[:, None]
    dist = past_len + t - (CMP_STRIDE * c + CMP_LEN - 1)
    bc = _lane_table(rel_bias, dist, (c < n_ch - 1) & (dist >= 0), dt)
    s = np.arange(past_len)[:, None]
    n = np.arange(NEW_ROWS)[:, None]
    dist = np.concatenate([past_len + t - s, t - n])
    valid = np.concatenate([np.ones((past_len, dt), bool), (n < dt) & (t - n >= 0)])
    bs = _lane_table(rel_bias, dist, valid, dt)
    l = np.arange(wb)[:, None]
    dist = np.concatenate([wb + t - l, t - n])
    valid = np.concatenate([wb + t - l <= NSA_WINDOW, (n < dt) & (t - n >= 0)])
    bw = _lane_table(rel_bias, dist, valid, dt)
    return bc, bs, bw


def nsa_sample_constants(past_len, dt):
    n_ch = past_len // CMP_STRIDE
    n_sel = -(-(past_len + dt) // SEL_BLOCK)
    n_sel_pad = -(-n_sel // 8) * 8
    c = np.arange(n_ch)[None, :]
    j0 = np.arange(n_sel_pad)[:, None] * SEL_BLOCK
    c_start, c_end = CMP_STRIDE * c, CMP_STRIDE * c + CMP_LEN - 1
    sel_map = (c_start <= j0 + SEL_BLOCK - 1) & (c_end >= j0) & (c < n_ch - 1) & (j0 < n_sel * SEL_BLOCK)
    lanes = np.arange(N_HEADS * dt)
    grp, t = lanes // (REP * dt), lanes % dt
    gsum = (grp[:, None] == grp[None, :]) & (t[:, None] == t[None, :])
    return (jnp.asarray(sel_map.astype(np.float32), dtype=BF16), jnp.asarray(gsum.astype(np.float32), dtype=BF16))


def nsa_sample_attention(cache_pages, page_idx, win_cache, win_base, new_rows, qbd, gates, cw, k_gain_cmp, tables):
    db, n_pages = page_idx.shape
    page = cache_pages.shape[2]
    past_len = n_pages * page
    lanes = qbd.shape[2]
    dt = lanes // N_HEADS
    wb = win_cache.shape[2]
    w1ab, pe, w2 = cw
    w1l = w1ab.reshape(2, CMP_STRIDE, HEAD_DIM, 2 * CMP_HIDDEN)
    wp = jnp.einsum('sldk,gh->slgdhk', w1l, jnp.eye(2, dtype=w1l.dtype)).reshape(
        2, CMP_STRIDE * 128, 4 * CMP_HIDDEN)
    tok = np.arange(page)
    perm = np.zeros((page, page), np.float32)
    perm[tok, (tok % CMP_STRIDE) * (page // CMP_STRIDE) + tok // CMP_STRIDE] = 1.0
    perm = jnp.asarray(perm, dtype=BF16)
    w2bd = jnp.einsum('shd,gk->sghkd', w2, jnp.eye(KV_HEADS, dtype=w2.dtype)).reshape(
        2, KV_HEADS * CMP_HIDDEN, KV_WIDTH)
    kn = jnp.tile(k_gain_cmp.reshape(1, HEAD_DIM), (1, KV_HEADS))
    sel_map, gsum = nsa_sample_constants(past_len, dt)
    bc, bs, bw = tables
    n_sel = -(-(past_len + dt) // SEL_BLOCK)
    assert past_len % SEL_BLOCK == 0 and dt <= SEL_BLOCK and page % SEL_BLOCK == 0

    def const(a):
        nd = a.ndim
        return pl.BlockSpec(a.shape, lambda i, pt: (0,) * nd)

    page_specs = [pl.BlockSpec((1, cache_pages.shape[1], page), functools.partial(lambda p, i, pt: (pt[i, p], 0, 0), p))
                  for p in range(n_pages)]
    consts = [wp, perm, w1ab, pe, w2bd, kn, sel_map, gsum, bc, bs, bw]
    grid_spec = pltpu.PrefetchScalarGridSpec(
        num_scalar_prefetch=1,
        grid=(db,),
        in_specs=page_specs + [
            pl.BlockSpec((1, win_cache.shape[1], wb), lambda i, pt: (win_base + i, 0, 0)),
            pl.BlockSpec((1,) + new_rows.shape[1:], lambda i, pt: (i, 0, 0, 0)),
            pl.BlockSpec((1,) + qbd.shape[1:], lambda i, pt: (i, 0, 0)),
            pl.BlockSpec((1,) + gates.shape[1:], lambda i, pt: (i, 0, 0)),
        ] + [const(a) for a in consts],
        out_specs=pl.BlockSpec((1, HEAD_DIM, lanes), lambda i, pt: (i, 0, 0)),
        scratch_shapes=[pltpu.VMEM((2 * KV_WIDTH // 128, CMP_STRIDE, past_len // CMP_STRIDE, 128), F32),
                        pltpu.VMEM((past_len, lanes), F32),
                        pltpu.VMEM((sel_map.shape[0], lanes), F32)],
    )
    return pl.pallas_call(
        functools.partial(_nsa_sample_kernel, n_pages=n_pages, page=page, n_top=min(SEL_TOP_N, n_sel),
                          past_len=past_len, dt=dt),
        out_shape=jax.ShapeDtypeStruct((db, HEAD_DIM, lanes), F32),
        grid_spec=grid_spec,
        compiler_params=_params("parallel"),
        name="nsa_sample",
    )(page_idx, *([cache_pages] * n_pages), win_cache, new_rows, qbd, gates, *consts)


def _dil_sample_kernel(kv_ref, new_ref, q_ref, b0_ref, b1_ref, b2_ref, o_ref, s_s, *, los, buf_len):
    lanes = q_ref.shape[3]
    chunk = 128
    ms, dens, accs = [], [], []
    for gi, (bias_ref, lo) in enumerate(zip((b0_ref, b1_ref, b2_ref), los)):
        q = q_ref[0, gi]
        n_chunks = (buf_len - lo) // chunk
        m_run = jnp.full((chunk, lanes), NEG, F32)
        for c in range(n_chunks):
            r0 = lo + c * chunk
            s = (_tn_dot(kv_ref[0, 0:KV_WIDTH, r0:r0 + chunk].astype(BF16), q)
                 + bias_ref[c * chunk:(c + 1) * chunk, :])
            s_s[c * chunk:(c + 1) * chunk, :] = s
            m_run = jnp.maximum(m_run, s)
        s_new = (jnp.dot(new_ref[0, 0], q, preferred_element_type=F32)
                 + bias_ref[n_chunks * chunk:n_chunks * chunk + NEW_ROWS, :])
        m = jnp.maximum(jnp.max(m_run, axis=0, keepdims=True), jnp.max(s_new, axis=0, keepdims=True))
        pn = jnp.exp(s_new - m)
        den = jnp.sum(pn, axis=0, keepdims=True)
        acc = _tn_dot(new_ref[0, 1], pn.astype(BF16))
        for c in range(n_chunks):
            r0 = lo + c * chunk
            pp = jnp.exp(s_s[c * chunk:(c + 1) * chunk, :] - m)
            den = den + jnp.sum(pp, axis=0, keepdims=True)
            acc = acc + jnp.dot(kv_ref[0, KV_WIDTH:2 * KV_WIDTH, r0:r0 + chunk].astype(BF16), pp.astype(BF16),
                                preferred_element_type=F32)
        ms.append(m)
        dens.append(den)
        accs.append(acc)
    mx = functools.reduce(jnp.maximum, ms)
    w = [jnp.exp(m - mx) for m in ms]
    num = functools.reduce(jnp.add, [wg * a for wg, a in zip(w, accs)])
    den = functools.reduce(jnp.add, [wg * d for wg, d in zip(w, dens)])
    o_ref[0] = _group_diagonal(num * (1.0 / den), lanes // KV_HEADS)


def dil_sample_tables(rel_bias, buf_len, dt):
    t = np.arange(dt)[None, :]
    n = np.arange(NEW_ROWS)[:, None]
    tables, los = [], []
    for window, dil in DIL_GROUPS:
        lo = max(0, buf_len - window) // 128 * 128
        s = np.arange(lo, buf_len)[:, None]
        dist = np.concatenate([buf_len + t - s, t - n])
        valid = np.concatenate([np.ones((buf_len - lo, dt), bool), (n < dt) & (t - n >= 0)])
        valid = valid & (dist % dil == 0) & (dist <= window)
        tables.append(_lane_table(rel_bias, dist, valid, dt))
        los.append(lo)
    return tables, tuple(los)


def dil_sample_attention(kv_cache, new_rows, qbd3, tables, los):
    db, width, buf_len = kv_cache.shape
    lanes = qbd3.shape[3]
    assert buf_len % 128 == 0
    tspec = [pl.BlockSpec(tb.shape, lambda i: (0, 0)) for tb in tables]
    return pl.pallas_call(
        functools.partial(_dil_sample_kernel, los=los, buf_len=buf_len),
        out_shape=jax.ShapeDtypeStruct((db, HEAD_DIM, lanes), F32),
        grid=(db,),
        in_specs=[
            pl.BlockSpec((1, width, buf_len), lambda i: (i, 0, 0)),
            pl.BlockSpec((1,) + new_rows.shape[1:], lambda i: (i, 0, 0, 0)),
            pl.BlockSpec((1,) + qbd3.shape[1:], lambda i: (i, 0, 0, 0)),
        ] + tspec,
        out_specs=pl.BlockSpec((1, HEAD_DIM, lanes), lambda i: (i, 0, 0)),
        scratch_shapes=[pltpu.VMEM((buf_len, lanes), F32)],
        compiler_params=_params("parallel"),
        name="dil_sample",
    )(kv_cache, new_rows, qbd3, *tables)


def lane_queries(qT, db, dt):
    q = qT.reshape(KV_HEADS, REP, HEAD_DIM, db, dt).transpose(3, 0, 2, 1, 4).reshape(db, KV_HEADS, HEAD_DIM, REP * dt)
    eye = jnp.eye(KV_HEADS, dtype=q.dtype)
    qbd = q[:, :, :, None, :] * eye[None, :, None, :, None]
    return qbd.reshape(db, KV_WIDTH, KV_HEADS * REP * dt).astype(BF16)


def lanes_to_features(o, db, dt):
    o = o.reshape(db, HEAD_DIM, KV_HEADS, REP, dt).transpose(2, 3, 1, 0, 4)
    return o.reshape(N_HEADS * HEAD_DIM, db * dt)


def token_major(xT, slots, b, s, last):
    x = xT.reshape(xT.shape[0], slots, KV_HEADS, HEAD_DIM, b, s)[..., s - last:]
    return x.transpose(0, 4, 5, 1, 2, 3)


def new_token_rows(zT_rows, db, dt):
    n = zT_rows.shape[0] // KV_WIDTH
    r = zT_rows.reshape(n, KV_WIDTH, db, dt).transpose(2, 0, 3, 1)
    return jnp.pad(r, ((0, 0), (0, 0), (0, NEW_ROWS - dt), (0, 0))).astype(BF16)


def kernel(x_prompt, x_sample, cache_nsa_kv, cache_win_kv, cache_dil_kv, page_table, rel_bias, a_attn_norm, a_w_in, a_q_norm, a_k_norm, a_cmp_pe, a_cmp_w1, a_cmp_w2, a_w_out, kv_norm, w_kv_shared, k_norm_shared, b_attn_norm, b_w_q, b_q_norm, b_w_out, mlp_norm, mlp_w1, mlp_w2):
    b, s, d = x_prompt.shape
    db, dt, _ = x_sample.shape
    n_a = a_w_in.shape[0]
    n_b = b_w_q.shape[0]
    n_pool, page = cache_nsa_kv.shape[1:3]
    wb = cache_win_kv.shape[2]
    buf_len = cache_dil_kv.shape[1]
    past_len = page_table.shape[1] * page
    assert N_HEADS * dt == 128
    tables = {}
    xpT = x_prompt.reshape(b * s, d).T
    xsT = x_sample.reshape(db * dt, d).T
    cache_pages = cache_nsa_kv.transpose(0, 1, 3, 4, 5, 2).reshape(n_a * n_pool, -1, page)
    win_cache = cache_win_kv.transpose(0, 1, 3, 4, 5, 2).reshape(n_a * db, -1, wb)
    nsa_tables = nsa_sample_tables(rel_bias, past_len, wb, dt)
    rows_p, rows_s, wins_p, wins_s = [], [], [], []
    for l in range(n_a + n_b):
        w1T = mlp_w1[l].T.astype(BF16)
        w2T = mlp_w2[l].T.astype(BF16)
        if l < n_a:
            w_inT = a_w_in[l].T.astype(BF16)
            norms = jnp.stack([a_q_norm[l], a_k_norm[l][1], a_k_norm[l][2]])[..., None]
            cw = compress_weights(a_cmp_pe[l], a_cmp_w1[l], a_cmp_w2[l])
            woT = a_w_out[l].T.astype(BF16)
            proj = nsa_proj_prompt(xpT, a_attn_norm[l], w_inT, norms, b, s)
            oT, rows, win = nsa_attention_prompt(proj, b, s, rel_bias, cw, a_k_norm[l][0], tables=tables)
            rows_p.append(rows)
            wins_p.append(win)
            xpT = outproj_mlp(xpT, oT, woT, mlp_norm[l], w1T, w2T)
            zs = norm_proj(xsT, a_attn_norm[l], w_inT, norms, NSA_SEGS)
            gates = zs[2560:2560 + N_BRANCHES * N_HEADS].reshape(N_BRANCHES, KV_HEADS, REP, db, dt)
            gates = gates.transpose(3, 0, 1, 2, 4).reshape(db, N_BRANCHES, N_HEADS * dt)
            o = nsa_sample_attention(
                cache_pages, page_table + l * n_pool, win_cache, l * db, new_token_rows(zs[1536:2560], db, dt),
                lane_queries(zs[:1024], db, dt), gates, cw, a_k_norm[l][0], nsa_tables)
            kv6 = zs[1024:2560].T.reshape(db, dt, 6, KV_HEADS, HEAD_DIM)
            rows_s.append(kv6[:, :, 0:4])
            wins_s.append(kv6[:, :, 4:6])
            xsT = outproj_mlp(xsT, lanes_to_features(o, db, dt).astype(BF16), woT, mlp_norm[l], w1T, w2T)
        else:
            i = l - n_a
            if i == 0:
                w_kvT = w_kv_shared.T.astype(BF16)
                kn = k_norm_shared.reshape(1, HEAD_DIM, 1)
                kvpT = norm_proj(xpT, kv_norm, w_kvT, kn, KV_SEGS)
                kvsT = norm_proj(xsT, kv_norm, w_kvT, kn, KV_SEGS)
                kvs = kvsT.T.reshape(db, dt, 2, KV_HEADS, HEAD_DIM)
                dmax = max(w for w, _ in DIL_GROUPS)
                new_dil_p = token_major(kvpT[None], 2, b, s, min(dmax, s))[0]
                new_dil_s = jnp.concatenate([cache_dil_kv, kvs], axis=1)[:, -min(dmax, buf_len + dt):]
                dil_cache = cache_dil_kv.transpose(0, 2, 3, 4, 1).reshape(db, -1, buf_len)
                dil_new = new_token_rows(kvsT, db, dt)
                dil_tables, dil_los = dil_sample_tables(rel_bias, buf_len, dt)
            w_qT = b_w_q[i].T.astype(BF16)
            qn = b_q_norm[i][..., None]
            woT = b_w_out[i].T.astype(BF16)
            qT = norm_proj(xpT, b_attn_norm[i], w_qT, qn, DIL_Q_SEGS)
            oT = dil_attention_prompt(qT, kvpT, b, s, rel_bias, tables=tables)
            xpT = outproj_mlp(xpT, oT, woT, mlp_norm[l], w1T, w2T)
            qs = norm_proj(xsT, b_attn_norm[i], w_qT, qn, DIL_Q_SEGS)
            qbd3 = jnp.stack([lane_queries(qs[1024 * gi:1024 * (gi + 1)], db, dt) for gi in range(len(DIL_GROUPS))], axis=1)
            o = dil_sample_attention(dil_cache, dil_new, qbd3, dil_tables, dil_los)
            xsT = outproj_mlp(xsT, lanes_to_features(o, db, dt).astype(BF16), woT, mlp_norm[l], w1T, w2T)
    return (xpT.T.reshape(b, s, d), xsT.T.reshape(db, dt, d),
            token_major(jnp.stack(rows_p), 4, b, s, s), jnp.stack(rows_s),
            token_major(jnp.stack(wins_p), 2, b, s, min(NSA_WINDOW, s)),
            jnp.concatenate([cache_win_kv, jnp.stack(wins_s)], axis=2)[:, :, -min(NSA_WINDOW, wb + dt):],
            new_dil_p, new_dil_s)
```

```python
import functools
import math

import numpy as np
import jax
import jax.numpy as jnp
from jax import lax
from jax.experimental import pallas as pl
from jax.experimental.pallas import tpu as pltpu

F32 = jnp.float32
BF16 = jnp.bfloat16

D_MODEL = 1024
N_HEADS = 16
HEAD_DIM = 64
KV_HEADS = 4
REP = N_HEADS // KV_HEADS
GROUP_ROWS = REP * HEAD_DIM
KV_WIDTH = KV_HEADS * HEAD_DIM
D_FF = 4 * D_MODEL
EPS = 1e-6
ATTN_SCALE = HEAD_DIM ** -0.5
N_BUCKETS = 32
MAX_DISTANCE = 2048
CMP_LEN = 32
CMP_STRIDE = 16
CMP_HIDDEN = 2 * HEAD_DIM
SEL_BLOCK = 64
SEL_TOP_N = 16
NSA_WINDOW = 512
N_BRANCHES = 3
NSA_IN = N_HEADS * HEAD_DIM + 6 * KV_HEADS * HEAD_DIM + N_BRANCHES * N_HEADS
DIL_GROUPS = ((128, 1), (512, 4), (2048, 16))
PAGE_SIZE = 128

NEG = -1e30
LOG2E = math.log2(math.e)
VMEM_LIMIT = 56 * 1024 * 1024
BIAS_TABLE_LEN = 2048


def _bucket_of_distance(d):
    max_exact = N_BUCKETS // 2
    d = np.maximum(d, 0)
    ratio = np.log(np.maximum(d, 1).astype(np.float32) / np.float32(max_exact)) / np.float32(
        math.log(MAX_DISTANCE / max_exact))
    large = max_exact + (ratio * np.float32(N_BUCKETS - max_exact)).astype(np.int32)
    return np.where(d < max_exact, d, np.minimum(large, N_BUCKETS - 1)).astype(np.int32)


_BUCKETS = _bucket_of_distance(np.arange(BIAS_TABLE_LEN))
BIAS_CONST_FROM = int(np.argmax(_BUCKETS == N_BUCKETS - 1))
assert np.all(_BUCKETS[BIAS_CONST_FROM:] == N_BUCKETS - 1)


def _params(*sem):
    return pltpu.CompilerParams(dimension_semantics=sem, vmem_limit_bytes=VMEM_LIMIT)


def _token_tile(t, tm):
    tm = min(tm, t)
    assert t % tm == 0, (t, tm)
    return tm


def _tn_dot(a, b):
    return lax.dot_general(a, b, (((0,), (0,)), ((), ())), preferred_element_type=F32)


def _bias_lookup_kernel(bkt_ref, tab_ref, o_ref):
    bkt = bkt_ref[...]
    v = jnp.full(bkt.shape, NEG, F32)
    for bb in range(N_BUCKETS):
        v = jnp.where(bkt == bb, tab_ref[0, bb:bb + 1, :], v)
    o_ref[0] = v


def bias_lookup(dist, valid, tab):
    rows, lanes = dist.shape
    bkt = np.where(valid, _BUCKETS[np.clip(dist, 0, BIAS_TABLE_LEN - 1)], N_BUCKETS).astype(np.int32)
    rb = next(r for r in range(min(rows, 512) // 8 * 8, 0, -8) if rows % r == 0)
    nh = tab.shape[0]
    return pl.pallas_call(
        _bias_lookup_kernel,
        out_shape=jax.ShapeDtypeStruct((nh, rows, lanes), F32),
        grid=(rows // rb, nh),
        in_specs=[pl.BlockSpec((rb, lanes), lambda r, h: (r, 0)),
                  pl.BlockSpec((1, N_BUCKETS, lanes), lambda r, h: (h, 0, 0))],
        out_specs=pl.BlockSpec((1, rb, lanes), lambda r, h: (h, r, 0)),
        compiler_params=_params("parallel", "parallel"),
        name="bias_lookup",
    )(jnp.asarray(bkt), tab)


def _cached(tables, key, build):
    if tables is None:
        return build()
    if key not in tables:
        tables[key] = build()
    return tables[key]


def head_rows(rel_bias, lanes):
    return jnp.broadcast_to((rel_bias.T * LOG2E)[:, :, None], (N_HEADS, N_BUCKETS, lanes))


def _proj_kernel(x_ref, g_ref, w_ref, n_ref, o_ref, *, segs):
    x = x_ref[...]
    ms = jnp.mean(x * x, axis=0, keepdims=True)
    xn = (x * lax.rsqrt(ms + EPS) * g_ref[...]).astype(BF16)
    tm = x.shape[1]
    for r0, nr, kind, ni, scale in segs:
        z = jnp.dot(w_ref[r0:r0 + nr, :], xn, preferred_element_type=F32)
        if kind == "norm":
            z3 = z.reshape(nr // HEAD_DIM, HEAD_DIM, tm)
            hs = jnp.mean(z3 * z3, axis=1, keepdims=True)
            z = (z3 * lax.rsqrt(hs + EPS) * n_ref[ni][None] * scale).reshape(nr, tm)
        elif kind == "sigmoid":
            z = 1.0 / (1.0 + jnp.exp(-z))
        o_ref[r0:r0 + nr, :] = z


def norm_proj(xT, gain, wT, norms, segs, tm=512):
    d, t = xT.shape
    tm = _token_tile(t, tm)
    n = wT.shape[0]
    return pl.pallas_call(
        functools.partial(_proj_kernel, segs=tuple(segs)),
        out_shape=jax.ShapeDtypeStruct((n, t), F32),
        grid=(t // tm,),
        in_specs=[
            pl.BlockSpec((d, tm), lambda i: (0, i)),
            pl.BlockSpec((d, 1), lambda i: (0, 0)),
            pl.BlockSpec((n, d), lambda i: (0, 0)),
            pl.BlockSpec(norms.shape, lambda i: (0, 0, 0)),
        ],
        out_specs=pl.BlockSpec((n, tm), lambda i: (0, i)),
        compiler_params=_params("parallel"),
        name="norm_proj",
    )(xT, gain.reshape(d, 1), wT, norms)


def _nsa_proj_kernel(x_ref, g_ref, w_ref, n_ref, q_ref, rows_ref, gate_ref, ks_ref, vs_ref, kw_ref, vw_ref, *, t):
    x = x_ref[...]
    ms = jnp.mean(x * x, axis=0, keepdims=True)
    xn = (x * lax.rsqrt(ms + EPS) * g_ref[...]).astype(BF16)
    tm = x.shape[1]

    def proj(r0, nr):
        return jnp.dot(w_ref[r0:r0 + nr, :], xn, preferred_element_type=F32)

    def head_norm(z, ni, scale):
        z3 = z.reshape(z.shape[0] // HEAD_DIM, HEAD_DIM, tm)
        hs = jnp.mean(z3 * z3, axis=1, keepdims=True)
        return (z3 * lax.rsqrt(hs + EPS) * n_ref[ni][None] * scale).reshape(z.shape)

    half = N_HEADS * HEAD_DIM // 2
    for c in range(2):
        q_ref[c * half:(c + 1) * half, :] = head_norm(proj(c * half, half), 0, ATTN_SCALE * LOG2E).astype(BF16)
    base = N_HEADS * HEAD_DIM
    rows_ref[0:2 * KV_WIDTH, :] = proj(base, 2 * KV_WIDTH)
    kv = [head_norm(proj(base + 2 * KV_WIDTH, KV_WIDTH), 1, 1.0), proj(base + 3 * KV_WIDTH, KV_WIDTH),
          head_norm(proj(base + 4 * KV_WIDTH, KV_WIDTH), 2, 1.0), proj(base + 5 * KV_WIDTH, KV_WIDTH)]
    for idx, z in enumerate(kv):
        rows_ref[(2 + idx) * KV_WIDTH:(3 + idx) * KV_WIDTH, :] = z
    gate_ref[...] = 1.0 / (1.0 + jnp.exp(-proj(base + 6 * KV_WIDTH, N_BRANCHES * N_HEADS)))
    r = lax.broadcasted_iota(jnp.int32, (SEL_ROWS, t), 0)
    col = lax.broadcasted_iota(jnp.int32, (SEL_ROWS, t), 1)
    block_rows = jnp.where(col // SEL_BLOCK == r, 1.0, 0.0).astype(BF16)
    ones_rows = jnp.where(r == 0, 1.0, 0.0).astype(BF16)
    for (ref, extra), z in zip(((ks_ref, block_rows), (vs_ref, ones_rows), (kw_ref, None), (vw_ref, ones_rows)), kv):
        for j in range(tm // t):
            for g in range(KV_HEADS):
                ref[0, g, j, 0:HEAD_DIM, :] = z[g * HEAD_DIM:(g + 1) * HEAD_DIM, j * t:(j + 1) * t].astype(BF16)
                if extra is not None:
                    ref[0, g, j, HEAD_DIM:HEAD_DIM + SEL_ROWS, :] = extra


def nsa_proj_prompt(xT, gain, wT, norms, b, s, t=256, tm=512):
    d, tt = xT.shape
    nps = s // tm
    assert tt == b * s and s % tm == 0 and tm % t == 0
    n = wT.shape[0]
    col = lambda rows, dt: (jax.ShapeDtypeStruct((rows, tt), dt), pl.BlockSpec((rows, tm), lambda i: (0, i)))
    tile = lambda rows: (jax.ShapeDtypeStruct((b, KV_HEADS, s // t, rows, t), BF16),
                         pl.BlockSpec((1, KV_HEADS, tm // t, rows, t), lambda i: (i // nps, 0, i % nps, 0, 0)))
    outs = [col(N_HEADS * HEAD_DIM, BF16), col(6 * KV_WIDTH, F32), col(N_BRANCHES * N_HEADS, F32),
            tile(HEAD_DIM + SEL_ROWS), tile(HEAD_DIM + SEL_ROWS), tile(HEAD_DIM), tile(HEAD_DIM + SEL_ROWS)]
    return pl.pallas_call(
        functools.partial(_nsa_proj_kernel, t=t),
        out_shape=tuple(o[0] for o in outs),
        grid=(tt // tm,),
        in_specs=[
            pl.BlockSpec((d, tm), lambda i: (0, i)),
            pl.BlockSpec((d, 1), lambda i: (0, 0)),
            pl.BlockSpec((n, d), lambda i: (0, 0)),
            pl.BlockSpec(norms.shape, lambda i: (0, 0, 0)),
        ],
        out_specs=tuple(o[1] for o in outs),
        compiler_params=_params("parallel"),
        name="nsa_proj",
    )(xT, gain.reshape(d, 1), wT, norms)


def _outmlp_kernel(x_ref, o_ref, wo_ref, g_ref, w1_ref, w2_ref, y_ref, x1_s, xn_s, acc_s):
    f = pl.program_id(1)

    @pl.when(f == 0)
    def _():
        x1 = x_ref[...] + jnp.dot(wo_ref[...], o_ref[...], preferred_element_type=F32)
        x1_s[...] = x1
        ms = jnp.mean(x1 * x1, axis=0, keepdims=True)
        xn_s[...] = (x1 * lax.rsqrt(ms + EPS) * g_ref[...]).astype(BF16)
        acc_s[...] = jnp.zeros_like(acc_s)

    h = jnp.maximum(jnp.dot(w1_ref[...], xn_s[...], preferred_element_type=F32), 0.0)
    acc_s[...] += jnp.dot(w2_ref[...], (h * h).astype(BF16), preferred_element_type=F32)

    @pl.when(f == pl.num_programs(1) - 1)
    def _():
        y_ref[...] = x1_s[...] + acc_s[...]


def outproj_mlp(xT, oT, woT, gain, w1T, w2T, tm=1024, tf=1024):
    d, t = xT.shape
    tm = _token_tile(t, tm)
    dff = w1T.shape[0]
    return pl.pallas_call(
        _outmlp_kernel,
        out_shape=jax.ShapeDtypeStruct((d, t), F32),
        grid=(t // tm, dff // tf),
        in_specs=[
            pl.BlockSpec((d, tm), lambda i, f: (0, i)),
            pl.BlockSpec((d, tm), lambda i, f: (0, i)),
            pl.BlockSpec((d, d), lambda i, f: (0, 0)),
            pl.BlockSpec((d, 1), lambda i, f: (0, 0)),
            pl.BlockSpec((tf, d), lambda i, f: (f, 0)),
            pl.BlockSpec((d, tf), lambda i, f: (0, f)),
        ],
        out_specs=pl.BlockSpec((d, tm), lambda i, f: (0, i)),
        scratch_shapes=[pltpu.VMEM((d, tm), F32), pltpu.VMEM((d, tm), BF16), pltpu.VMEM((d, tm), F32)],
        compiler_params=_params("parallel", "arbitrary"),
        name="outproj_mlp",
    )(xT, oT, woT, gain.reshape(d, 1), w1T, w2T)


def _compress_kernel(ch_ref, w1_ref, pe_ref, w2_ref, kn_ref, o_ref):
    slot = pl.program_id(0)
    bn, c, _ = ch_ref.shape[1:]
    w1 = w1_ref[0]
    ab = jnp.dot(ch_ref[0].reshape(bn * c, CMP_STRIDE * HEAD_DIM), w1, preferred_element_type=F32)
    pe_a = jnp.dot(pe_ref[0, 0], w1[:, :CMP_HIDDEN], preferred_element_type=F32)[0:1]
    pe_b = jnp.dot(pe_ref[0, 1], w1[:, CMP_HIDDEN:], preferred_element_type=F32)[0:1]
    nxt = pltpu.roll(ab[:, CMP_HIDDEN:], bn * c - 1, 0)
    hid = ab[:, :CMP_HIDDEN] + nxt + (pe_a + pe_b)
    act = hid / (1.0 + jnp.exp(-hid))
    out = jnp.dot(act.astype(BF16), w2_ref[0], preferred_element_type=F32)
    ms = jnp.mean(out * out, axis=-1, keepdims=True)
    normed = out * lax.rsqrt(ms + EPS) * kn_ref[...]
    o_ref[0] = jnp.where(slot == 0, normed, out).reshape(bn, c, HEAD_DIM)


def compress(ch, w1ab, pe, w2, k_gain, bn):
    _, nb, c, w = ch.shape
    return pl.pallas_call(
        _compress_kernel,
        out_shape=jax.ShapeDtypeStruct((2, nb, c, HEAD_DIM), F32),
        grid=(2, nb // bn),
        in_specs=[
            pl.BlockSpec((1, bn, c, w), lambda s, i: (s, i, 0, 0)),
            pl.BlockSpec((1, w, 2 * CMP_HIDDEN), lambda s, i: (s, 0, 0)),
            pl.BlockSpec((1, 2, 16, w), lambda s, i: (s, 0, 0, 0)),
            pl.BlockSpec((1, CMP_HIDDEN, HEAD_DIM), lambda s, i: (s, 0, 0)),
            pl.BlockSpec((1, HEAD_DIM), lambda s, i: (0, 0)),
        ],
        out_specs=pl.BlockSpec((1, bn, c, HEAD_DIM), lambda s, i: (s, i, 0, 0)),
        compiler_params=_params("parallel", "parallel"),
        name="nsa_compress",
    )(ch, w1ab, pe, w2, k_gain.reshape(1, HEAD_DIM))


def compress_weights(cmp_pe, cmp_w1, cmp_w2):
    half = CMP_STRIDE * HEAD_DIM
    w1 = cmp_w1.reshape(2, 2, half, CMP_HIDDEN)
    w1ab = jnp.concatenate([w1[:, 0], w1[:, 1]], axis=-1).astype(BF16)
    pe = jnp.broadcast_to(cmp_pe.reshape(2, 2, 1, half), (2, 2, 16, half)).astype(BF16)
    return w1ab, pe, cmp_w2.astype(BF16)


CMP_CHUNK = 128


def _cmp_topk_kernel(q_ref, ck_ref, cv_ref, map_ref, bias_ref, oc_ref, sel_ref, v_s, s_s, cnt_s, *,
                     tq, n_top, k_const):
    i = pl.program_id(2)
    c = ck_ref.shape[2]
    n_sel = map_ref.shape[0]
    n_chunks = c // CMP_CHUNK
    ms = []
    for r in range(REP):
        q = q_ref[r * HEAD_DIM:(r + 1) * HEAD_DIM, :]
        cm = None
        for jc in range(n_chunks):
            k0 = i * (tq // CMP_STRIDE) + jc * CMP_CHUNK
            start = pl.multiple_of(jnp.minimum(k0, k_const), 8)
            s = (jnp.dot(ck_ref[0, 0, jc * CMP_CHUNK:(jc + 1) * CMP_CHUNK, :], q, preferred_element_type=F32)
                 + bias_ref[r, pl.ds(start, CMP_CHUNK), :])
            s_s[r, jc * CMP_CHUNK:(jc + 1) * CMP_CHUNK, :] = s
            mx = jnp.max(s, axis=0, keepdims=True)
            cm = mx if cm is None else jnp.maximum(cm, mx)
        ms.append(jnp.where(cm < 0.5 * NEG, 0.0, cm))
    psum = None
    for r in range(REP):
        p = jnp.exp2(s_s[r] - ms[r])
        p = p * (1.0 / jnp.maximum(jnp.sum(p, axis=0, keepdims=True), 1e-30))
        oc_ref[r * HEAD_DIM:(r + 1) * HEAD_DIM, :] = jnp.dot(
            cv_ref[0, 0], p.astype(BF16), preferred_element_type=F32)
        psum = p if psum is None else psum + p
    hi = psum.astype(BF16)
    lo = (psum - hi.astype(F32)).astype(BF16)
    imp = (jnp.dot(map_ref[...], hi, preferred_element_type=F32)
           + jnp.dot(map_ref[...], lo, preferred_element_type=F32))
    qpos = i * tq + lax.broadcasted_iota(jnp.int32, (n_sel, tq), 1)
    j = lax.broadcasted_iota(jnp.int32, (n_sel, tq), 0)
    cur = qpos // SEL_BLOCK
    forced = (j == 0) | (j == cur) | (j == cur - 1)
    future = j * SEL_BLOCK > qpos
    v = jnp.where(forced, jnp.inf, jnp.where(future, -jnp.inf, imp))
    v_s[...] = v
    groups = n_sel // 8
    row = lax.broadcasted_iota(jnp.int32, (8, 128), 0)
    cnt_s[...] = jnp.zeros(cnt_s.shape, F32)
    newest = (i * tq + tq - 1) // SEL_BLOCK
    for ga in range(groups):
        @pl.when(8 * ga <= newest)
        def _(ga=ga):
            for lt in range(tq // 128):
                lanes = slice(lt * 128, (lt + 1) * 128)
                vg = [v_s[8 * gi:8 * gi + 8, lanes] for gi in range(groups)]
                cnt = [cnt_s[8 * gi:8 * gi + 8, lanes] for gi in range(groups)]
                for a in range(8 * ga, 8 * ga + 8):
                    va = v_s[a:a + 1, lanes]
                    for gi in range(groups):
                        if 8 * gi > a:
                            ahead = va >= vg[gi]
                        elif 8 * gi + 7 < a:
                            ahead = va > vg[gi]
                        else:
                            ahead = (va > vg[gi]) | ((va == vg[gi]) & (row > a - 8 * gi))
                        cnt[gi] = cnt[gi] + jnp.where(ahead, 1.0, 0.0)
                for gi in range(groups):
                    cnt_s[8 * gi:8 * gi + 8, lanes] = cnt[gi]

    for lt in range(tq // 128):
        lanes = slice(lt * 128, (lt + 1) * 128)
        bpt = n_sel // sel_ref.shape[2]
        for gi in range(groups):
            mask = jnp.where(cnt_s[8 * gi:8 * gi + 8, lanes] < n_top, 0.0, NEG)
            for hh in range(8 // bpt):
                part = mask if hh == 0 else pltpu.roll(mask, 8 - hh * bpt, 0)
                part = jnp.where(row < bpt, part, 0.0)
                sel_ref[0, 0, gi * (8 // bpt) + hh, :, lanes] = jnp.concatenate(
                    [part, jnp.zeros((SEL_ROWS - 8, 128), F32)], axis=0).astype(BF16)


def cmp_strip(rel_bias, c, tq, rows):
    k = np.arange(rows)[:, None]
    tl = np.arange(tq)[None, :]
    d = CMP_STRIDE * (k - (c - 1)) + tl - (CMP_LEN - 1)
    return bias_lookup(d, d >= 0, head_rows(rel_bias, tq))


def cmp_topk(qT, ckr, cvTr, rel_bias, b, s, tq=256, key_tile=256, tables=None):
    c = s // CMP_STRIDE
    n_sel = s // SEL_BLOCK
    nkt = s // key_tile
    assert 8 % (key_tile // SEL_BLOCK) == 0 and n_sel % 8 == 0
    n_top = min(SEL_TOP_N, n_sel)
    nq = s // tq
    k_const = (c - 1) + -(-(BIAS_CONST_FROM + CMP_LEN - 1) // CMP_STRIDE)
    k_const = -(-k_const // 8) * 8
    strip = _cached(tables, ("cmp", c, tq), lambda: cmp_strip(rel_bias, c, tq, k_const + CMP_CHUNK))
    cc = np.arange(c)[::-1]
    c_end = cc * CMP_STRIDE + CMP_LEN - 1
    c_start = c_end - CMP_LEN + 1
    j0 = np.arange(n_sel)[:, None] * SEL_BLOCK
    sel_map = ((c_start[None] <= j0 + SEL_BLOCK - 1) & (c_end[None] >= j0) & (cc[None] < c - 1))
    sel_map = jnp.asarray(sel_map.astype(np.float32), dtype=BF16)
    return pl.pallas_call(
        functools.partial(_cmp_topk_kernel, tq=tq, n_top=n_top, k_const=k_const),
        out_shape=(jax.ShapeDtypeStruct((N_HEADS * HEAD_DIM, b * s), F32),
                   jax.ShapeDtypeStruct((b, KV_HEADS, nkt, SEL_ROWS, s), BF16)),
        grid=(KV_HEADS, b, nq),
        in_specs=[
            pl.BlockSpec((GROUP_ROWS, tq), lambda g, n, i: (g, n * nq + i)),
            pl.BlockSpec((1, 1, c, HEAD_DIM), lambda g, n, i: (n, g, 0, 0)),
            pl.BlockSpec((1, 1, HEAD_DIM, c), lambda g, n, i: (n, g, 0, 0)),
            pl.BlockSpec((n_sel, c), lambda g, n, i: (0, 0)),
            pl.BlockSpec((REP, strip.shape[1], tq), lambda g, n, i: (g, 0, 0)),
        ],
        out_specs=(pl.BlockSpec((GROUP_ROWS, tq), lambda g, n, i: (g, n * nq + i)),
                   pl.BlockSpec((1, 1, nkt, SEL_ROWS, tq), lambda g, n, i: (n, g, 0, 0, i))),
        scratch_shapes=[pltpu.VMEM((n_sel, tq), F32), pltpu.VMEM((REP, c, tq), F32), pltpu.VMEM((n_sel, tq), F32)],
        compiler_params=_params("parallel", "parallel", "parallel"),
        name="nsa_cmp_topk",
    )(qT, ckr, cvTr, sel_map, strip)


SEL_ROWS = 16


def _flash_kernel(*refs, tq, n_tiles_max, n_delta, use_sel):
    nh = REP
    if use_sel:
        q_ref, k_ref, v_ref, bias_ref, sel_ref = refs[:5]
        outs = refs[5:]
    else:
        q_ref, k_ref, v_ref, bias_ref = refs[:4]
        sel_ref = None
        outs = refs[4:]
    o_ref = outs[0]
    accs, s_bufs = outs[1:1 + nh], outs[1 + nh:]
    i = pl.program_id(2)
    n_tiles = jnp.minimum(i + 1, n_tiles_max)
    for acc in accs:
        acc[...] = jnp.zeros(acc.shape, F32)

    def scores(jt, s_buf):
        kt = jnp.maximum(i - jt, 0)
        kT = k_ref[0, 0, kt]
        tile = jnp.where(jt < n_tiles, jnp.minimum(jt, n_delta - 1), n_delta)
        cms = []
        for r in range(nh):
            q = q_ref[r * HEAD_DIM:(r + 1) * HEAD_DIM, :]
            if use_sel:
                q = jnp.concatenate([q, sel_ref[0, 0, kt]], axis=0)
            s = _tn_dot(kT, q) + bias_ref[r, tile]
            s_buf[r] = s
            cms.append(jnp.max(s, axis=0, keepdims=True))
        return tuple(cms)

    def accumulate(jt, s_buf, cms, ms):
        v = v_ref[0, 0, jnp.maximum(i - jt, 0)]
        new_m = []
        for r in range(nh):
            m_new = jnp.maximum(ms[r], cms[r])
            alpha = jnp.exp2(ms[r] - m_new)
            p = jnp.exp2((s_buf[r] - m_new).astype(BF16))
            accs[r][...] = alpha * accs[r][...] + jnp.dot(v, p, preferred_element_type=F32)
            new_m.append(m_new)
        return tuple(new_m)

    def pair(jj, carry):
        ms, cm0 = carry
        cm1 = scores(2 * jj + 1, s_bufs[1])
        ms = accumulate(2 * jj, s_bufs[0], cm0, ms)
        cm0 = scores(2 * jj + 2, s_bufs[0])
        ms = accumulate(2 * jj + 1, s_bufs[1], cm1, ms)
        return ms, cm0

    init = (tuple(jnp.full((1, tq), NEG, F32) for _ in range(nh)), scores(0, s_bufs[0]))
    ms, cm0 = lax.fori_loop(0, n_tiles // 2, pair, init)

    @pl.when(n_tiles % 2 == 1)
    def _():
        accumulate(n_tiles - 1, s_bufs[0], cm0, ms)

    for r in range(nh):
        o_ref[r * HEAD_DIM:(r + 1) * HEAD_DIM, :] = (
            accs[r][0:HEAD_DIM, :] * (1.0 / accs[r][HEAD_DIM:HEAD_DIM + 1, :]))


def flash_bias_tiles(rel_bias, t, n_delta, dil, window):
    j = np.arange(n_delta + 1)[:, None, None]
    sl = np.arange(t)[None, :, None]
    tl = np.arange(t)[None, None, :]
    d = j * t + tl - sl
    ok = (d >= 0) & (j < n_delta)
    if window is not None:
        ok = ok & (d <= window)
    tiles = bias_lookup((d * dil).reshape(-1, t), ok.reshape(-1, t), head_rows(rel_bias, t))
    return tiles.reshape(N_HEADS, n_delta + 1, t, t)


def flash_attention(qT, kT, vT, rel_bias, nseq, seqlen, sel=None, window=None, dil=1, tables=None):
    nkt, tk = kT.shape[2], kT.shape[4]
    tq = tk
    assert seqlen == nkt * tk
    nq = nkt
    if window is None:
        n_delta = -(-(BIAS_CONST_FROM + tk - 1) // tk) + 1
        n_tiles_max = nkt
    else:
        n_tiles_max = (window + tk - 1) // tk + 1
        n_delta = n_tiles_max
    tiles = _cached(tables, ("flash", tk, n_delta, dil, window),
                    lambda: flash_bias_tiles(rel_bias, tk, n_delta, dil, window))
    h_rows = qT.shape[0]
    use_sel = sel is not None
    in_specs = [
        pl.BlockSpec((GROUP_ROWS, tq), lambda g, n, i: (g, n * nq + i)),
        pl.BlockSpec((1, 1) + kT.shape[2:], lambda g, n, i: (n, g, 0, 0, 0)),
        pl.BlockSpec((1, 1) + vT.shape[2:], lambda g, n, i: (n, g, 0, 0, 0)),
        pl.BlockSpec((REP, n_delta + 1, tk, tq), lambda g, n, i: (g, 0, 0, 0)),
    ]
    args = [qT, kT, vT, tiles]
    if use_sel:
        in_specs.append(pl.BlockSpec((1, 1) + sel.shape[2:4] + (tq,), lambda g, n, i: (n, g, 0, 0, i)))
        args.append(sel)
    return pl.pallas_call(
        functools.partial(_flash_kernel, tq=tq, n_tiles_max=n_tiles_max, n_delta=n_delta, use_sel=use_sel),
        out_shape=jax.ShapeDtypeStruct((h_rows, nseq * seqlen), F32),
        grid=(KV_HEADS, nseq, nq),
        in_specs=in_specs,
        out_specs=pl.BlockSpec((GROUP_ROWS, tq), lambda g, n, i: (g, n * nq + i)),
        scratch_shapes=([pltpu.VMEM((vT.shape[3], tq), F32) for _ in range(REP)]
                        + [pltpu.VMEM((REP, tk, tq), F32) for _ in range(2)]),
        compiler_params=_params("parallel", "parallel", "parallel"),
        name="flash_sel" if use_sel else ("flash_band" if window is not None else "flash_causal"),
    )(*args)


def _band_kernel(q_ref, k_ref, v_ref, bias_ref, o_ref, mo_ref, lo_ref, *, t, n_tiles, sub):
    step = pl.program_id(2)
    for u in range(sub):
        i = step * sub + u
        cols = slice(u * t, (u + 1) * t)
        kts = [jnp.maximum(i - jt, 0) for jt in range(n_tiles)]
        tiles = [jnp.where(jt <= i, jt, n_tiles) for jt in range(n_tiles)]
        kTs = [k_ref[0, 0, kt] for kt in kts]
        vTs = [v_ref[0, 0, kt] for kt in kts]
        for r in range(REP):
            rows = slice(r * HEAD_DIM, (r + 1) * HEAD_DIM)
            q = q_ref[rows, cols]
            s = [_tn_dot(kTs[jt], q) + bias_ref[r, tiles[jt]] for jt in range(n_tiles)]
            m = functools.reduce(jnp.maximum, [jnp.max(x, axis=0, keepdims=True) for x in s])
            p = [jnp.exp2(x - m) for x in s]
            den = functools.reduce(jnp.add, [jnp.sum(x, axis=0, keepdims=True) for x in p])
            acc = functools.reduce(jnp.add, [jnp.dot(vTs[jt], p[jt].astype(BF16), preferred_element_type=F32)
                                             for jt in range(n_tiles)])
            o_ref[rows, cols] = acc
            mo_ref[0, r:r + 1, cols] = m
            lo_ref[0, r:r + 1, cols] = den


def band_attention(qT, kT, vT, rel_bias, nseq, seqlen, window, dil, sub=8, tables=None):
    nkt, t = kT.shape[2], kT.shape[4]
    n_tiles = (window + t - 1) // t + 1
    sub = min(sub, nkt)
    assert seqlen == nkt * t and nkt % sub == 0
    nq = nkt // sub
    tiles = _cached(tables, ("flash", t, n_tiles, dil, window),
                    lambda: flash_bias_tiles(rel_bias, t, n_tiles, dil, window))
    o_spec = pl.BlockSpec((GROUP_ROWS, sub * t), lambda g, n, i: (g, n * nq + i))
    st_spec = pl.BlockSpec((1, REP, sub * t), lambda g, n, i: (g, 0, n * nq + i))
    st_shape = jax.ShapeDtypeStruct((KV_HEADS, REP, nseq * seqlen), F32)
    return pl.pallas_call(
        functools.partial(_band_kernel, t=t, n_tiles=n_tiles, sub=sub),
        out_shape=(jax.ShapeDtypeStruct((qT.shape[0], nseq * seqlen), F32), st_shape, st_shape),
        grid=(KV_HEADS, nseq, nq),
        in_specs=[
            pl.BlockSpec((GROUP_ROWS, sub * t), lambda g, n, i: (g, n * nq + i)),
            pl.BlockSpec((1, 1) + kT.shape[2:], lambda g, n, i: (n, g, 0, 0, 0)),
            pl.BlockSpec((1, 1) + vT.shape[2:], lambda g, n, i: (n, g, 0, 0, 0)),
            pl.BlockSpec((REP, n_tiles + 1, t, t), lambda g, n, i: (g, 0, 0, 0)),
        ],
        out_specs=(o_spec, st_spec, st_spec),
        compiler_params=_params("parallel", "parallel", "parallel"),
        name="band_attention",
    )(qT, kT, vT, tiles)


def _nsa_combine_kernel(oc_ref, os_ref, ow_ref, g_ref, o_ref):
    for h in range(N_HEADS):
        rows = slice(h * HEAD_DIM, (h + 1) * HEAD_DIM)
        o = (g_ref[h:h + 1, :] * oc_ref[rows, :]
             + g_ref[N_HEADS + h:N_HEADS + h + 1, :] * os_ref[rows, :]
             + g_ref[2 * N_HEADS + h:2 * N_HEADS + h + 1, :] * ow_ref[rows, :])
        o_ref[rows, :] = o.astype(BF16)


def nsa_combine(ocT, osT, owT, gatesT, tm=512):
    d, t = ocT.shape
    tm = _token_tile(t, tm)
    big = pl.BlockSpec((d, tm), lambda i: (0, i))
    return pl.pallas_call(
        _nsa_combine_kernel,
        out_shape=jax.ShapeDtypeStruct((d, t), BF16),
        grid=(t // tm,),
        in_specs=[big, big, big, pl.BlockSpec((N_BRANCHES * N_HEADS, tm), lambda i: (0, i))],
        out_specs=big,
        compiler_params=_params("parallel"),
        name="nsa_combine",
    )(ocT, osT, owT, gatesT)


def _dil_merge_kernel(*refs):
    ng = (len(refs) - 3)
    acc_refs, (m_ref, l_ref, o_ref) = refs[:ng], refs[ng:]
    mx = functools.reduce(jnp.maximum, [m_ref[g] for g in range(ng)])
    w = [jnp.exp2(m_ref[g] - mx) for g in range(ng)]
    den = functools.reduce(jnp.add, [w[g] * l_ref[g] for g in range(ng)])
    inv = 1.0 / den
    for h in range(N_HEADS):
        rows = slice(h * HEAD_DIM, (h + 1) * HEAD_DIM)
        num = functools.reduce(jnp.add, [w[g][h:h + 1, :] * acc_refs[g][rows, :] for g in range(ng)])
        o_ref[rows, :] = (num * inv[h:h + 1, :]).astype(BF16)


def dil_merge(accs, m, l, tm=512):
    ng = len(accs)
    d, t = accs[0].shape
    tm = _token_tile(t, tm)
    st = pl.BlockSpec((ng, N_HEADS, tm), lambda i: (0, 0, i))
    big = pl.BlockSpec((d, tm), lambda i: (0, i))
    return pl.pallas_call(
        _dil_merge_kernel,
        out_shape=jax.ShapeDtypeStruct((d, t), BF16),
        grid=(t // tm,),
        in_specs=[big] * ng + [st, st],
        out_specs=big,
        compiler_params=_params("parallel"),
        name="dil_merge",
    )(*accs, m, l)


NSA_SEGS = (
    (0, 512, "norm", 0, ATTN_SCALE), (512, 512, "norm", 0, ATTN_SCALE),
    (1024, 512, "plain", 0, 1.0),
    (1536, 256, "norm", 1, 1.0),
    (1792, 256, "plain", 0, 1.0),
    (2048, 256, "norm", 2, 1.0),
    (2304, 256, "plain", 0, 1.0),
    (2560, N_BRANCHES * N_HEADS, "sigmoid", 0, 1.0),
)
KV_SEGS = ((0, 256, "norm", 0, 1.0), (256, 256, "plain", 0, 1.0))
DIL_Q_SEGS = tuple((512 * j, 512, "norm", j // 2, ATTN_SCALE) for j in range(6))


def kv_tiles(xT, nseq, seqlen, t):
    return xT.reshape(KV_HEADS, HEAD_DIM, nseq, seqlen // t, t).transpose(2, 0, 3, 1, 4).astype(BF16)


def nsa_attention_prompt(proj, b, s, rel_bias, cw, k_gain_cmp, tables=None):
    qT, rowsT, gatesT, ksT, vsT, kwT, vwT = proj
    t = ksT.shape[4]
    rows, win = rowsT[:4 * KV_WIDTH], rowsT[4 * KV_WIDTH:]
    c = s // CMP_STRIDE
    ch = rowsT[:2 * KV_WIDTH].astype(BF16).reshape(2, KV_HEADS, HEAD_DIM, b, c, CMP_STRIDE)
    ch = ch.transpose(0, 3, 1, 4, 5, 2).reshape(2, b * KV_HEADS, c, CMP_STRIDE * HEAD_DIM)
    cmp = compress(ch, *cw, k_gain_cmp, bn=4)
    cmp = jnp.flip(cmp, axis=2).reshape(2, b, KV_HEADS, c, HEAD_DIM)
    ckr = cmp[0].astype(BF16)
    cvTr = cmp[1].transpose(0, 1, 3, 2).astype(BF16)
    ocT, mask = cmp_topk(qT, ckr, cvTr, rel_bias, b, s, key_tile=t, tables=tables)
    osT = flash_attention(qT, ksT, vsT, rel_bias, b, s, sel=mask, tables=tables)
    owT = flash_attention(qT, kwT, vwT, rel_bias, b, s, window=NSA_WINDOW, tables=tables)
    oT = nsa_combine(ocT, osT, owT, gatesT)
    return oT, rows, win


def dil_attention_prompt(qallT, kvT, b, s, rel_bias, t=128, tables=None):
    accs, ms, ls = [], [], []
    for gi, (window, dil) in enumerate(DIL_GROUPS):
        length = s // dil

        def split(x):
            r = x.shape[0]
            return x.reshape(r, b, length, dil).transpose(0, 1, 3, 2).reshape(r, b * s)

        def merge(x):
            r = x.shape[0]
            return x.reshape(r, b, dil, length).transpose(0, 1, 3, 2).reshape(r, b * s)

        qg = split((qallT[1024 * gi:1024 * (gi + 1)] * LOG2E).astype(BF16))
        kvg = split(kvT.astype(BF16))
        acc, m, l = band_attention(qg, kv_tiles(kvg[:KV_WIDTH], b * dil, length, t),
                                   kv_tiles(kvg[KV_WIDTH:], b * dil, length, t), rel_bias, b * dil, length,
                                   window // dil, dil, tables=tables)

        accs.append(merge(acc))
        ms.append(merge(m.reshape(N_HEADS, b * s)))
        ls.append(merge(l.reshape(N_HEADS, b * s)))
    return dil_merge(accs, jnp.stack(ms), jnp.stack(ls))


NEW_ROWS = 16


def _group_diagonal(oT, lanes_per_group):
    grp = lax.broadcasted_iota(jnp.int32, (HEAD_DIM, oT.shape[1]), 1) // lanes_per_group
    out = jnp.zeros((HEAD_DIM, oT.shape[1]), F32)
    for g in range(KV_HEADS):
        out = jnp.where(grp == g, oT[g * HEAD_DIM:(g + 1) * HEAD_DIM, :], out)
    return out


def _nsa_sample_kernel(pt_ref, *refs, n_pages, page, n_top, past_len, dt):
    del pt_ref
    pages = refs[:n_pages]
    (win_ref, new_ref, q_ref, gate_ref, wp_ref, perm_ref, w1f_ref, pe_ref, w2_ref, kn_ref, map_ref, gs_ref,
     bc_ref, bs_ref, bw_ref, o_ref, x_s, s_s, v_s) = refs[n_pages:]
    q = q_ref[0]
    lanes = q.shape[1]
    n_ch = past_len // CMP_STRIDE

    pairs = KV_WIDTH // 128
    cpp = page // CMP_STRIDE
    for p in range(n_pages):
        for jj in range(2 * pairs):
            f = pages[p][0, 128 * jj:128 * (jj + 1), :].astype(BF16)
            t = jnp.dot(f, perm_ref[...], preferred_element_type=F32).T
            x_s[jj, :, cpp * p:cpp * (p + 1), :] = t.reshape(CMP_STRIDE, cpp, 128)
    comp = []
    for slot in range(2):
        ab = []
        for hp in range(pairs):
            x = jnp.concatenate([x_s[slot * pairs + hp, l] for l in range(CMP_STRIDE)], axis=1).astype(BF16)
            both = jnp.dot(x, wp_ref[slot], preferred_element_type=F32)
            ab += [both[:, :2 * CMP_HIDDEN], both[:, 2 * CMP_HIDDEN:]]
        w1f = w1f_ref[slot]
        pe = (jnp.dot(pe_ref[slot, 0], w1f[:, :CMP_HIDDEN], preferred_element_type=F32)
              + jnp.dot(pe_ref[slot, 1], w1f[:, CMP_HIDDEN:], preferred_element_type=F32))[0:1]
        hid = jnp.concatenate(
            [a[:, :CMP_HIDDEN] + pltpu.roll(a[:, CMP_HIDDEN:], n_ch - 1, 0) + pe for a in ab], axis=1)
        act = hid / (1.0 + jnp.exp(-hid))
        comp.append(jnp.dot(act.astype(BF16), w2_ref[slot], preferred_element_type=F32))
    ck = comp[0]
    lane_g = lax.broadcasted_iota(jnp.int32, ck.shape, 1) // HEAD_DIM
    sq = ck * ck
    scale = jnp.zeros_like(ck)
    for g in range(KV_HEADS):
        ms = jnp.sum(jnp.where(lane_g == g, sq, 0.0), axis=1, keepdims=True) * (1.0 / HEAD_DIM)
        scale = jnp.where(lane_g == g, lax.rsqrt(ms + EPS), scale)
    ck = (ck * scale * kn_ref[...]).astype(BF16)
    cv = comp[1].astype(BF16)

    sc = jnp.dot(ck, q, preferred_element_type=F32) + bc_ref[...]
    m = jnp.max(sc, axis=0, keepdims=True)
    m = jnp.where(m < 0.5 * NEG, 0.0, m)
    p = jnp.exp(sc - m)
    p = p * (1.0 / jnp.maximum(jnp.sum(p, axis=0, keepdims=True), 1e-30))
    ocT = _tn_dot(cv, p.astype(BF16))
    hi = p.astype(BF16)
    lo = (p - hi.astype(F32)).astype(BF16)
    m1 = (jnp.dot(map_ref[...], hi, preferred_element_type=F32)
          + jnp.dot(map_ref[...], lo, preferred_element_type=F32))
    hi = m1.astype(BF16)
    lo = (m1 - hi.astype(F32)).astype(BF16)
    imp = (jnp.dot(hi, gs_ref[...], preferred_element_type=F32)
           + jnp.dot(lo, gs_ref[...], preferred_element_type=F32))
    n_sel_pad = imp.shape[0]
    n_sel = -(-(past_len + dt) // SEL_BLOCK)
    j = lax.broadcasted_iota(jnp.int32, (n_sel_pad, lanes), 0)
    qpos = past_len + lax.broadcasted_iota(jnp.int32, (n_sel_pad, lanes), 1) % dt
    cur = qpos // SEL_BLOCK
    forced = (j == 0) | (j == cur) | (j == cur - 1)
    future = (j * SEL_BLOCK > qpos) | (j >= n_sel)
    v = jnp.where(forced, jnp.inf, jnp.where(future, -jnp.inf, imp))
    v_s[...] = v

    def count(a, cnt):
        va = v_s[pl.ds(a, 1), :]
        ahead = (va > v) | ((va == v) & (j > a))
        return cnt + jnp.where(ahead, 1.0, 0.0)

    rank = lax.fori_loop(0, n_sel, count, jnp.zeros((n_sel_pad, lanes), F32))
    sel = jnp.where((rank < n_top) & (j < n_sel), 0.0, NEG)

    blocks_per_page = page // SEL_BLOCK
    m_run = jnp.full((page, lanes), NEG, F32)
    for pg in range(n_pages):
        kT = pages[pg][0, 2 * KV_WIDTH:3 * KV_WIDTH, :].astype(BF16)
        mask = jnp.concatenate(
            [jnp.broadcast_to(sel[blocks_per_page * pg + bb:blocks_per_page * pg + bb + 1], (SEL_BLOCK, lanes))
             for bb in range(blocks_per_page)], axis=0)
        s = _tn_dot(kT, q) + bs_ref[pg * page:(pg + 1) * page, :] + mask
        s_s[pg * page:(pg + 1) * page, :] = s
        m_run = jnp.maximum(m_run, s)
    nb_past = past_len // SEL_BLOCK
    s_new = (jnp.dot(new_ref[0, 0], q, preferred_element_type=F32) + bs_ref[past_len:past_len + NEW_ROWS, :]
             + sel[nb_past:nb_past + 1])
    m = jnp.maximum(jnp.max(m_run, axis=0, keepdims=True), jnp.max(s_new, axis=0, keepdims=True))
    pn = jnp.exp(s_new - m)
    den = jnp.sum(pn, axis=0, keepdims=True)
    acc = _tn_dot(new_ref[0, 1], pn.astype(BF16))
    for pg in range(n_pages):
        pp = jnp.exp(s_s[pg * page:(pg + 1) * page, :] - m)
        den = den + jnp.sum(pp, axis=0, keepdims=True)
        acc = acc + jnp.dot(pages[pg][0, 3 * KV_WIDTH:4 * KV_WIDTH, :].astype(BF16), pp.astype(BF16),
                            preferred_element_type=F32)
    osT = acc * (1.0 / den)

    wb = win_ref.shape[2]
    s_w = _tn_dot(win_ref[0, 0:KV_WIDTH, :].astype(BF16), q) + bw_ref[0:wb, :]
    s_n = jnp.dot(new_ref[0, 2], q, preferred_element_type=F32) + bw_ref[wb:wb + NEW_ROWS, :]
    m = jnp.maximum(jnp.max(s_w, axis=0, keepdims=True), jnp.max(s_n, axis=0, keepdims=True))
    pw = jnp.exp(s_w - m)
    pn = jnp.exp(s_n - m)
    den = jnp.sum(pw, axis=0, keepdims=True) + jnp.sum(pn, axis=0, keepdims=True)
    acc = (jnp.dot(win_ref[0, KV_WIDTH:2 * KV_WIDTH, :].astype(BF16), pw.astype(BF16), preferred_element_type=F32)
           + _tn_dot(new_ref[0, 3], pn.astype(BF16)))
    owT = acc * (1.0 / den)

    lpg = lanes // KV_HEADS
    o_ref[0] = (gate_ref[0, 0:1, :] * _group_diagonal(ocT, lpg)
                + gate_ref[0, 1:2, :] * _group_diagonal(osT, lpg)
                + gate_ref[0, 2:3, :] * _group_diagonal(owT, lpg))


def _lane_table(rel_bias, dist, valid, dt):
    lanes = np.arange(N_HEADS * dt)
    h, t = lanes // dt, lanes % dt
    return bias_lookup(dist[:, t], valid[:, t], rel_bias[:, jnp.asarray(h)][None])[0]


def nsa_sample_tables(rel_bias, past_len, wb, dt):
    t = np.arange(dt)[None, :]
    n_ch = past_len // CMP_STRIDE
    c = np.arange(n_ch)[:, None]
    dist = past_len + t - (CMP_STRIDE * c + CMP_LEN - 1)
    bc = _lane_table(rel_bias, dist, (c < n_ch - 1) & (dist >= 0), dt)
    s = np.arange(past_len)[:, None]
    n = np.arange(NEW_ROWS)[:, None]
    dist = np.concatenate([past_len + t - s, t - n])
    valid = np.concatenate([np.ones((past_len, dt), bool), (n < dt) & (t - n >= 0)])
    bs = _lane_table(rel_bias, dist, valid, dt)
    l = np.arange(wb)[:, None]
    dist = np.concatenate([wb + t - l, t - n])
    valid = np.concatenate([wb + t - l <= NSA_WINDOW, (n < dt) & (t - n >= 0)])
    bw = _lane_table(rel_bias, dist, valid, dt)
    return bc, bs, bw


def nsa_sample_constants(past_len, dt):
    n_ch = past_len // CMP_STRIDE
    n_sel = -(-(past_len + dt) // SEL_BLOCK)
    n_sel_pad = -(-n_sel // 8) * 8
    c = np.arange(n_ch)[None, :]
    j0 = np.arange(n_sel_pad)[:, None] * SEL_BLOCK
    c_start, c_end = CMP_STRIDE * c, CMP_STRIDE * c + CMP_LEN - 1
    sel_map = (c_start <= j0 + SEL_BLOCK - 1) & (c_end >= j0) & (c < n_ch - 1) & (j0 < n_sel * SEL_BLOCK)
    lanes = np.arange(N_HEADS * dt)
    grp, t = lanes // (REP * dt), lanes % dt
    gsum = (grp[:, None] == grp[None, :]) & (t[:, None] == t[None, :])
    return (jnp.asarray(sel_map.astype(np.float32), dtype=BF16), jnp.asarray(gsum.astype(np.float32), dtype=BF16))


def nsa_sample_attention(cache_pages, page_idx, win_cache, win_base, new_rows, qbd, gates, cw, k_gain_cmp, tables):
    db, n_pages = page_idx.shape
    page = cache_pages.shape[2]
    past_len = n_pages * page
    lanes = qbd.shape[2]
    dt = lanes // N_HEADS
    wb = win_cache.shape[2]
    w1ab, pe, w2 = cw
    w1l = w1ab.reshape(2, CMP_STRIDE, HEAD_DIM, 2 * CMP_HIDDEN)
    wp = jnp.einsum('sldk,gh->slgdhk', w1l, jnp.eye(2, dtype=w1l.dtype)).reshape(
        2, CMP_STRIDE * 128, 4 * CMP_HIDDEN)
    tok = np.arange(page)
    perm = np.zeros((page, page), np.float32)
    perm[tok, (tok % CMP_STRIDE) * (page // CMP_STRIDE) + tok // CMP_STRIDE] = 1.0
    perm = jnp.asarray(perm, dtype=BF16)
    w2bd = jnp.einsum('shd,gk->sghkd', w2, jnp.eye(KV_HEADS, dtype=w2.dtype)).reshape(
        2, KV_HEADS * CMP_HIDDEN, KV_WIDTH)
    kn = jnp.tile(k_gain_cmp.reshape(1, HEAD_DIM), (1, KV_HEADS))
    sel_map, gsum = nsa_sample_constants(past_len, dt)
    bc, bs, bw = tables
    n_sel = -(-(past_len + dt) // SEL_BLOCK)
    assert past_len % SEL_BLOCK == 0 and dt <= SEL_BLOCK and page % SEL_BLOCK == 0

    def const(a):
        nd = a.ndim
        return pl.BlockSpec(a.shape, lambda i, pt: (0,) * nd)

    page_specs = [pl.BlockSpec((1, cache_pages.shape[1], page), functools.partial(lambda p, i, pt: (pt[i, p], 0, 0), p))
                  for p in range(n_pages)]
    consts = [wp, perm, w1ab, pe, w2bd, kn, sel_map, gsum, bc, bs, bw]
    grid_spec = pltpu.PrefetchScalarGridSpec(
        num_scalar_prefetch=1,
        grid=(db,),
        in_specs=page_specs + [
            pl.BlockSpec((1, win_cache.shape[1], wb), lambda i, pt: (win_base + i, 0, 0)),
            pl.BlockSpec((1,) + new_rows.shape[1:], lambda i, pt: (i, 0, 0, 0)),
            pl.BlockSpec((1,) + qbd.shape[1:], lambda i, pt: (i, 0, 0)),
            pl.BlockSpec((1,) + gates.shape[1:], lambda i, pt: (i, 0, 0)),
        ] + [const(a) for a in consts],
        out_specs=pl.BlockSpec((1, HEAD_DIM, lanes), lambda i, pt: (i, 0, 0)),
        scratch_shapes=[pltpu.VMEM((2 * KV_WIDTH // 128, CMP_STRIDE, past_len // CMP_STRIDE, 128), F32),
                        pltpu.VMEM((past_len, lanes), F32),
                        pltpu.VMEM((sel_map.shape[0], lanes), F32)],
    )
    return pl.pallas_call(
        functools.partial(_nsa_sample_kernel, n_pages=n_pages, page=page, n_top=min(SEL_TOP_N, n_sel),
                          past_len=past_len, dt=dt),
        out_shape=jax.ShapeDtypeStruct((db, HEAD_DIM, lanes), F32),
        grid_spec=grid_spec,
        compiler_params=_params("parallel"),
        name="nsa_sample",
    )(page_idx, *([cache_pages] * n_pages), win_cache, new_rows, qbd, gates, *consts)


def _dil_sample_kernel(kv_ref, new_ref, q_ref, b0_ref, b1_ref, b2_ref, o_ref, s_s, *, los, buf_len):
    lanes = q_ref.shape[3]
    chunk = 128
    ms, dens, accs = [], [], []
    for gi, (bias_ref, lo) in enumerate(zip((b0_ref, b1_ref, b2_ref), los)):
        q = q_ref[0, gi]
        n_chunks = (buf_len - lo) // chunk
        m_run = jnp.full((chunk, lanes), NEG, F32)
        for c in range(n_chunks):
            r0 = lo + c * chunk
            s = (_tn_dot(kv_ref[0, 0:KV_WIDTH, r0:r0 + chunk].astype(BF16), q)
                 + bias_ref[c * chunk:(c + 1) * chunk, :])
            s_s[c * chunk:(c + 1) * chunk, :] = s
            m_run = jnp.maximum(m_run, s)
        s_new = (jnp.dot(new_ref[0, 0], q, preferred_element_type=F32)
                 + bias_ref[n_chunks * chunk:n_chunks * chunk + NEW_ROWS, :])
        m = jnp.maximum(jnp.max(m_run, axis=0, keepdims=True), jnp.max(s_new, axis=0, keepdims=True))
        pn = jnp.exp(s_new - m)
        den = jnp.sum(pn, axis=0, keepdims=True)
        acc = _tn_dot(new_ref[0, 1], pn.astype(BF16))
        for c in range(n_chunks):
            r0 = lo + c * chunk
            pp = jnp.exp(s_s[c * chunk:(c + 1) * chunk, :] - m)
            den = den + jnp.sum(pp, axis=0, keepdims=True)
            acc = acc + jnp.dot(kv_ref[0, KV_WIDTH:2 * KV_WIDTH, r0:r0 + chunk].astype(BF16), pp.astype(BF16),
                                preferred_element_type=F32)
        ms.append(m)
        dens.append(den)
        accs.append(acc)
    mx = functools.reduce(jnp.maximum, ms)
    w = [jnp.exp(m - mx) for m in ms]
    num = functools.reduce(jnp.add, [wg * a for wg, a in zip(w, accs)])
    den = functools.reduce(jnp.add, [wg * d for wg, d in zip(w, dens)])
    o_ref[0] = _group_diagonal(num * (1.0 / den), lanes // KV_HEADS)


def dil_sample_tables(rel_bias, buf_len, dt):
    t = np.arange(dt)[None, :]
    n = np.arange(NEW_ROWS)[:, None]
    tables, los = [], []
    for window, dil in DIL_GROUPS:
        lo = max(0, buf_len - window) // 128 * 128
        s = np.arange(lo, buf_len)[:, None]
        dist = np.concatenate([buf_len + t - s, t - n])
        valid = np.concatenate([np.ones((buf_len - lo, dt), bool), (n < dt) & (t - n >= 0)])
        valid = valid & (dist % dil == 0) & (dist <= window)
        tables.append(_lane_table(rel_bias, dist, valid, dt))
        los.append(lo)
    return tables, tuple(los)


def dil_sample_attention(kv_cache, new_rows, qbd3, tables, los):
    db, width, buf_len = kv_cache.shape
    lanes = qbd3.shape[3]
    assert buf_len % 128 == 0
    tspec = [pl.BlockSpec(tb.shape, lambda i: (0, 0)) for tb in tables]
    return pl.pallas_call(
        functools.partial(_dil_sample_kernel, los=los, buf_len=buf_len),
        out_shape=jax.ShapeDtypeStruct((db, HEAD_DIM, lanes), F32),
        grid=(db,),
        in_specs=[
            pl.BlockSpec((1, width, buf_len), lambda i: (i, 0, 0)),
            pl.BlockSpec((1,) + new_rows.shape[1:], lambda i: (i, 0, 0, 0)),
            pl.BlockSpec((1,) + qbd3.shape[1:], lambda i: (i, 0, 0, 0)),
        ] + tspec,
        out_specs=pl.BlockSpec((1, HEAD_DIM, lanes), lambda i: (i, 0, 0)),
        scratch_shapes=[pltpu.VMEM((buf_len, lanes), F32)],
        compiler_params=_params("parallel"),
        name="dil_sample",
    )(kv_cache, new_rows, qbd3, *tables)


def lane_queries(qT, db, dt):
    q = qT.reshape(KV_HEADS, REP, HEAD_DIM, db, dt).transpose(3, 0, 2, 1, 4).reshape(db, KV_HEADS, HEAD_DIM, REP * dt)
    eye = jnp.eye(KV_HEADS, dtype=q.dtype)
    qbd = q[:, :, :, None, :] * eye[None, :, None, :, None]
    return qbd.reshape(db, KV_WIDTH, KV_HEADS * REP * dt).astype(BF16)


def lanes_to_features(o, db, dt):
    o = o.reshape(db, HEAD_DIM, KV_HEADS, REP, dt).transpose(2, 3, 1, 0, 4)
    return o.reshape(N_HEADS * HEAD_DIM, db * dt)


def token_major(xT, slots, b, s, last):
    x = xT.reshape(xT.shape[0], slots, KV_HEADS, HEAD_DIM, b, s)[..., s - last:]
    return x.transpose(0, 4, 5, 1, 2, 3)


def new_token_rows(zT_rows, db, dt):
    n = zT_rows.shape[0] // KV_WIDTH
    r = zT_rows.reshape(n, KV_WIDTH, db, dt).transpose(2, 0, 3, 1)
    return jnp.pad(r, ((0, 0), (0, 0), (0, NEW_ROWS - dt), (0, 0))).astype(BF16)


def kernel(x_prompt, x_sample, cache_nsa_kv, cache_win_kv, cache_dil_kv, page_table, rel_bias, a_attn_norm, a_w_in, a_q_norm, a_k_norm, a_cmp_pe, a_cmp_w1, a_cmp_w2, a_w_out, kv_norm, w_kv_shared, k_norm_shared, b_attn_norm, b_w_q, b_q_norm, b_w_out, mlp_norm, mlp_w1, mlp_w2):
    b, s, d = x_prompt.shape
    db, dt, _ = x_sample.shape
    n_a = a_w_in.shape[0]
    n_b = b_w_q.shape[0]
    n_pool, page = cache_nsa_kv.shape[1:3]
    wb = cache_win_kv.shape[2]
    buf_len = cache_dil_kv.shape[1]
    past_len = page_table.shape[1] * page
    assert N_HEADS * dt == 128
    tables = {}
    xpT = x_prompt.reshape(b * s, d).T
    xsT = x_sample.reshape(db * dt, d).T
    cache_pages = cache_nsa_kv.transpose(0, 1, 3, 4, 5, 2).reshape(n_a * n_pool, -1, page)
    win_cache = cache_win_kv.transpose(0, 1, 3, 4, 5, 2).reshape(n_a * db, -1, wb)
    nsa_tables = nsa_sample_tables(rel_bias, past_len, wb, dt)
    rows_p, rows_s, wins_p, wins_s = [], [], [], []
    for l in range(n_a + n_b):
        w1T = mlp_w1[l].T.astype(BF16)
        w2T = mlp_w2[l].T.astype(BF16)
        if l < n_a:
            w_inT = a_w_in[l].T.astype(BF16)
            norms = jnp.stack([a_q_norm[l], a_k_norm[l][1], a_k_norm[l][2]])[..., None]
            cw = compress_weights(a_cmp_pe[l], a_cmp_w1[l], a_cmp_w2[l])
            woT = a_w_out[l].T.astype(BF16)
            proj = nsa_proj_prompt(xpT, a_attn_norm[l], w_inT, norms, b, s)
            oT, rows, win = nsa_attention_prompt(proj, b, s, rel_bias, cw, a_k_norm[l][0], tables=tables)
            rows_p.append(rows)
            wins_p.append(win)
            xpT = outproj_mlp(xpT, oT, woT, mlp_norm[l], w1T, w2T)
            zs = norm_proj(xsT, a_attn_norm[l], w_inT, norms, NSA_SEGS)
            gates = zs[2560:2560 + N_BRANCHES * N_HEADS].reshape(N_BRANCHES, KV_HEADS, REP, db, dt)
            gates = gates.transpose(3, 0, 1, 2, 4).reshape(db, N_BRANCHES, N_HEADS * dt)
            o = nsa_sample_attention(
                cache_pages, page_table + l * n_pool, win_cache, l * db, new_token_rows(zs[1536:2560], db, dt),
                lane_queries(zs[:1024], db, dt), gates, cw, a_k_norm[l][0], nsa_tables)
            kv6 = zs[1024:2560].T.reshape(db, dt, 6, KV_HEADS, HEAD_DIM)
            rows_s.append(kv6[:, :, 0:4])
            wins_s.append(kv6[:, :, 4:6])
            xsT = outproj_mlp(xsT, lanes_to_features(o, db, dt).astype(BF16), woT, mlp_norm[l], w1T, w2T)
        else:
            i = l - n_a
            if i == 0:
                w_kvT = w_kv_shared.T.astype(BF16)
                kn = k_norm_shared.reshape(1, HEAD_DIM, 1)
                kvpT = norm_proj(xpT, kv_norm, w_kvT, kn, KV_SEGS)
                kvsT = norm_proj(xsT, kv_norm, w_kvT, kn, KV_SEGS)
                kvs = kvsT.T.reshape(db, dt, 2, KV_HEADS, HEAD_DIM)
                dmax = max(w for w, _ in DIL_GROUPS)
                new_dil_p = token_major(kvpT[None], 2, b, s, min(dmax, s))[0]
                new_dil_s = jnp.concatenate([cache_dil_kv, kvs], axis=1)[:, -min(dmax, buf_len + dt):]
                dil_cache = cache_dil_kv.transpose(0, 2, 3, 4, 1).reshape(db, -1, buf_len)
                dil_new = new_token_rows(kvsT, db, dt)
                dil_tables, dil_los = dil_sample_tables(rel_bias, buf_len, dt)
            w_qT = b_w_q[i].T.astype(BF16)
            qn = b_q_norm[i][..., None]
            woT = b_w_out[i].T.astype(BF16)
            qT = norm_proj(xpT, b_attn_norm[i], w_qT, qn, DIL_Q_SEGS)
            oT = dil_attention_prompt(qT, kvpT, b, s, rel_bias, tables=tables)
            xpT = outproj_mlp(xpT, oT, woT, mlp_norm[l], w1T, w2T)
            qs = norm_proj(xsT, b_attn_norm[i], w_qT, qn, DIL_Q_SEGS)
            qbd3 = jnp.stack([lane_queries(qs[1024 * gi:1024 * (gi + 1)], db, dt) for gi in range(len(DIL_GROUPS))], axis=1)
            o = dil_sample_attention(dil_cache, dil_new, qbd3, dil_tables, dil_los)
            xsT = outproj_mlp(xsT, lanes_to_features(o, db, dt).astype(BF16), woT, mlp_norm[l], w1T, w2T)
    return (xpT.T.reshape(b, s, d), xsT.T.reshape(db, dt, d),
            token_major(jnp.stack(rows_p), 4, b, s, s), jnp.stack(rows_s),
            token_major(jnp.stack(wins_p), 2, b, s, min(NSA_WINDOW, s)),
            jnp.concatenate([cache_win_kv, jnp.stack(wins_s)], axis=2)[:, :, -min(NSA_WINDOW, wb + dt):],
            new_dil_p, new_dil_s)
```

```python
import functools
import math

import numpy as np
import jax
import jax.numpy as jnp
from jax import lax
from jax.experimental import pallas as pl
from jax.experimental.pallas import tpu as pltpu

F32 = jnp.float32
BF16 = jnp.bfloat16

D_MODEL = 1024
N_HEADS = 16
HEAD_DIM = 64
KV_HEADS = 4
REP = N_HEADS // KV_HEADS
GROUP_ROWS = REP * HEAD_DIM
KV_WIDTH = KV_HEADS * HEAD_DIM
D_FF = 4 * D_MODEL
EPS = 1e-6
ATTN_SCALE = HEAD_DIM ** -0.5
N_BUCKETS = 32
MAX_DISTANCE = 2048
CMP_LEN = 32
CMP_STRIDE = 16
CMP_HIDDEN = 2 * HEAD_DIM
SEL_BLOCK = 64
SEL_TOP_N = 16
NSA_WINDOW = 512
N_BRANCHES = 3
NSA_IN = N_HEADS * HEAD_DIM + 6 * KV_HEADS * HEAD_DIM + N_BRANCHES * N_HEADS
DIL_GROUPS = ((128, 1), (512, 4), (2048, 16))
PAGE_SIZE = 128

NEG = -1e30
LOG2E = math.log2(math.e)
VMEM_LIMIT = 56 * 1024 * 1024
BIAS_TABLE_LEN = 2048


def _bucket_of_distance(d):
    max_exact = N_BUCKETS // 2
    d = np.maximum(d, 0)
    ratio = np.log(np.maximum(d, 1).astype(np.float32) / np.float32(max_exact)) / np.float32(
        math.log(MAX_DISTANCE / max_exact))
    large = max_exact + (ratio * np.float32(N_BUCKETS - max_exact)).astype(np.int32)
    return np.where(d < max_exact, d, np.minimum(large, N_BUCKETS - 1)).astype(np.int32)


_BUCKETS = _bucket_of_distance(np.arange(BIAS_TABLE_LEN))
BIAS_CONST_FROM = int(np.argmax(_BUCKETS == N_BUCKETS - 1))
assert np.all(_BUCKETS[BIAS_CONST_FROM:] == N_BUCKETS - 1)


def _params(*sem):
    return pltpu.CompilerParams(dimension_semantics=sem, vmem_limit_bytes=VMEM_LIMIT)


def _token_tile(t, tm):
    tm = min(tm, t)
    assert t % tm == 0, (t, tm)
    return tm


def _tn_dot(a, b):
    return lax.dot_general(a, b, (((0,), (0,)), ((), ())), preferred_element_type=F32)


def _bias_lookup_kernel(bkt_ref, tab_ref, o_ref):
    bkt = bkt_ref[...]
    v = jnp.full(bkt.shape, NEG, F32)
    for bb in range(N_BUCKETS):
        v = jnp.where(bkt == bb, tab_ref[0, bb:bb + 1, :], v)
    o_ref[0] = v


def bias_lookup(dist, valid, tab):
    rows, lanes = dist.shape
    bkt = np.where(valid, _BUCKETS[np.clip(dist, 0, BIAS_TABLE_LEN - 1)], N_BUCKETS).astype(np.int32)
    rb = next(r for r in range(min(rows, 512) // 8 * 8, 0, -8) if rows % r == 0)
    nh = tab.shape[0]
    return pl.pallas_call(
        _bias_lookup_kernel,
        out_shape=jax.ShapeDtypeStruct((nh, rows, lanes), F32),
        grid=(rows // rb, nh),
        in_specs=[pl.BlockSpec((rb, lanes), lambda r, h: (r, 0)),
                  pl.BlockSpec((1, N_BUCKETS, lanes), lambda r, h: (h, 0, 0))],
        out_specs=pl.BlockSpec((1, rb, lanes), lambda r, h: (h, r, 0)),
        compiler_params=_params("parallel", "parallel"),
        name="bias_lookup",
    )(jnp.asarray(bkt), tab)


def _cached(tables, key, build):
    if tables is None:
        return build()
    if key not in tables:
        tables[key] = build()
    return tables[key]


def head_rows(rel_bias, lanes):
    return jnp.broadcast_to((rel_bias.T * LOG2E)[:, :, None], (N_HEADS, N_BUCKETS, lanes))


def _proj_kernel(x_ref, g_ref, w_ref, n_ref, o_ref, *, segs):
    x = x_ref[...]
    ms = jnp.mean(x * x, axis=0, keepdims=True)
    xn = (x * lax.rsqrt(ms + EPS) * g_ref[...]).astype(BF16)
    tm = x.shape[1]
    for r0, nr, kind, ni, scale in segs:
        z = jnp.dot(w_ref[r0:r0 + nr, :], xn, preferred_element_type=F32)
        if kind == "norm":
            z3 = z.reshape(nr // HEAD_DIM, HEAD_DIM, tm)
            hs = jnp.mean(z3 * z3, axis=1, keepdims=True)
            z = (z3 * lax.rsqrt(hs + EPS) * n_ref[ni][None] * scale).reshape(nr, tm)
        elif kind == "sigmoid":
            z = 1.0 / (1.0 + jnp.exp(-z))
        o_ref[r0:r0 + nr, :] = z


def norm_proj(xT, gain, wT, norms, segs, tm=512):
    d, t = xT.shape
    tm = _token_tile(t, tm)
    n = wT.shape[0]
    return pl.pallas_call(
        functools.partial(_proj_kernel, segs=tuple(segs)),
        out_shape=jax.ShapeDtypeStruct((n, t), F32),
        grid=(t // tm,),
        in_specs=[
            pl.BlockSpec((d, tm), lambda i: (0, i)),
            pl.BlockSpec((d, 1), lambda i: (0, 0)),
            pl.BlockSpec((n, d), lambda i: (0, 0)),
            pl.BlockSpec(norms.shape, lambda i: (0, 0, 0)),
        ],
        out_specs=pl.BlockSpec((n, tm), lambda i: (0, i)),
        compiler_params=_params("parallel"),
        name="norm_proj",
    )(xT, gain.reshape(d, 1), wT, norms)


def _nsa_proj_kernel(x_ref, g_ref, w_ref, n_ref, q_ref, rows_ref, gate_ref, ks_ref, vs_ref, kw_ref, vw_ref, *, t):
    x = x_ref[...]
    ms = jnp.mean(x * x, axis=0, keepdims=True)
    xn = (x * lax.rsqrt(ms + EPS) * g_ref[...]).astype(BF16)
    tm = x.shape[1]

    def proj(r0, nr):
        return jnp.dot(w_ref[r0:r0 + nr, :], xn, preferred_element_type=F32)

    def head_norm(z, ni, scale):
        z3 = z.reshape(z.shape[0] // HEAD_DIM, HEAD_DIM, tm)
        hs = jnp.mean(z3 * z3, axis=1, keepdims=True)
        return (z3 * lax.rsqrt(hs + EPS) * n_ref[ni][None] * scale).reshape(z.shape)

    half = N_HEADS * HEAD_DIM // 2
    for c in range(2):
        q_ref[c * half:(c + 1) * half, :] = head_norm(proj(c * half, half), 0, ATTN_SCALE * LOG2E).astype(BF16)
    base = N_HEADS * HEAD_DIM
    rows_ref[0:2 * KV_WIDTH, :] = proj(base, 2 * KV_WIDTH)
    kv = [head_norm(proj(base + 2 * KV_WIDTH, KV_WIDTH), 1, 1.0), proj(base + 3 * KV_WIDTH, KV_WIDTH),
          head_norm(proj(base + 4 * KV_WIDTH, KV_WIDTH), 2, 1.0), proj(base + 5 * KV_WIDTH, KV_WIDTH)]
    for idx, z in enumerate(kv):
        rows_ref[(2 + idx) * KV_WIDTH:(3 + idx) * KV_WIDTH, :] = z
    gate_ref[...] = 1.0 / (1.0 + jnp.exp(-proj(base + 6 * KV_WIDTH, N_BRANCHES * N_HEADS)))
    r = lax.broadcasted_iota(jnp.int32, (SEL_ROWS, t), 0)
    col = lax.broadcasted_iota(jnp.int32, (SEL_ROWS, t), 1)
    block_rows = jnp.where(col // SEL_BLOCK == r, 1.0, 0.0).astype(BF16)
    ones_rows = jnp.where(r == 0, 1.0, 0.0).astype(BF16)
    for (ref, extra), z in zip(((ks_ref, block_rows), (vs_ref, ones_rows), (kw_ref, None), (vw_ref, ones_rows)), kv):
        for j in range(tm // t):
            for g in range(KV_HEADS):
                ref[0, g, j, 0:HEAD_DIM, :] = z[g * HEAD_DIM:(g + 1) * HEAD_DIM, j * t:(j + 1) * t].astype(BF16)
                if extra is not None:
                    ref[0, g, j, HEAD_DIM:HEAD_DIM + SEL_ROWS, :] = extra


def nsa_proj_prompt(xT, gain, wT, norms, b, s, t=256, tm=512):
    d, tt = xT.shape
    nps = s // tm
    assert tt == b * s and s % tm == 0 and tm % t == 0
    n = wT.shape[0]
    col = lambda rows, dt: (jax.ShapeDtypeStruct((rows, tt), dt), pl.BlockSpec((rows, tm), lambda i: (0, i)))
    tile = lambda rows: (jax.ShapeDtypeStruct((b, KV_HEADS, s // t, rows, t), BF16),
                         pl.BlockSpec((1, KV_HEADS, tm // t, rows, t), lambda i: (i // nps, 0, i % nps, 0, 0)))
    outs = [col(N_HEADS * HEAD_DIM, BF16), col(6 * KV_WIDTH, F32), col(N_BRANCHES * N_HEADS, F32),
            tile(HEAD_DIM + SEL_ROWS), tile(HEAD_DIM + SEL_ROWS), tile(HEAD_DIM), tile(HEAD_DIM + SEL_ROWS)]
    return pl.pallas_call(
        functools.partial(_nsa_proj_kernel, t=t),
        out_shape=tuple(o[0] for o in outs),
        grid=(tt // tm,),
        in_specs=[
            pl.BlockSpec((d, tm), lambda i: (0, i)),
            pl.BlockSpec((d, 1), lambda i: (0, 0)),
            pl.BlockSpec((n, d), lambda i: (0, 0)),
            pl.BlockSpec(norms.shape, lambda i: (0, 0, 0)),
        ],
        out_specs=tuple(o[1] for o in outs),
        compiler_params=_params("parallel"),
        name="nsa_proj",
    )(xT, gain.reshape(d, 1), wT, norms)


def _outmlp_kernel(x_ref, o_ref, wo_ref, g_ref, w1_ref, w2_ref, y_ref, x1_s, xn_s, acc_s):
    f = pl.program_id(1)

    @pl.when(f == 0)
    def _():
        x1 = x_ref[...] + jnp.dot(wo_ref[...], o_ref[...], preferred_element_type=F32)
        x1_s[...] = x1
        ms = jnp.mean(x1 * x1, axis=0, keepdims=True)
        xn_s[...] = (x1 * lax.rsqrt(ms + EPS) * g_ref[...]).astype(BF16)
        acc_s[...] = jnp.zeros_like(acc_s)

    h = jnp.maximum(jnp.dot(w1_ref[...], xn_s[...], preferred_element_type=F32), 0.0)
    acc_s[...] += jnp.dot(w2_ref[...], (h * h).astype(BF16), preferred_element_type=F32)

    @pl.when(f == pl.num_programs(1) - 1)
    def _():
        y_ref[...] = x1_s[...] + acc_s[...]


def outproj_mlp(xT, oT, woT, gain, w1T, w2T, tm=1024, tf=1024):
    d, t = xT.shape
    tm = _token_tile(t, tm)
    dff = w1T.shape[0]
    return pl.pallas_call(
        _outmlp_kernel,
        out_shape=jax.ShapeDtypeStruct((d, t), F32),
        grid=(t // tm, dff // tf),
        in_specs=[
            pl.BlockSpec((d, tm), lambda i, f: (0, i)),
            pl.BlockSpec((d, tm), lambda i, f: (0, i)),
            pl.BlockSpec((d, d), lambda i, f: (0, 0)),
            pl.BlockSpec((d, 1), lambda i, f: (0, 0)),
            pl.BlockSpec((tf, d), lambda i, f: (f, 0)),
            pl.BlockSpec((d, tf), lambda i, f: (0, f)),
        ],
        out_specs=pl.BlockSpec((d, tm), lambda i, f: (0, i)),
        scratch_shapes=[pltpu.VMEM((d, tm), F32), pltpu.VMEM((d, tm), BF16), pltpu.VMEM((d, tm), F32)],
        compiler_params=_params("parallel", "arbitrary"),
        name="outproj_mlp",
    )(xT, oT, woT, gain.reshape(d, 1), w1T, w2T)


def _compress_kernel(ch_ref, w1_ref, pe_ref, w2_ref, kn_ref, o_ref):
    slot = pl.program_id(0)
    bn, c, _ = ch_ref.shape[1:]
    w1 = w1_ref[0]
    ab = jnp.dot(ch_ref[0].reshape(bn * c, CMP_STRIDE * HEAD_DIM), w1, preferred_element_type=F32)
    pe_a = jnp.dot(pe_ref[0, 0], w1[:, :CMP_HIDDEN], preferred_element_type=F32)[0:1]
    pe_b = jnp.dot(pe_ref[0, 1], w1[:, CMP_HIDDEN:], preferred_element_type=F32)[0:1]
    nxt = pltpu.roll(ab[:, CMP_HIDDEN:], bn * c - 1, 0)
    hid = ab[:, :CMP_HIDDEN] + nxt + (pe_a + pe_b)
    act = hid / (1.0 + jnp.exp(-hid))
    out = jnp.dot(act.astype(BF16), w2_ref[0], preferred_element_type=F32)
    ms = jnp.mean(out * out, axis=-1, keepdims=True)
    normed = out * lax.rsqrt(ms + EPS) * kn_ref[...]
    o_ref[0] = jnp.where(slot == 0, normed, out).reshape(bn, c, HEAD_DIM)


def compress(ch, w1ab, pe, w2, k_gain, bn):
    _, nb, c, w = ch.shape
    return pl.pallas_call(
        _compress_kernel,
        out_shape=jax.ShapeDtypeStruct((2, nb, c, HEAD_DIM), F32),
        grid=(2, nb // bn),
        in_specs=[
            pl.BlockSpec((1, bn, c, w), lambda s, i: (s, i, 0, 0)),
            pl.BlockSpec((1, w, 2 * CMP_HIDDEN), lambda s, i: (s, 0, 0)),
            pl.BlockSpec((1, 2, 16, w), lambda s, i: (s, 0, 0, 0)),
            pl.BlockSpec((1, CMP_HIDDEN, HEAD_DIM), lambda s, i: (s, 0, 0)),
            pl.BlockSpec((1, HEAD_DIM), lambda s, i: (0, 0)),
        ],
        out_specs=pl.BlockSpec((1, bn, c, HEAD_DIM), lambda s, i: (s, i, 0, 0)),
        compiler_params=_params("parallel", "parallel"),
        name="nsa_compress",
    )(ch, w1ab, pe, w2, k_gain.reshape(1, HEAD_DIM))


def compress_weights(cmp_pe, cmp_w1, cmp_w2):
    half = CMP_STRIDE * HEAD_DIM
    w1 = cmp_w1.reshape(2, 2, half, CMP_HIDDEN)
    w1ab = jnp.concatenate([w1[:, 0], w1[:, 1]], axis=-1).astype(BF16)
    pe = jnp.broadcast_to(cmp_pe.reshape(2, 2, 1, half), (2, 2, 16, half)).astype(BF16)
    return w1ab, pe, cmp_w2.astype(BF16)


CMP_CHUNK = 128


def _cmp_topk_kernel(q_ref, ck_ref, cv_ref, map_ref, bias_ref, oc_ref, sel_ref, v_s, s_s, cnt_s, *,
                     tq, n_top, k_const):
    i = pl.program_id(2)
    c = ck_ref.shape[2]
    n_sel = map_ref.shape[0]
    n_chunks = c // CMP_CHUNK
    ms = []
    for r in range(REP):
        q = q_ref[r * HEAD_DIM:(r + 1) * HEAD_DIM, :]
        cm = None
        for jc in range(n_chunks):
            k0 = i * (tq // CMP_STRIDE) + jc * CMP_CHUNK
            start = pl.multiple_of(jnp.minimum(k0, k_const), 8)
            s = (jnp.dot(ck_ref[0, 0, jc * CMP_CHUNK:(jc + 1) * CMP_CHUNK, :], q, preferred_element_type=F32)
                 + bias_ref[r, pl.ds(start, CMP_CHUNK), :])
            s_s[r, jc * CMP_CHUNK:(jc + 1) * CMP_CHUNK, :] = s
            mx = jnp.max(s, axis=0, keepdims=True)
            cm = mx if cm is None else jnp.maximum(cm, mx)
        ms.append(jnp.where(cm < 0.5 * NEG, 0.0, cm))
    psum = None
    for r in range(REP):
        p = jnp.exp2(s_s[r] - ms[r])
        p = p * (1.0 / jnp.maximum(jnp.sum(p, axis=0, keepdims=True), 1e-30))
        oc_ref[r * HEAD_DIM:(r + 1) * HEAD_DIM, :] = jnp.dot(
            cv_ref[0, 0], p.astype(BF16), preferred_element_type=F32).astype(oc_ref.dtype)
        psum = p if psum is None else psum + p
    hi = psum.astype(BF16)
    lo = (psum - hi.astype(F32)).astype(BF16)
    imp = (jnp.dot(map_ref[...], hi, preferred_element_type=F32)
           + jnp.dot(map_ref[...], lo, preferred_element_type=F32))
    qpos = i * tq + lax.broadcasted_iota(jnp.int32, (n_sel, tq), 1)
    j = lax.broadcasted_iota(jnp.int32, (n_sel, tq), 0)
    cur = qpos // SEL_BLOCK
    forced = (j == 0) | (j == cur) | (j == cur - 1)
    future = j * SEL_BLOCK > qpos
    v = jnp.where(forced, jnp.inf, jnp.where(future, -jnp.inf, imp))
    v_s[...] = v
    groups = n_sel // 8
    row = lax.broadcasted_iota(jnp.int32, (8, 128), 0)
    cnt_s[...] = jnp.zeros(cnt_s.shape, F32)
    newest = (i * tq + tq - 1) // SEL_BLOCK
    for ga in range(groups):
        @pl.when(8 * ga <= newest)
        def _(ga=ga):
            for lt in range(tq // 128):
                lanes = slice(lt * 128, (lt + 1) * 128)
                vg = [v_s[8 * gi:8 * gi + 8, lanes] for gi in range(groups)]
                cnt = [cnt_s[8 * gi:8 * gi + 8, lanes] for gi in range(groups)]
                for a in range(8 * ga, 8 * ga + 8):
                    va = v_s[a:a + 1, lanes]
                    for gi in range(groups):
                        if 8 * gi > a:
                            ahead = va >= vg[gi]
                        elif 8 * gi + 7 < a:
                            ahead = va > vg[gi]
                        else:
                            ahead = (va > vg[gi]) | ((va == vg[gi]) & (row > a - 8 * gi))
                        cnt[gi] = cnt[gi] + jnp.where(ahead, 1.0, 0.0)
                for gi in range(groups):
                    cnt_s[8 * gi:8 * gi + 8, lanes] = cnt[gi]

    for lt in range(tq // 128):
        lanes = slice(lt * 128, (lt + 1) * 128)
        bpt = n_sel // sel_ref.shape[2]
        for gi in range(groups):
            mask = jnp.where(cnt_s[8 * gi:8 * gi + 8, lanes] < n_top, 0.0, NEG)
            for hh in range(8 // bpt):
                part = mask if hh == 0 else pltpu.roll(mask, 8 - hh * bpt, 0)
                part = jnp.where(row < bpt, part, 0.0)
                sel_ref[0, 0, gi * (8 // bpt) + hh, :, lanes] = jnp.concatenate(
                    [part, jnp.zeros((SEL_ROWS - 8, 128), F32)], axis=0).astype(BF16)


def cmp_strip(rel_bias, c, tq, rows):
    k = np.arange(rows)[:, None]
    tl = np.arange(tq)[None, :]
    d = CMP_STRIDE * (k - (c - 1)) + tl - (CMP_LEN - 1)
    return bias_lookup(d, d >= 0, head_rows(rel_bias, tq))


def cmp_topk(qT, ckr, cvTr, rel_bias, b, s, tq=256, key_tile=256, tables=None):
    c = s // CMP_STRIDE
    n_sel = s // SEL_BLOCK
    nkt = s // key_tile
    assert 8 % (key_tile // SEL_BLOCK) == 0 and n_sel % 8 == 0
    n_top = min(SEL_TOP_N, n_sel)
    nq = s // tq
    k_const = (c - 1) + -(-(BIAS_CONST_FROM + CMP_LEN - 1) // CMP_STRIDE)
    k_const = -(-k_const // 8) * 8
    strip = _cached(tables, ("cmp", c, tq), lambda: cmp_strip(rel_bias, c, tq, k_const + CMP_CHUNK))
    cc = np.arange(c)[::-1]
    c_end = cc * CMP_STRIDE + CMP_LEN - 1
    c_start = c_end - CMP_LEN + 1
    j0 = np.arange(n_sel)[:, None] * SEL_BLOCK
    sel_map = ((c_start[None] <= j0 + SEL_BLOCK - 1) & (c_end[None] >= j0) & (cc[None] < c - 1))
    sel_map = jnp.asarray(sel_map.astype(np.float32), dtype=BF16)
    return pl.pallas_call(
        functools.partial(_cmp_topk_kernel, tq=tq, n_top=n_top, k_const=k_const),
        out_shape=(jax.ShapeDtypeStruct((N_HEADS * HEAD_DIM, b * s), BF16),
                   jax.ShapeDtypeStruct((b, KV_HEADS, nkt, SEL_ROWS, s), BF16)),
        grid=(KV_HEADS, b, nq),
        in_specs=[
            pl.BlockSpec((GROUP_ROWS, tq), lambda g, n, i: (g, n * nq + i)),
            pl.BlockSpec((1, 1, c, HEAD_DIM), lambda g, n, i: (n, g, 0, 0)),
            pl.BlockSpec((1, 1, HEAD_DIM, c), lambda g, n, i: (n, g, 0, 0)),
            pl.BlockSpec((n_sel, c), lambda g, n, i: (0, 0)),
            pl.BlockSpec((REP, strip.shape[1], tq), lambda g, n, i: (g, 0, 0)),
        ],
        out_specs=(pl.BlockSpec((GROUP_ROWS, tq), lambda g, n, i: (g, n * nq + i)),
                   pl.BlockSpec((1, 1, nkt, SEL_ROWS, tq), lambda g, n, i: (n, g, 0, 0, i))),
        scratch_shapes=[pltpu.VMEM((n_sel, tq), F32), pltpu.VMEM((REP, c, tq), F32), pltpu.VMEM((n_sel, tq), F32)],
        compiler_params=_params("parallel", "parallel", "parallel"),
        name="nsa_cmp_topk",
    )(qT, ckr, cvTr, sel_map, strip)


SEL_ROWS = 16


def _flash_kernel(*refs, tq, n_tiles_max, n_delta, use_sel):
    nh = REP
    if use_sel:
        q_ref, k_ref, v_ref, bias_ref, sel_ref = refs[:5]
        outs = refs[5:]
    else:
        q_ref, k_ref, v_ref, bias_ref = refs[:4]
        sel_ref = None
        outs = refs[4:]
    o_ref = outs[0]
    accs, s_bufs = outs[1:1 + nh], outs[1 + nh:]
    i = pl.program_id(2)
    n_tiles = jnp.minimum(i + 1, n_tiles_max)
    for acc in accs:
        acc[...] = jnp.zeros(acc.shape, F32)

    def scores(jt, s_buf):
        kt = jnp.maximum(i - jt, 0)
        kT = k_ref[0, 0, kt]
        tile = jnp.where(jt < n_tiles, jnp.minimum(jt, n_delta - 1), n_delta)
        cms = []
        for r in range(nh):
            q = q_ref[r * HEAD_DIM:(r + 1) * HEAD_DIM, :]
            if use_sel:
                q = jnp.concatenate([q, sel_ref[0, 0, kt]], axis=0)
            s = _tn_dot(kT, q) + bias_ref[r, tile]
            s_buf[r] = s
            cms.append(jnp.max(s, axis=0, keepdims=True))
        return tuple(cms)

    def accumulate(jt, s_buf, cms, ms):
        v = v_ref[0, 0, jnp.maximum(i - jt, 0)]
        new_m = []
        for r in range(nh):
            m_new = jnp.maximum(ms[r], cms[r])
            alpha = jnp.exp2(ms[r] - m_new)
            p = jnp.exp2((s_buf[r] - m_new).astype(BF16))
            accs[r][...] = alpha * accs[r][...] + jnp.dot(v, p, preferred_element_type=F32)
            new_m.append(m_new)
        return tuple(new_m)

    def pair(jj, carry):
        ms, cm0 = carry
        cm1 = scores(2 * jj + 1, s_bufs[1])
        ms = accumulate(2 * jj, s_bufs[0], cm0, ms)
        cm0 = scores(2 * jj + 2, s_bufs[0])
        ms = accumulate(2 * jj + 1, s_bufs[1], cm1, ms)
        return ms, cm0

    init = (tuple(jnp.full((1, tq), NEG, F32) for _ in range(nh)), scores(0, s_bufs[0]))
    ms, cm0 = lax.fori_loop(0, n_tiles // 2, pair, init)

    @pl.when(n_tiles % 2 == 1)
    def _():
        accumulate(n_tiles - 1, s_bufs[0], cm0, ms)

    for r in range(nh):
        o_ref[r * HEAD_DIM:(r + 1) * HEAD_DIM, :] = (
            accs[r][0:HEAD_DIM, :] * (1.0 / accs[r][HEAD_DIM:HEAD_DIM + 1, :])).astype(o_ref.dtype)


def flash_bias_tiles(rel_bias, t, n_delta, dil, window):
    j = np.arange(n_delta + 1)[:, None, None]
    sl = np.arange(t)[None, :, None]
    tl = np.arange(t)[None, None, :]
    d = j * t + tl - sl
    ok = (d >= 0) & (j < n_delta)
    if window is not None:
        ok = ok & (d <= window)
    tiles = bias_lookup((d * dil).reshape(-1, t), ok.reshape(-1, t), head_rows(rel_bias, t))
    return tiles.reshape(N_HEADS, n_delta + 1, t, t)


def flash_attention(qT, kT, vT, rel_bias, nseq, seqlen, sel=None, window=None, dil=1, tables=None):
    nkt, tk = kT.shape[2], kT.shape[4]
    tq = tk
    assert seqlen == nkt * tk
    nq = nkt
    if window is None:
        n_delta = -(-(BIAS_CONST_FROM + tk - 1) // tk) + 1
        n_tiles_max = nkt
    else:
        n_tiles_max = (window + tk - 1) // tk + 1
        n_delta = n_tiles_max
    tiles = _cached(tables, ("flash", tk, n_delta, dil, window),
                    lambda: flash_bias_tiles(rel_bias, tk, n_delta, dil, window))
    h_rows = qT.shape[0]
    use_sel = sel is not None
    in_specs = [
        pl.BlockSpec((GROUP_ROWS, tq), lambda g, n, i: (g, n * nq + i)),
        pl.BlockSpec((1, 1) + kT.shape[2:], lambda g, n, i: (n, g, 0, 0, 0)),
        pl.BlockSpec((1, 1) + vT.shape[2:], lambda g, n, i: (n, g, 0, 0, 0)),
        pl.BlockSpec((REP, n_delta + 1, tk, tq), lambda g, n, i: (g, 0, 0, 0)),
    ]
    args = [qT, kT, vT, tiles]
    if use_sel:
        in_specs.append(pl.BlockSpec((1, 1) + sel.shape[2:4] + (tq,), lambda g, n, i: (n, g, 0, 0, i)))
        args.append(sel)
    return pl.pallas_call(
        functools.partial(_flash_kernel, tq=tq, n_tiles_max=n_tiles_max, n_delta=n_delta, use_sel=use_sel),
        out_shape=jax.ShapeDtypeStruct((h_rows, nseq * seqlen), BF16),
        grid=(KV_HEADS, nseq, nq),
        in_specs=in_specs,
        out_specs=pl.BlockSpec((GROUP_ROWS, tq), lambda g, n, i: (g, n * nq + i)),
        scratch_shapes=([pltpu.VMEM((vT.shape[3], tq), F32) for _ in range(REP)]
                        + [pltpu.VMEM((REP, tk, tq), F32) for _ in range(2)]),
        compiler_params=_params("parallel", "parallel", "parallel"),
        name="flash_sel" if use_sel else ("flash_band" if window is not None else "flash_causal"),
    )(*args)


def _band_kernel(q_ref, k_ref, v_ref, bias_ref, o_ref, mo_ref, lo_ref, *, t, n_tiles, sub):
    step = pl.program_id(2)
    for u in range(sub):
        i = step * sub + u
        cols = slice(u * t, (u + 1) * t)
        kts = [jnp.maximum(i - jt, 0) for jt in range(n_tiles)]
        tiles = [jnp.where(jt <= i, jt, n_tiles) for jt in range(n_tiles)]
        kTs = [k_ref[0, 0, kt] for kt in kts]
        vTs = [v_ref[0, 0, kt] for kt in kts]
        for r in range(REP):
            rows = slice(r * HEAD_DIM, (r + 1) * HEAD_DIM)
            q = q_ref[rows, cols]
            s = [_tn_dot(kTs[jt], q) + bias_ref[r, tiles[jt]] for jt in range(n_tiles)]
            m = functools.reduce(jnp.maximum, [jnp.max(x, axis=0, keepdims=True) for x in s])
            p = [jnp.exp2(x - m) for x in s]
            den = functools.reduce(jnp.add, [jnp.sum(x, axis=0, keepdims=True) for x in p])
            acc = functools.reduce(jnp.add, [jnp.dot(vTs[jt], p[jt].astype(BF16), preferred_element_type=F32)
                                             for jt in range(n_tiles)])
            o_ref[rows, cols] = acc
            mo_ref[0, r:r + 1, cols] = m
            lo_ref[0, r:r + 1, cols] = den


def band_attention(qT, kT, vT, rel_bias, nseq, seqlen, window, dil, sub=8, tables=None):
    nkt, t = kT.shape[2], kT.shape[4]
    n_tiles = (window + t - 1) // t + 1
    sub = min(sub, nkt)
    assert seqlen == nkt * t and nkt % sub == 0
    nq = nkt // sub
    tiles = _cached(tables, ("flash", t, n_tiles, dil, window),
                    lambda: flash_bias_tiles(rel_bias, t, n_tiles, dil, window))
    o_spec = pl.BlockSpec((GROUP_ROWS, sub * t), lambda g, n, i: (g, n * nq + i))
    st_spec = pl.BlockSpec((1, REP, sub * t), lambda g, n, i: (g, 0, n * nq + i))
    st_shape = jax.ShapeDtypeStruct((KV_HEADS, REP, nseq * seqlen), F32)
    return pl.pallas_call(
        functools.partial(_band_kernel, t=t, n_tiles=n_tiles, sub=sub),
        out_shape=(jax.ShapeDtypeStruct((qT.shape[0], nseq * seqlen), F32), st_shape, st_shape),
        grid=(KV_HEADS, nseq, nq),
        in_specs=[
            pl.BlockSpec((GROUP_ROWS, sub * t), lambda g, n, i: (g, n * nq + i)),
            pl.BlockSpec((1, 1) + kT.shape[2:], lambda g, n, i: (n, g, 0, 0, 0)),
            pl.BlockSpec((1, 1) + vT.shape[2:], lambda g, n, i: (n, g, 0, 0, 0)),
            pl.BlockSpec((REP, n_tiles + 1, t, t), lambda g, n, i: (g, 0, 0, 0)),
        ],
        out_specs=(o_spec, st_spec, st_spec),
        compiler_params=_params("parallel", "parallel", "parallel"),
        name="band_attention",
    )(qT, kT, vT, tiles)


def _nsa_combine_kernel(oc_ref, os_ref, ow_ref, g_ref, o_ref):
    for h in range(N_HEADS):
        rows = slice(h * HEAD_DIM, (h + 1) * HEAD_DIM)
        o = (g_ref[h:h + 1, :] * oc_ref[rows, :]
             + g_ref[N_HEADS + h:N_HEADS + h + 1, :] * os_ref[rows, :]
             + g_ref[2 * N_HEADS + h:2 * N_HEADS + h + 1, :] * ow_ref[rows, :])
        o_ref[rows, :] = o.astype(BF16)


def nsa_combine(ocT, osT, owT, gatesT, tm=512):
    d, t = ocT.shape
    tm = _token_tile(t, tm)
    big = pl.BlockSpec((d, tm), lambda i: (0, i))
    return pl.pallas_call(
        _nsa_combine_kernel,
        out_shape=jax.ShapeDtypeStruct((d, t), BF16),
        grid=(t // tm,),
        in_specs=[big, big, big, pl.BlockSpec((N_BRANCHES * N_HEADS, tm), lambda i: (0, i))],
        out_specs=big,
        compiler_params=_params("parallel"),
        name="nsa_combine",
    )(ocT, osT, owT, gatesT)


def _dil_merge_kernel(*refs):
    ng = (len(refs) - 3)
    acc_refs, (m_ref, l_ref, o_ref) = refs[:ng], refs[ng:]
    mx = functools.reduce(jnp.maximum, [m_ref[g] for g in range(ng)])
    w = [jnp.exp2(m_ref[g] - mx) for g in range(ng)]
    den = functools.reduce(jnp.add, [w[g] * l_ref[g] for g in range(ng)])
    inv = 1.0 / den
    for h in range(N_HEADS):
        rows = slice(h * HEAD_DIM, (h + 1) * HEAD_DIM)
        num = functools.reduce(jnp.add, [w[g][h:h + 1, :] * acc_refs[g][rows, :] for g in range(ng)])
        o_ref[rows, :] = (num * inv[h:h + 1, :]).astype(BF16)


def dil_merge(accs, m, l, tm=512):
    ng = len(accs)
    d, t = accs[0].shape
    tm = _token_tile(t, tm)
    st = pl.BlockSpec((ng, N_HEADS, tm), lambda i: (0, 0, i))
    big = pl.BlockSpec((d, tm), lambda i: (0, i))
    return pl.pallas_call(
        _dil_merge_kernel,
        out_shape=jax.ShapeDtypeStruct((d, t), BF16),
        grid=(t // tm,),
        in_specs=[big] * ng + [st, st],
        out_specs=big,
        compiler_params=_params("parallel"),
        name="dil_merge",
    )(*accs, m, l)


NSA_SEGS = (
    (0, 512, "norm", 0, ATTN_SCALE), (512, 512, "norm", 0, ATTN_SCALE),
    (1024, 512, "plain", 0, 1.0),
    (1536, 256, "norm", 1, 1.0),
    (1792, 256, "plain", 0, 1.0),
    (2048, 256, "norm", 2, 1.0),
    (2304, 256, "plain", 0, 1.0),
    (2560, N_BRANCHES * N_HEADS, "sigmoid", 0, 1.0),
)
KV_SEGS = ((0, 256, "norm", 0, 1.0), (256, 256, "plain", 0, 1.0))
DIL_Q_SEGS = tuple((512 * j, 512, "norm", j // 2, ATTN_SCALE) for j in range(6))


def kv_tiles(xT, nseq, seqlen, t):
    return xT.reshape(KV_HEADS, HEAD_DIM, nseq, seqlen // t, t).transpose(2, 0, 3, 1, 4).astype(BF16)


def nsa_attention_prompt(proj, b, s, rel_bias, cw, k_gain_cmp, tables=None):
    qT, rowsT, gatesT, ksT, vsT, kwT, vwT = proj
    t = ksT.shape[4]
    rows, win = rowsT[:4 * KV_WIDTH], rowsT[4 * KV_WIDTH:]
    c = s // CMP_STRIDE
    ch = rowsT[:2 * KV_WIDTH].astype(BF16).reshape(2, KV_HEADS, HEAD_DIM, b, c, CMP_STRIDE)
    ch = ch.transpose(0, 3, 1, 4, 5, 2).reshape(2, b * KV_HEADS, c, CMP_STRIDE * HEAD_DIM)
    cmp = compress(ch, *cw, k_gain_cmp, bn=4)
    cmp = jnp.flip(cmp, axis=2).reshape(2, b, KV_HEADS, c, HEAD_DIM)
    ckr = cmp[0].astype(BF16)
    cvTr = cmp[1].transpose(0, 1, 3, 2).astype(BF16)
    ocT, mask = cmp_topk(qT, ckr, cvTr, rel_bias, b, s, key_tile=t, tables=tables)
    osT = flash_attention(qT, ksT, vsT, rel_bias, b, s, sel=mask, tables=tables)
    owT = flash_attention(qT, kwT, vwT, rel_bias, b, s, window=NSA_WINDOW, tables=tables)
    oT = nsa_combine(ocT, osT, owT, gatesT)
    return oT, rows, win


def dil_attention_prompt(qallT, kvT, b, s, rel_bias, t=128, tables=None):
    accs, ms, ls = [], [], []
    for gi, (window, dil) in enumerate(DIL_GROUPS):
        length = s // dil

        def split(x):
            r = x.shape[0]
            return x.reshape(r, b, length, dil).transpose(0, 1, 3, 2).reshape(r, b * s)

        def merge(x):
            r = x.shape[0]
            return x.reshape(r, b, dil, length).transpose(0, 1, 3, 2).reshape(r, b * s)

        qg = split((qallT[1024 * gi:1024 * (gi + 1)] * LOG2E).astype(BF16))
        kvg = split(kvT.astype(BF16))
        acc, m, l = band_attention(qg, kv_tiles(kvg[:KV_WIDTH], b * dil, length, t),
                                   kv_tiles(kvg[KV_WIDTH:], b * dil, length, t), rel_bias, b * dil, length,
                                   window // dil, dil, tables=tables)

        accs.append(merge(acc))
        ms.append(merge(m.reshape(N_HEADS, b * s)))
        ls.append(merge(l.reshape(N_HEADS, b * s)))
    return dil_merge(accs, jnp.stack(ms), jnp.stack(ls))


NEW_ROWS = 16


def _group_diagonal(oT, lanes_per_group):
    grp = lax.broadcasted_iota(jnp.int32, (HEAD_DIM, oT.shape[1]), 1) // lanes_per_group
    out = jnp.zeros((HEAD_DIM, oT.shape[1]), F32)
    for g in range(KV_HEADS):
        out = jnp.where(grp == g, oT[g * HEAD_DIM:(g + 1) * HEAD_DIM, :], out)
    return out


def _nsa_sample_kernel(pt_ref, *refs, n_pages, page, n_top, past_len, dt):
    del pt_ref
    pages = refs[:n_pages]
    (win_ref, new_ref, q_ref, gate_ref, wp_ref, perm_ref, w1f_ref, pe_ref, w2_ref, kn_ref, map_ref, gs_ref,
     bc_ref, bs_ref, bw_ref, o_ref, x_s, s_s, v_s) = refs[n_pages:]
    q = q_ref[0]
    lanes = q.shape[1]
    n_ch = past_len // CMP_STRIDE

    pairs = KV_WIDTH // 128
    cpp = page // CMP_STRIDE
    for p in range(n_pages):
        for jj in range(2 * pairs):
            f = pages[p][0, 128 * jj:128 * (jj + 1), :].astype(BF16)
            t = jnp.dot(f, perm_ref[...], preferred_element_type=F32).T
            x_s[jj, :, cpp * p:cpp * (p + 1), :] = t.reshape(CMP_STRIDE, cpp, 128)
    comp = []
    for slot in range(2):
        ab = []
        for hp in range(pairs):
            x = jnp.concatenate([x_s[slot * pairs + hp, l] for l in range(CMP_STRIDE)], axis=1).astype(BF16)
            both = jnp.dot(x, wp_ref[slot], preferred_element_type=F32)
            ab += [both[:, :2 * CMP_HIDDEN], both[:, 2 * CMP_HIDDEN:]]
        w1f = w1f_ref[slot]
        pe = (jnp.dot(pe_ref[slot, 0], w1f[:, :CMP_HIDDEN], preferred_element_type=F32)
              + jnp.dot(pe_ref[slot, 1], w1f[:, CMP_HIDDEN:], preferred_element_type=F32))[0:1]
        hid = jnp.concatenate(
            [a[:, :CMP_HIDDEN] + pltpu.roll(a[:, CMP_HIDDEN:], n_ch - 1, 0) + pe for a in ab], axis=1)
        act = hid / (1.0 + jnp.exp(-hid))
        comp.append(jnp.dot(act.astype(BF16), w2_ref[slot], preferred_element_type=F32))
    ck = comp[0]
    lane_g = lax.broadcasted_iota(jnp.int32, ck.shape, 1) // HEAD_DIM
    sq = ck * ck
    scale = jnp.zeros_like(ck)
    for g in range(KV_HEADS):
        ms = jnp.sum(jnp.where(lane_g == g, sq, 0.0), axis=1, keepdims=True) * (1.0 / HEAD_DIM)
        scale = jnp.where(lane_g == g, lax.rsqrt(ms + EPS), scale)
    ck = (ck * scale * kn_ref[...]).astype(BF16)
    cv = comp[1].astype(BF16)

    sc = jnp.dot(ck, q, preferred_element_type=F32) + bc_ref[...]
    m = jnp.max(sc, axis=0, keepdims=True)
    m = jnp.where(m < 0.5 * NEG, 0.0, m)
    p = jnp.exp(sc - m)
    p = p * (1.0 / jnp.maximum(jnp.sum(p, axis=0, keepdims=True), 1e-30))
    ocT = _tn_dot(cv, p.astype(BF16))
    hi = p.astype(BF16)
    lo = (p - hi.astype(F32)).astype(BF16)
    m1 = (jnp.dot(map_ref[...], hi, preferred_element_type=F32)
          + jnp.dot(map_ref[...], lo, preferred_element_type=F32))
    hi = m1.astype(BF16)
    lo = (m1 - hi.astype(F32)).astype(BF16)
    imp = (jnp.dot(hi, gs_ref[...], preferred_element_type=F32)
           + jnp.dot(lo, gs_ref[...], preferred_element_type=F32))
    n_sel_pad = imp.shape[0]
    n_sel = -(-(past_len + dt) // SEL_BLOCK)
    j = lax.broadcasted_iota(jnp.int32, (n_sel_pad, lanes), 0)
    qpos = past_len + lax.broadcasted_iota(jnp.int32, (n_sel_pad, lanes), 1) % dt
    cur = qpos // SEL_BLOCK
    forced = (j == 0) | (j == cur) | (j == cur - 1)
    future = (j * SEL_BLOCK > qpos) | (j >= n_sel)
    v = jnp.where(forced, jnp.inf, jnp.where(future, -jnp.inf, imp))
    v_s[...] = v

    def count(a, cnt):
        va = v_s[pl.ds(a, 1), :]
        ahead = (va > v) | ((va == v) & (j > a))
        return cnt + jnp.where(ahead, 1.0, 0.0)

    rank = lax.fori_loop(0, n_sel, count, jnp.zeros((n_sel_pad, lanes), F32))
    sel = jnp.where((rank < n_top) & (j < n_sel), 0.0, NEG)

    blocks_per_page = page // SEL_BLOCK
    m_run = jnp.full((page, lanes), NEG, F32)
    for pg in range(n_pages):
        kT = pages[pg][0, 2 * KV_WIDTH:3 * KV_WIDTH, :].astype(BF16)
        mask = jnp.concatenate(
            [jnp.broadcast_to(sel[blocks_per_page * pg + bb:blocks_per_page * pg + bb + 1], (SEL_BLOCK, lanes))
             for bb in range(blocks_per_page)], axis=0)
        s = _tn_dot(kT, q) + bs_ref[pg * page:(pg + 1) * page, :] + mask
        s_s[pg * page:(pg + 1) * page, :] = s
        m_run = jnp.maximum(m_run, s)
    nb_past = past_len // SEL_BLOCK
    s_new = (jnp.dot(new_ref[0, 0], q, preferred_element_type=F32) + bs_ref[past_len:past_len + NEW_ROWS, :]
             + sel[nb_past:nb_past + 1])
    m = jnp.maximum(jnp.max(m_run, axis=0, keepdims=True), jnp.max(s_new, axis=0, keepdims=True))
    pn = jnp.exp(s_new - m)
    den = jnp.sum(pn, axis=0, keepdims=True)
    acc = _tn_dot(new_ref[0, 1], pn.astype(BF16))
    for pg in range(n_pages):
        pp = jnp.exp(s_s[pg * page:(pg + 1) * page, :] - m)
        den = den + jnp.sum(pp, axis=0, keepdims=True)
        acc = acc + jnp.dot(pages[pg][0, 3 * KV_WIDTH:4 * KV_WIDTH, :].astype(BF16), pp.astype(BF16),
                            preferred_element_type=F32)
    osT = acc * (1.0 / den)

    wb = win_ref.shape[2]
    s_w = _tn_dot(win_ref[0, 0:KV_WIDTH, :].astype(BF16), q) + bw_ref[0:wb, :]
    s_n = jnp.dot(new_ref[0, 2], q, preferred_element_type=F32) + bw_ref[wb:wb + NEW_ROWS, :]
    m = jnp.maximum(jnp.max(s_w, axis=0, keepdims=True), jnp.max(s_n, axis=0, keepdims=True))
    pw = jnp.exp(s_w - m)
    pn = jnp.exp(s_n - m)
    den = jnp.sum(pw, axis=0, keepdims=True) + jnp.sum(pn, axis=0, keepdims=True)
    acc = (jnp.dot(win_ref[0, KV_WIDTH:2 * KV_WIDTH, :].astype(BF16), pw.astype(BF16), preferred_element_type=F32)
           + _tn_dot(new_ref[0, 3], pn.astype(BF16)))
    owT = acc * (1.0 / den)

    lpg = lanes // KV_HEADS
    o_ref[0] = (gate_ref[0, 0:1, :] * _group_diagonal(ocT, lpg)
                + gate_ref[0, 1:2, :] * _group_diagonal(osT, lpg)
                + gate_ref[0, 2:3, :] * _group_diagonal(owT, lpg))


def _lane_table(rel_bias, dist, valid, dt):
    lanes = np.arange(N_HEADS * dt)
    h, t = lanes // dt, lanes % dt
    return bias_lookup(dist[:, t], valid[:, t], rel_bias[:, jnp.asarray(h)][None])[0]


def nsa_sample_tables(rel_bias, past_len, wb, dt):
    t = np.arange(dt)[None, :]
    n_ch = past_len // CMP_STRIDE
    c = np.arange(n_ch)[:, None]
    dist = past_len + t - (CMP_STRIDE * c + CMP_LEN - 1)
    bc = _lane_table(rel_bias, dist, (c < n_ch - 1) & (dist >= 0), dt)
    s = np.arange(past_len)[:, None]
    n = np.arange(NEW_ROWS)[:, None]
    dist = np.concatenate([past_len + t - s, t - n])
    valid = np.concatenate([np.ones((past_len, dt), bool), (n < dt) & (t - n >= 0)])
    bs = _lane_table(rel_bias, dist, valid, dt)
    l = np.arange(wb)[:, None]
    dist = np.concatenate([wb + t - l, t - n])
    valid = np.concatenate([wb + t - l <= NSA_WINDOW, (n < dt) & (t - n >= 0)])
    bw = _lane_table(rel_bias, dist, valid, dt)
    return bc, bs, bw


def nsa_sample_constants(past_len, dt):
    n_ch = past_len // CMP_STRIDE
    n_sel = -(-(past_len + dt) // SEL_BLOCK)
    n_sel_pad = -(-n_sel // 8) * 8
    c = np.arange(n_ch)[None, :]
    j0 = np.arange(n_sel_pad)[:, None] * SEL_BLOCK
    c_start, c_end = CMP_STRIDE * c, CMP_STRIDE * c + CMP_LEN - 1
    sel_map = (c_start <= j0 + SEL_BLOCK - 1) & (c_end >= j0) & (c < n_ch - 1) & (j0 < n_sel * SEL_BLOCK)
    lanes = np.arange(N_HEADS * dt)
    grp, t = lanes // (REP * dt), lanes % dt
    gsum = (grp[:, None] == grp[None, :]) & (t[:, None] == t[None, :])
    return (jnp.asarray(sel_map.astype(np.float32), dtype=BF16), jnp.asarray(gsum.astype(np.float32), dtype=BF16))


def nsa_sample_attention(cache_pages, page_idx, win_cache, win_base, new_rows, qbd, gates, cw, k_gain_cmp, tables):
    db, n_pages = page_idx.shape
    page = cache_pages.shape[2]
    past_len = n_pages * page
    lanes = qbd.shape[2]
    dt = lanes // N_HEADS
    wb = win_cache.shape[2]
    w1ab, pe, w2 = cw
    w1l = w1ab.reshape(2, CMP_STRIDE, HEAD_DIM, 2 * CMP_HIDDEN)
    wp = jnp.einsum('sldk,gh->slgdhk', w1l, jnp.eye(2, dtype=w1l.dtype)).reshape(
        2, CMP_STRIDE * 128, 4 * CMP_HIDDEN)
    tok = np.arange(page)
    perm = np.zeros((page, page), np.float32)
    perm[tok, (tok % CMP_STRIDE) * (page // CMP_STRIDE) + tok // CMP_STRIDE] = 1.0
    perm = jnp.asarray(perm, dtype=BF16)
    w2bd = jnp.einsum('shd,gk->sghkd', w2, jnp.eye(KV_HEADS, dtype=w2.dtype)).reshape(
        2, KV_HEADS * CMP_HIDDEN, KV_WIDTH)
    kn = jnp.tile(k_gain_cmp.reshape(1, HEAD_DIM), (1, KV_HEADS))
    sel_map, gsum = nsa_sample_constants(past_len, dt)
    bc, bs, bw = tables
    n_sel = -(-(past_len + dt) // SEL_BLOCK)
    assert past_len % SEL_BLOCK == 0 and dt <= SEL_BLOCK and page % SEL_BLOCK == 0

    def const(a):
        nd = a.ndim
        return pl.BlockSpec(a.shape, lambda i, pt: (0,) * nd)

    page_specs = [pl.BlockSpec((1, cache_pages.shape[1], page), functools.partial(lambda p, i, pt: (pt[i, p], 0, 0), p))
                  for p in range(n_pages)]
    consts = [wp, perm, w1ab, pe, w2bd, kn, sel_map, gsum, bc, bs, bw]
    grid_spec = pltpu.PrefetchScalarGridSpec(
        num_scalar_prefetch=1,
        grid=(db,),
        in_specs=page_specs + [
            pl.BlockSpec((1, win_cache.shape[1], wb), lambda i, pt: (win_base + i, 0, 0)),
            pl.BlockSpec((1,) + new_rows.shape[1:], lambda i, pt: (i, 0, 0, 0)),
            pl.BlockSpec((1,) + qbd.shape[1:], lambda i, pt: (i, 0, 0)),
            pl.BlockSpec((1,) + gates.shape[1:], lambda i, pt: (i, 0, 0)),
        ] + [const(a) for a in consts],
        out_specs=pl.BlockSpec((1, HEAD_DIM, lanes), lambda i, pt: (i, 0, 0)),
        scratch_shapes=[pltpu.VMEM((2 * KV_WIDTH // 128, CMP_STRIDE, past_len // CMP_STRIDE, 128), F32),
                        pltpu.VMEM((past_len, lanes), F32),
                        pltpu.VMEM((sel_map.shape[0], lanes), F32)],
    )
    return pl.pallas_call(
        functools.partial(_nsa_sample_kernel, n_pages=n_pages, page=page, n_top=min(SEL_TOP_N, n_sel),
                          past_len=past_len, dt=dt),
        out_shape=jax.ShapeDtypeStruct((db, HEAD_DIM, lanes), F32),
        grid_spec=grid_spec,
        compiler_params=_params("parallel"),
        name="nsa_sample",
    )(page_idx, *([cache_pages] * n_pages), win_cache, new_rows, qbd, gates, *consts)


def _dil_sample_kernel(kv_ref, new_ref, q_ref, b0_ref, b1_ref, b2_ref, o_ref, s_s, *, los, buf_len):
    lanes = q_ref.shape[3]
    chunk = 128
    ms, dens, accs = [], [], []
    for gi, (bias_ref, lo) in enumerate(zip((b0_ref, b1_ref, b2_ref), los)):
        q = q_ref[0, gi]
        n_chunks = (buf_len - lo) // chunk
        m_run = jnp.full((chunk, lanes), NEG, F32)
        for c in range(n_chunks):
            r0 = lo + c * chunk
            s = (_tn_dot(kv_ref[0, 0:KV_WIDTH, r0:r0 + chunk].astype(BF16), q)
                 + bias_ref[c * chunk:(c + 1) * chunk, :])
            s_s[c * chunk:(c + 1) * chunk, :] = s
            m_run = jnp.maximum(m_run, s)
        s_new = (jnp.dot(new_ref[0, 0], q, preferred_element_type=F32)
                 + bias_ref[n_chunks * chunk:n_chunks * chunk + NEW_ROWS, :])
        m = jnp.maximum(jnp.max(m_run, axis=0, keepdims=True), jnp.max(s_new, axis=0, keepdims=True))
        pn = jnp.exp(s_new - m)
        den = jnp.sum(pn, axis=0, keepdims=True)
        acc = _tn_dot(new_ref[0, 1], pn.astype(BF16))
        for c in range(n_chunks):
            r0 = lo + c * chunk
            pp = jnp.exp(s_s[c * chunk:(c + 1) * chunk, :] - m)
            den = den + jnp.sum(pp, axis=0, keepdims=True)
            acc = acc + jnp.dot(kv_ref[0, KV_WIDTH:2 * KV_WIDTH, r0:r0 + chunk].astype(BF16), pp.astype(BF16),
                                preferred_element_type=F32)
        ms.append(m)
        dens.append(den)
        accs.append(acc)
    mx = functools.reduce(jnp.maximum, ms)
    w = [jnp.exp(m - mx) for m in ms]
    num = functools.reduce(jnp.add, [wg * a for wg, a in zip(w, accs)])
    den = functools.reduce(jnp.add, [wg * d for wg, d in zip(w, dens)])
    o_ref[0] = _group_diagonal(num * (1.0 / den), lanes // KV_HEADS)


def dil_sample_tables(rel_bias, buf_len, dt):
    t = np.arange(dt)[None, :]
    n = np.arange(NEW_ROWS)[:, None]
    tables, los = [], []
    for window, dil in DIL_GROUPS:
        lo = max(0, buf_len - window) // 128 * 128
        s = np.arange(lo, buf_len)[:, None]
        dist = np.concatenate([buf_len + t - s, t - n])
        valid = np.concatenate([np.ones((buf_len - lo, dt), bool), (n < dt) & (t - n >= 0)])
        valid = valid & (dist % dil == 0) & (dist <= window)
        tables.append(_lane_table(rel_bias, dist, valid, dt))
        los.append(lo)
    return tables, tuple(los)


def dil_sample_attention(kv_cache, new_rows, qbd3, tables, los):
    db, width, buf_len = kv_cache.shape
    lanes = qbd3.shape[3]
    assert buf_len % 128 == 0
    tspec = [pl.BlockSpec(tb.shape, lambda i: (0, 0)) for tb in tables]
    return pl.pallas_call(
        functools.partial(_dil_sample_kernel, los=los, buf_len=buf_len),
        out_shape=jax.ShapeDtypeStruct((db, HEAD_DIM, lanes), F32),
        grid=(db,),
        in_specs=[
            pl.BlockSpec((1, width, buf_len), lambda i: (i, 0, 0)),
            pl.BlockSpec((1,) + new_rows.shape[1:], lambda i: (i, 0, 0, 0)),
            pl.BlockSpec((1,) + qbd3.shape[1:], lambda i: (i, 0, 0, 0)),
        ] + tspec,
        out_specs=pl.BlockSpec((1, HEAD_DIM, lanes), lambda i: (i, 0, 0)),
        scratch_shapes=[pltpu.VMEM((buf_len, lanes), F32)],
        compiler_params=_params("parallel"),
        name="dil_sample",
    )(kv_cache, new_rows, qbd3, *tables)


def lane_queries(qT, db, dt):
    q = qT.reshape(KV_HEADS, REP, HEAD_DIM, db, dt).transpose(3, 0, 2, 1, 4).reshape(db, KV_HEADS, HEAD_DIM, REP * dt)
    eye = jnp.eye(KV_HEADS, dtype=q.dtype)
    qbd = q[:, :, :, None, :] * eye[None, :, None, :, None]
    return qbd.reshape(db, KV_WIDTH, KV_HEADS * REP * dt).astype(BF16)


def lanes_to_features(o, db, dt):
    o = o.reshape(db, HEAD_DIM, KV_HEADS, REP, dt).transpose(2, 3, 1, 0, 4)
    return o.reshape(N_HEADS * HEAD_DIM, db * dt)


def token_major(xT, slots, b, s, last):
    x = xT.reshape(xT.shape[0], slots, KV_HEADS, HEAD_DIM, b, s)[..., s - last:]
    return x.transpose(0, 4, 5, 1, 2, 3)


def new_token_rows(zT_rows, db, dt):
    n = zT_rows.shape[0] // KV_WIDTH
    r = zT_rows.reshape(n, KV_WIDTH, db, dt).transpose(2, 0, 3, 1)
    return jnp.pad(r, ((0, 0), (0, 0), (0, NEW_ROWS - dt), (0, 0))).astype(BF16)


def kernel(x_prompt, x_sample, cache_nsa_kv, cache_win_kv, cache_dil_kv, page_table, rel_bias, a_attn_norm, a_w_in, a_q_norm, a_k_norm, a_cmp_pe, a_cmp_w1, a_cmp_w2, a_w_out, kv_norm, w_kv_shared, k_norm_shared, b_attn_norm, b_w_q, b_q_norm, b_w_out, mlp_norm, mlp_w1, mlp_w2):
    b, s, d = x_prompt.shape
    db, dt, _ = x_sample.shape
    n_a = a_w_in.shape[0]
    n_b = b_w_q.shape[0]
    n_pool, page = cache_nsa_kv.shape[1:3]
    wb = cache_win_kv.shape[2]
    buf_len = cache_dil_kv.shape[1]
    past_len = page_table.shape[1] * page
    assert N_HEADS * dt == 128
    tables = {}
    xpT = x_prompt.reshape(b * s, d).T
    xsT = x_sample.reshape(db * dt, d).T
    cache_pages = cache_nsa_kv.transpose(0, 1, 3, 4, 5, 2).reshape(n_a * n_pool, -1, page)
    win_cache = cache_win_kv.transpose(0, 1, 3, 4, 5, 2).reshape(n_a * db, -1, wb)
    nsa_tables = nsa_sample_tables(rel_bias, past_len, wb, dt)
    rows_p, rows_s, wins_p, wins_s = [], [], [], []
    for l in range(n_a + n_b):
        w1T = mlp_w1[l].T.astype(BF16)
        w2T = mlp_w2[l].T.astype(BF16)
        if l < n_a:
            w_inT = a_w_in[l].T.astype(BF16)
            norms = jnp.stack([a_q_norm[l], a_k_norm[l][1], a_k_norm[l][2]])[..., None]
            cw = compress_weights(a_cmp_pe[l], a_cmp_w1[l], a_cmp_w2[l])
            woT = a_w_out[l].T.astype(BF16)
            proj = nsa_proj_prompt(xpT, a_attn_norm[l], w_inT, norms, b, s)
            oT, rows, win = nsa_attention_prompt(proj, b, s, rel_bias, cw, a_k_norm[l][0], tables=tables)
            rows_p.append(rows)
            wins_p.append(win)
            xpT = outproj_mlp(xpT, oT, woT, mlp_norm[l], w1T, w2T)
            zs = norm_proj(xsT, a_attn_norm[l], w_inT, norms, NSA_SEGS)
            gates = zs[2560:2560 + N_BRANCHES * N_HEADS].reshape(N_BRANCHES, KV_HEADS, REP, db, dt)
            gates = gates.transpose(3, 0, 1, 2, 4).reshape(db, N_BRANCHES, N_HEADS * dt)
            o = nsa_sample_attention(
                cache_pages, page_table + l * n_pool, win_cache, l * db, new_token_rows(zs[1536:2560], db, dt),
                lane_queries(zs[:1024], db, dt), gates, cw, a_k_norm[l][0], nsa_tables)
            kv6 = zs[1024:2560].T.reshape(db, dt, 6, KV_HEADS, HEAD_DIM)
            rows_s.append(kv6[:, :, 0:4])
            wins_s.append(kv6[:, :, 4:6])
            xsT = outproj_mlp(xsT, lanes_to_features(o, db, dt).astype(BF16), woT, mlp_norm[l], w1T, w2T)
        else:
            i = l - n_a
            if i == 0:
                w_kvT = w_kv_shared.T.astype(BF16)
                kn = k_norm_shared.reshape(1, HEAD_DIM, 1)
                kvpT = norm_proj(xpT, kv_norm, w_kvT, kn, KV_SEGS)
                kvsT = norm_proj(xsT, kv_norm, w_kvT, kn, KV_SEGS)
                kvs = kvsT.T.reshape(db, dt, 2, KV_HEADS, HEAD_DIM)
                dmax = max(w for w, _ in DIL_GROUPS)
                new_dil_p = token_major(kvpT[None], 2, b, s, min(dmax, s))[0]
                new_dil_s = jnp.concatenate([cache_dil_kv, kvs], axis=1)[:, -min(dmax, buf_len + dt):]
                dil_cache = cache_dil_kv.transpose(0, 2, 3, 4, 1).reshape(db, -1, buf_len)
                dil_new = new_token_rows(kvsT, db, dt)
                dil_tables, dil_los = dil_sample_tables(rel_bias, buf_len, dt)
            w_qT = b_w_q[i].T.astype(BF16)
            qn = b_q_norm[i][..., None]
            woT = b_w_out[i].T.astype(BF16)
            qT = norm_proj(xpT, b_attn_norm[i], w_qT, qn, DIL_Q_SEGS)
            oT = dil_attention_prompt(qT, kvpT, b, s, rel_bias, tables=tables)
            xpT = outproj_mlp(xpT, oT, woT, mlp_norm[l], w1T, w2T)
            qs = norm_proj(xsT, b_attn_norm[i], w_qT, qn, DIL_Q_SEGS)
            qbd3 = jnp.stack([lane_queries(qs[1024 * gi:1024 * (gi + 1)], db, dt) for gi in range(len(DIL_GROUPS))], axis=1)
            o = dil_sample_attention(dil_cache, dil_new, qbd3, dil_tables, dil_los)
            xsT = outproj_mlp(xsT, lanes_to_features(o, db, dt).astype(BF16), woT, mlp_norm[l], w1T, w2T)
    return (xpT.T.reshape(b, s, d), xsT.T.reshape(db, dt, d),
            token_major(jnp.stack(rows_p), 4, b, s, s), jnp.stack(rows_s),
            token_major(jnp.stack(wins_p), 2, b, s, min(NSA_WINDOW, s)),
            jnp.concatenate([cache_win_kv, jnp.stack(wins_s)], axis=2)[:, :, -min(NSA_WINDOW, wb + dt):],
            new_dil_p, new_dil_s)
```
